```python
import math
import jax
import jax.numpy as jnp
from jax import lax
import numpy as np


D_MODEL = 1024
BATCH = 4
SEQ = 4096
DEPTH = 2

NORM_EPS = 1e-6
GDN_HEADS = 4
GDN_HEAD_DIM = 128
GDN_WIDTH = GDN_HEADS * GDN_HEAD_DIM
GDN_CHUNK = 64
CONV_WIDTH = 4
GDN_COLS = 4 * GDN_WIDTH + 2 * GDN_HEADS
RWKV_HEADS = 8
RWKV_HEAD_DIM = 64
RWKV_WIDTH = RWKV_HEADS * RWKV_HEAD_DIM
DECAY_LORA = 64
AAA_LORA = 64
GATE_LORA = 128
RWKV_COLS = 3 * RWKV_WIDTH + DECAY_LORA + AAA_LORA + GATE_LORA
RWKV_GN_EPS = 64e-5
MIX_WIDTH = GDN_WIDTH + RWKV_WIDTH
LRU_WIDTH = D_MODEL
LRU_BLOCKS = 4
LRU_BLOCK = LRU_WIDTH // LRU_BLOCKS
LRU_C = 8.0
D_FF = -(-8 * D_MODEL // (3 * 256)) * 256

kernel_name = 'hybrid_gdn_rwkv7_rglru_block'


def rms_norm(x, w):
    xf = x.astype(jnp.float32)
    y = xf * lax.rsqrt(jnp.mean(xf * xf, axis=-1, keepdims=True) + NORM_EPS)
    return (y * w.astype(jnp.float32)).astype(x.dtype)


def l2_normalize(x):
    return x * lax.rsqrt(jnp.sum(x * x, axis=-1, keepdims=True) + NORM_EPS)


def adaln_params(c, w, b):
    m = (jax.nn.silu(c) @ w + b)[:, None, :]
    shift, scale, gate = jnp.split(m, 3, axis=-1)
    return shift, scale, gate


def causal_depthwise_conv(x, w):
    width = w.shape[0]
    T = x.shape[1]
    xp = jnp.pad(x, ((0, 0), (width - 1, 0), (0, 0)))
    y = xp[:, 0:T] * w[0]
    for j in range(1, width):
        y = y + xp[:, j:j + T] * w[j]
    return y


def token_shift(x):
    return jnp.pad(x, ((0, 0), (1, 0), (0, 0)))[:, :-1]


def chunk_gated_delta_rule(q, k, v, g, beta):
    Bsz, T, H, Dk = q.shape
    Dv = v.shape[-1]
    C = GDN_CHUNK
    N = T // C

    def chunk(t):
        t = t.reshape((Bsz, N, C, H) + t.shape[3:])
        return jnp.moveaxis(t, 3, 1)

    q = chunk(q) * (Dk ** -0.5)
    k = chunk(k)
    v = chunk(v)
    g = chunk(g)
    beta = chunk(beta)
    G = jnp.cumsum(g, axis=-1)
    causal = jnp.tril(jnp.ones((C, C), dtype=bool))
    strict = jnp.tril(jnp.ones((C, C), dtype=bool), -1)
    decay = jnp.exp(jnp.where(causal, G[..., :, None] - G[..., None, :], -jnp.inf))
    k_beta = k * beta[..., None]
    m = jnp.where(strict, jnp.einsum('bhnik,bhnjk->bhnij', k_beta, k) * decay, 0.0)
    rhs = jnp.concatenate([v * beta[..., None], k_beta * jnp.exp(G)[..., None]], axis=-1)
    sol = lax.linalg.triangular_solve(m + jnp.eye(C, dtype=m.dtype), rhs, left_side=True,
                                      lower=True, unit_diagonal=True)
    u = sol[..., :Dv]
    w = sol[..., Dv:]
    attn = jnp.where(causal, jnp.einsum('bhnik,bhnjk->bhnij', q, k) * decay, 0.0)
    q_dec = q * jnp.exp(G)[..., None]
    k_dec = k * jnp.exp(G[..., -1:] - G)[..., None]
    g_last = jnp.exp(G[..., -1])
    xs = tuple(jnp.moveaxis(t, 2, 0) for t in (q_dec, k_dec, u, w, attn, g_last))

    def step(S, inp):
        qd, kd, uc, wc, ac, gl = inp
        v_new = uc - jnp.einsum('bhck,bhkv->bhcv', wc, S)
        o = jnp.einsum('bhck,bhkv->bhcv', qd, S) + jnp.einsum('bhcs,bhsv->bhcv', ac, v_new)
        S = S * gl[..., None, None] + jnp.einsum('bhck,bhcv->bhkv', kd, v_new)
        return S, o

    S0 = jnp.zeros((Bsz, H, Dk, Dv), dtype=q.dtype)
    _, o = lax.scan(step, S0, xs)
    o = jnp.moveaxis(o, 0, 2)
    return jnp.moveaxis(o, 1, 3).reshape(Bsz, T, H, Dv)


def gated_deltanet_group(cols, conv_w, a_log, dt_bias, norm_w):
    Bsz, T, _ = cols.shape
    W = GDN_WIDTH
    qkv = jax.nn.silu(causal_depthwise_conv(cols[..., :3 * W], conv_w)).astype(jnp.float32)
    z = cols[..., 3 * W:4 * W].astype(jnp.float32).reshape(Bsz, T, GDN_HEADS, GDN_HEAD_DIM)
    alpha = cols[..., 4 * W:4 * W + GDN_HEADS].astype(jnp.float32)
    b = cols[..., 4 * W + GDN_HEADS:].astype(jnp.float32)
    q, k, v = jnp.split(qkv, 3, axis=-1)
    q = l2_normalize(q.reshape(Bsz, T, GDN_HEADS, GDN_HEAD_DIM))
    k = l2_normalize(k.reshape(Bsz, T, GDN_HEADS, GDN_HEAD_DIM))
    v = v.reshape(Bsz, T, GDN_HEADS, GDN_HEAD_DIM)
    beta = jax.nn.sigmoid(b)
    g = -jnp.exp(a_log.astype(jnp.float32)) * jax.nn.softplus(alpha + dt_bias.astype(jnp.float32))
    o = chunk_gated_delta_rule(q, k, v, g, beta)
    o = rms_norm(o, norm_w) * jax.nn.silu(z)
    return o.reshape(Bsz, T, W).astype(cols.dtype)


def rwkv7_group(cols, mu, w0, w2, a0, a2, g2, k_k, k_a, r_k, ln_w, ln_b):
    Bsz, T, _ = cols.shape
    W = RWKV_WIDTH
    H = RWKV_HEADS
    Dh = RWKV_HEAD_DIM
    cf = cols.astype(jnp.float32)
    cf = cf + mu * (token_shift(cf) - cf)
    r = cf[..., :W]
    k = cf[..., W:2 * W]
    v = cf[..., 2 * W:3 * W]
    off = 3 * W
    wd = cf[..., off:off + DECAY_LORA]
    off = off + DECAY_LORA
    ad = cf[..., off:off + AAA_LORA]
    off = off + AAA_LORA
    gd = cf[..., off:off + GATE_LORA]
    w_log = -jax.nn.softplus(-(w0 + jnp.tanh(wd) @ w2)) - 0.5
    decay = jnp.exp(-jnp.exp(w_log))
    a = jax.nn.sigmoid(a0 + ad @ a2)
    g = jax.nn.sigmoid(gd) @ g2

    def heads(t):
        return t.reshape(Bsz, T, H, Dh)

    kk = l2_normalize(heads(k * k_k))
    k = k * (1.0 + (a - 1.0) * k_a)
    r, k, v, decay, a = heads(r), heads(k), heads(v), heads(decay), heads(a)
    kka = kk * a
    xs = tuple(jnp.moveaxis(t, 1, 0) for t in (r, decay, k, v, kk, kka))

    def step(S, inp):
        r_t, w_t, k_t, v_t, kk_t, b_t = inp
        sa = -jnp.einsum('bhvk,bhk->bhv', S, kk_t)
        S = S * w_t[:, :, None, :] + sa[..., None] * b_t[:, :, None, :] + v_t[..., None] * k_t[:, :, None, :]
        return S, jnp.einsum('bhvk,bhk->bhv', S, r_t)

    S0 = jnp.zeros((Bsz, H, Dh, Dh), dtype=jnp.float32)
    _, y = lax.scan(step, S0, xs)
    y = jnp.moveaxis(y, 0, 1)
    mean = jnp.mean(y, axis=-1, keepdims=True)
    var = jnp.mean(jnp.square(y - mean), axis=-1, keepdims=True)
    y = ((y - mean) * lax.rsqrt(var + RWKV_GN_EPS)).reshape(Bsz, T, W) * ln_w + ln_b
    bonus = (jnp.sum(r * k * r_k, axis=-1, keepdims=True) * v).reshape(Bsz, T, W)
    y = (y + bonus) * g
    return y.astype(cols.dtype)


def delta_rwkv_mixer(h, w_in, w_out, gdn_conv_w, gdn_a_log, gdn_dt_bias, gdn_norm_w,
                     rwkv_mu, rwkv_w0, rwkv_w2, rwkv_a0, rwkv_a2, rwkv_g2, rwkv_k_k, rwkv_k_a,
                     rwkv_r_k, rwkv_ln_w, rwkv_ln_b):
    cols = h @ w_in
    out_a = gated_deltanet_group(cols[..., :GDN_COLS], gdn_conv_w, gdn_a_log, gdn_dt_bias, gdn_norm_w)
    out_b = rwkv7_group(cols[..., GDN_COLS:], rwkv_mu, rwkv_w0, rwkv_w2, rwkv_a0, rwkv_a2, rwkv_g2,
                        rwkv_k_k, rwkv_k_a, rwkv_r_k, rwkv_ln_w, rwkv_ln_b)
    return jnp.concatenate([out_a, out_b], axis=-1) @ w_out


def rglru_mixer(h, w_in, conv_w, conv_b, wa, ba, wx, bx, lam, w_out):
    Bsz, T, _ = h.shape
    gate_branch, xb = jnp.split(h @ w_in, 2, axis=-1)
    xb = (causal_depthwise_conv(xb, conv_w) + conv_b).astype(jnp.float32)
    xblk = xb.reshape(Bsz, T, LRU_BLOCKS, LRU_BLOCK)
    r = jax.nn.sigmoid(jnp.einsum('btnd,nde->btne', xblk, wa).reshape(Bsz, T, LRU_WIDTH) + ba)
    i = jax.nn.sigmoid(jnp.einsum('btnd,nde->btne', xblk, wx).reshape(Bsz, T, LRU_WIDTH) + bx)
    log_a = -LRU_C * r * jax.nn.softplus(-lam)
    a = jnp.exp(log_a)
    u = xb * i * jnp.sqrt(-jnp.expm1(2.0 * log_a))

    def combine(left, right):
        a_l, u_l = left
        a_r, u_r = right
        return a_l * a_r, a_r * u_l + u_r

    _, hs = lax.associative_scan(combine, (a, u), axis=1)
    y = hs.astype(h.dtype) * jax.nn.gelu(gate_branch)
    return y @ w_out


def swiglu_ffn(h, w_gate, w_up, w_down):
    return (jax.nn.silu(h @ w_gate) * (h @ w_up)) @ w_down


def setup_inputs(seed: int = 0) -> dict:
    key = jax.random.key(seed)
    keys = iter(jax.random.split(key, 48))
    f32 = jnp.float32

    def normal(shape, scale):
        return scale * jax.random.normal(next(keys), shape, f32)

    def uniform(shape, lo, hi):
        return jax.random.uniform(next(keys), shape, f32, lo, hi)

    D = D_MODEL
    L = DEPTH
    NE = (DEPTH + 1) // 2
    NO = DEPTH // 2
    x = normal((BATCH, SEQ, D), 1.0)
    c = normal((BATCH, D), 1.0)
    norm_pre = 1.0 + normal((L, 2, D), 0.05)
    norm_post = 1.0 + normal((L, 2, D), 0.05)
    ada_w = normal((L, 2, D, 3 * D), D ** -0.5)
    ada_b = normal((L, 2, 3 * D), 0.02)
    ffn_w_gate = normal((L, D, D_FF), D ** -0.5)
    ffn_w_up = normal((L, D, D_FF), D ** -0.5)
    ffn_w_down = normal((L, D_FF, D), D_FF ** -0.5)
    mix_w_in = normal((NE, D, GDN_COLS + RWKV_COLS), D ** -0.5)
    mix_w_out = normal((NE, MIX_WIDTH, D), MIX_WIDTH ** -0.5)
    gdn_conv_w = normal((NE, CONV_WIDTH, 3 * GDN_WIDTH), CONV_WIDTH ** -0.5)
    gdn_a_log = jnp.log(uniform((NE, GDN_HEADS), 1.0, 16.0))
    dt = jnp.exp(uniform((NE, GDN_HEADS), math.log(1e-3), math.log(1e-1)))
    gdn_dt_bias = dt + jnp.log(-jnp.expm1(-dt))
    gdn_norm_w = 1.0 + normal((NE, GDN_HEAD_DIM), 0.05)
    rwkv_mu = uniform((NE, RWKV_COLS), 0.0, 1.0)
    rwkv_w0 = uniform((NE, RWKV_WIDTH), -6.0, -1.0)
    rwkv_w2 = normal((NE, DECAY_LORA, RWKV_WIDTH), 0.3 * DECAY_LORA ** -0.5)
    rwkv_a0 = normal((NE, RWKV_WIDTH), 0.1)
    rwkv_a2 = normal((NE, AAA_LORA, RWKV_WIDTH), AAA_LORA ** -0.5)
    rwkv_g2 = normal((NE, GATE_LORA, RWKV_WIDTH), GATE_LORA ** -0.5)
    rwkv_k_k = 0.85 + normal((NE, RWKV_WIDTH), 0.05)
    rwkv_k_a = 1.0 + normal((NE, RWKV_WIDTH), 0.05)
    rwkv_r_k = normal((NE, RWKV_HEADS, RWKV_HEAD_DIM), 0.1)
    rwkv_ln_w = 1.0 + normal((NE, RWKV_WIDTH), 0.05)
    rwkv_ln_b = normal((NE, RWKV_WIDTH), 0.02)
    lru_w_in = normal((NO, D, 2 * LRU_WIDTH), D ** -0.5)
    lru_conv_w = normal((NO, CONV_WIDTH, LRU_WIDTH), CONV_WIDTH ** -0.5)
    lru_conv_b = normal((NO, LRU_WIDTH), 0.02)
    lru_wa = normal((NO, LRU_BLOCKS, LRU_BLOCK, LRU_BLOCK), LRU_BLOCK ** -0.5)
    lru_ba = normal((NO, LRU_WIDTH), 0.02)
    lru_wx = normal((NO, LRU_BLOCKS, LRU_BLOCK, LRU_BLOCK), LRU_BLOCK ** -0.5)
    lru_bx = normal((NO, LRU_WIDTH), 0.02)
    a_init = uniform((NO, LRU_WIDTH), 0.9, 0.999) ** (1.0 / LRU_C)
    lru_lambda = jnp.log(a_init) - jnp.log1p(-a_init)
    lru_w_out = normal((NO, LRU_WIDTH, D), LRU_WIDTH ** -0.5)
    return {'x': x, 'c': c, 'norm_pre': norm_pre, 'norm_post': norm_post, 'ada_w': ada_w, 'ada_b': ada_b,
            'ffn_w_gate': ffn_w_gate, 'ffn_w_up': ffn_w_up, 'ffn_w_down': ffn_w_down,
            'mix_w_in': mix_w_in, 'mix_w_out': mix_w_out, 'gdn_conv_w': gdn_conv_w, 'gdn_a_log': gdn_a_log,
            'gdn_dt_bias': gdn_dt_bias, 'gdn_norm_w': gdn_norm_w, 'rwkv_mu': rwkv_mu, 'rwkv_w0': rwkv_w0,
            'rwkv_w2': rwkv_w2, 'rwkv_a0': rwkv_a0, 'rwkv_a2': rwkv_a2, 'rwkv_g2': rwkv_g2,
            'rwkv_k_k': rwkv_k_k, 'rwkv_k_a': rwkv_k_a, 'rwkv_r_k': rwkv_r_k, 'rwkv_ln_w': rwkv_ln_w,
            'rwkv_ln_b': rwkv_ln_b, 'lru_w_in': lru_w_in, 'lru_conv_w': lru_conv_w, 'lru_conv_b': lru_conv_b,
            'lru_wa': lru_wa, 'lru_ba': lru_ba, 'lru_wx': lru_wx, 'lru_bx': lru_bx, 'lru_lambda': lru_lambda,
            'lru_w_out': lru_w_out}


def reference(x, c, norm_pre, norm_post, ada_w, ada_b, ffn_w_gate, ffn_w_up, ffn_w_down,
              mix_w_in, mix_w_out, gdn_conv_w, gdn_a_log, gdn_dt_bias, gdn_norm_w,
              rwkv_mu, rwkv_w0, rwkv_w2, rwkv_a0, rwkv_a2, rwkv_g2, rwkv_k_k, rwkv_k_a, rwkv_r_k,
              rwkv_ln_w, rwkv_ln_b, lru_w_in, lru_conv_w, lru_conv_b, lru_wa, lru_ba, lru_wx, lru_bx,
              lru_lambda, lru_w_out):
    for layer in range(DEPTH):
        j = layer // 2
        shift, scale, gate = adaln_params(c, ada_w[layer, 0], ada_b[layer, 0])
        h = rms_norm(x, norm_pre[layer, 0]) * (1.0 + scale) + shift
        if layer % 2 == 0:
            y = delta_rwkv_mixer(h, mix_w_in[j], mix_w_out[j], gdn_conv_w[j], gdn_a_log[j], gdn_dt_bias[j],
                                 gdn_norm_w[j], rwkv_mu[j], rwkv_w0[j], rwkv_w2[j], rwkv_a0[j], rwkv_a2[j],
                                 rwkv_g2[j], rwkv_k_k[j], rwkv_k_a[j], rwkv_r_k[j], rwkv_ln_w[j], rwkv_ln_b[j])
        else:
            y = rglru_mixer(h, lru_w_in[j], lru_conv_w[j], lru_conv_b[j], lru_wa[j], lru_ba[j], lru_wx[j],
                            lru_bx[j], lru_lambda[j], lru_w_out[j])
        x = x + gate * rms_norm(y, norm_post[layer, 0])
        shift, scale, gate = adaln_params(c, ada_w[layer, 1], ada_b[layer, 1])
        h = rms_norm(x, norm_pre[layer, 1]) * (1.0 + scale) + shift
        y = swiglu_ffn(h, ffn_w_gate[layer], ffn_w_up[layer], ffn_w_down[layer])
        x = x + gate * rms_norm(y, norm_post[layer, 1])
    return x
```

```python
import functools

import jax
import jax.numpy as jnp
from jax import lax
from jax.experimental import pallas as pl
from jax.experimental.pallas import tpu as pltpu

F32 = jnp.float32
BF16 = jnp.bfloat16

NORM_EPS = 1e-6
GDN_HEADS = 4
GDN_HEAD_DIM = 128
GDN_WIDTH = GDN_HEADS * GDN_HEAD_DIM
CHUNK = 64
CONV_WIDTH = 4
RWKV_HEADS = 8
RWKV_HEAD_DIM = 64
RWKV_WIDTH = RWKV_HEADS * RWKV_HEAD_DIM
DECAY_LORA = 64
AAA_LORA = 64
GATE_LORA = 128
RWKV_COLS = 3 * RWKV_WIDTH + DECAY_LORA + AAA_LORA + GATE_LORA
RWKV_GN_EPS = 64e-5
LRU_BLOCKS = 4
LRU_C = 8.0
LANES = 128
HALO = 8
VMEM_LIMIT = 56 * 1024 * 1024

NN = (((1,), (0,)), ((), ()))
NT = (((1,), (1,)), ((), ()))
TN = (((0,), (0,)), ((), ()))


def _mm(a, b, dims=NN):
    return lax.dot_general(a, b, dims, preferred_element_type=F32)


def _mm_bf(a, b, dims=NN):
    return _mm(a.astype(BF16), b.astype(BF16), dims)


def _split2(x):
    hi = x.astype(BF16)
    lo = (x - hi.astype(F32)).astype(BF16)
    return hi, lo


def _mm_x3(a, b, dims=NN):
    ah, al = _split2(a)
    bh, bl = _split2(b)
    return _mm(ah, bh, dims) + (_mm(ah, bl, dims) + _mm(al, bh, dims))


def _mm_const_lhs(c_bf, x):
    hi = x.astype(BF16)
    r1 = x - hi.astype(F32)
    mid = r1.astype(BF16)
    lo = (r1 - mid.astype(F32)).astype(BF16)
    return _mm(c_bf, hi) + (_mm(c_bf, mid) + _mm(c_bf, lo))


def _mm_const_rhs(x, c_bf):
    hi = x.astype(BF16)
    r1 = x - hi.astype(F32)
    mid = r1.astype(BF16)
    lo = (r1 - mid.astype(F32)).astype(BF16)
    return _mm(hi, c_bf) + (_mm(mid, c_bf) + _mm(lo, c_bf))


def _sigmoid(x):
    return 1.0 / (1.0 + jnp.exp(-x))


def _silu(x):
    return x * _sigmoid(x)


def _softplus(x):
    return jnp.maximum(x, 0.0) + jnp.log1p(jnp.exp(-jnp.abs(x)))


def _gelu_tanh(x):
    return 0.5 * x * (1.0 + jnp.tanh(0.7978845608028654 * (x + 0.044715 * (x * x * x))))


def _rms(x):
    return x * lax.rsqrt(jnp.mean(x * x, axis=-1, keepdims=True) + NORM_EPS)


def _tri_masks(n):
    row = lax.broadcasted_iota(jnp.int32, (n, n), 0)
    col = lax.broadcasted_iota(jnp.int32, (n, n), 1)
    return row, col


def _unit_lower_inverse(m, row, col):
    n = m.shape[0]
    eye = (row == col).astype(F32)
    blk = 8
    same = (row >> 3) == (col >> 3)
    m8 = jnp.where(same, m, 0.0)
    m2 = _mm_x3(m8, m8)
    m4 = _mm_x3(m2, m2)
    inv = _mm_x3(eye - m8, eye + m2)
    inv = _mm_x3(inv, eye + m4)
    shift = 3
    while blk < n:
        pair = (row >> (shift + 1)) == (col >> (shift + 1))
        off = jnp.where(jnp.logical_and(pair, jnp.logical_not(same)), m, 0.0)
        inv = inv - _mm_x3(_mm_x3(inv, off), inv)
        same = pair
        shift += 1
        blk *= 2
    return inv


def _ada_kernel(c_ref, w_ref, b_ref, o_ref):
    s = _silu(c_ref[...])
    o_ref[0] = _mm_x3(s, w_ref[0]) + b_ref[0]


def _ada_params(c, ada_w, ada_b):
    n_l, n_s, d, d3 = ada_w.shape
    n = n_l * n_s
    bsz = c.shape[0]
    tn = 1024
    return pl.pallas_call(
        _ada_kernel,
        out_shape=jax.ShapeDtypeStruct((n, bsz, d3), F32),
        grid=(n, d3 // tn),
        in_specs=[
            pl.BlockSpec((bsz, d), lambda i, j: (0, 0)),
            pl.BlockSpec((1, d, tn), lambda i, j: (i, 0, j)),
            pl.BlockSpec((1, 1, tn), lambda i, j: (i, 0, j)),
        ],
        out_specs=pl.BlockSpec((1, bsz, tn), lambda i, j: (i, 0, j)),
        compiler_params=pltpu.CompilerParams(vmem_limit_bytes=VMEM_LIMIT),
        name="ada_params",
    )(c, ada_w.reshape(n, d, d3), ada_b.reshape(n, 1, d3))


def _norm_proj_kernel(x_ref, nw_ref, shift_ref, scale_ref, w_ref, *out_refs):
    h = _rms(x_ref[...]) * nw_ref[...]
    h = (h * (1.0 + scale_ref[0]) + shift_ref[0]).astype(BF16)
    off = 0
    for o_ref in out_refs:
        n = o_ref.shape[1]
        o_ref[...] = _mm(h, w_ref[:, off:off + n])
        off += n


def _norm_proj(x2, nw, shift, scale, w_bf, splits, seq, tm=256):
    n_tok, d = x2.shape
    per_b = seq // tm
    return pl.pallas_call(
        _norm_proj_kernel,
        out_shape=[jax.ShapeDtypeStruct((n_tok, n), F32) for n in splits],
        grid=(n_tok // tm,),
        in_specs=[
            pl.BlockSpec((tm, d), lambda i: (i, 0)),
            pl.BlockSpec((1, d), lambda i: (0, 0)),
            pl.BlockSpec((1, 1, d), lambda i: (i // per_b, 0, 0)),
            pl.BlockSpec((1, 1, d), lambda i: (i // per_b, 0, 0)),
            pl.BlockSpec(w_bf.shape, lambda i: (0, 0), pipeline_mode=pl.Buffered(1)),
        ],
        out_specs=[pl.BlockSpec((tm, n), lambda i: (i, 0)) for n in splits],
        compiler_params=pltpu.CompilerParams(vmem_limit_bytes=VMEM_LIMIT),
        name="norm_proj",
    )(x2, nw, shift, scale, w_bf)


def _proj_residual_kernel(*refs, n_in):
    x_ref = refs[0]
    a_refs = refs[1:1 + n_in]
    w_refs = refs[1 + n_in:1 + 2 * n_in]
    gate_ref, nw_ref, o_ref = refs[1 + 2 * n_in:]
    y = _mm(a_refs[0][...].astype(BF16), w_refs[0][...])
    for a_ref, w_ref in zip(a_refs[1:], w_refs[1:]):
        y = y + _mm(a_ref[...].astype(BF16), w_ref[...])
    o_ref[...] = x_ref[...] + gate_ref[0] * (_rms(y) * nw_ref[...])


def _proj_residual(x2, acts, ws_bf, gate, nw, seq, tm=512):
    n_tok, d = x2.shape
    per_b = seq // tm
    n_in = len(acts)
    in_specs = [pl.BlockSpec((tm, d), lambda i: (i, 0))]
    in_specs += [pl.BlockSpec((tm, a.shape[1]), lambda i: (i, 0)) for a in acts]
    in_specs += [pl.BlockSpec(w.shape, lambda i: (0, 0), pipeline_mode=pl.Buffered(1)) for w in ws_bf]
    in_specs += [pl.BlockSpec((1, 1, d), lambda i: (i // per_b, 0, 0)), pl.BlockSpec((1, d), lambda i: (0, 0))]
    return pl.pallas_call(
        functools.partial(_proj_residual_kernel, n_in=n_in),
        out_shape=jax.ShapeDtypeStruct((n_tok, d), F32),
        grid=(n_tok // tm,),
        in_specs=in_specs,
        out_specs=pl.BlockSpec((tm, d), lambda i: (i, 0)),
        compiler_params=pltpu.CompilerParams(vmem_limit_bytes=VMEM_LIMIT),
        name="proj_residual",
    )(x2, *acts, *ws_bf, gate, nw)


def _ffn_kernel(x_ref, nw_pre_ref, shift_ref, scale_ref, wg_ref, wu_ref, wd_ref, gate_ref, nw_post_ref, o_ref):
    x = x_ref[...]
    h = _rms(x) * nw_pre_ref[...]
    h = (h * (1.0 + scale_ref[0]) + shift_ref[0]).astype(BF16)
    g = _mm(h, wg_ref[...])
    u = _mm(h, wu_ref[...])
    act = (_silu(g) * u).astype(BF16)
    y = _mm(act, wd_ref[...])
    o_ref[...] = x + gate_ref[0] * (_rms(y) * nw_post_ref[...])


def _ffn(x2, nw_pre, shift, scale, wg_bf, wu_bf, wd_bf, gate, nw_post, seq, tm=256):
    n_tok, d = x2.shape
    per_b = seq // tm
    row = lambda i: (i, 0)
    const = lambda i: (0, 0)
    per_batch = lambda i: (i // per_b, 0, 0)
    return pl.pallas_call(
        _ffn_kernel,
        out_shape=jax.ShapeDtypeStruct((n_tok, d), F32),
        grid=(n_tok // tm,),
        in_specs=[
            pl.BlockSpec((tm, d), row),
            pl.BlockSpec((1, d), const),
            pl.BlockSpec((1, 1, d), per_batch),
            pl.BlockSpec((1, 1, d), per_batch),
            pl.BlockSpec(wg_bf.shape, const, pipeline_mode=pl.Buffered(1)),
            pl.BlockSpec(wu_bf.shape, const, pipeline_mode=pl.Buffered(1)),
            pl.BlockSpec(wd_bf.shape, const, pipeline_mode=pl.Buffered(1)),
            pl.BlockSpec((1, 1, d), per_batch),
            pl.BlockSpec((1, d), const),
        ],
        out_specs=pl.BlockSpec((tm, d), row),
        compiler_params=pltpu.CompilerParams(vmem_limit_bytes=VMEM_LIMIT),
        name="ffn",
    )(x2, nw_pre, shift, scale, wg_bf, wu_bf, wd_bf, gate, nw_post)


def _gdn_kernel(qkvz_ref, ab_ref, convw_ref, alog_ref, dtb_ref, nw_ref, o_ref,
                halo_ref, qkv_s, g_s, beta_s, state_ref, *, tt):
    W = GDN_WIDTH
    D = GDN_HEAD_DIM
    C = CHUNK

    @pl.when(pl.program_id(1) == 0)
    def _():
        halo_ref[...] = jnp.zeros_like(halo_ref)
        state_ref[...] = jnp.zeros_like(state_ref)

    x = qkvz_ref[:, 0:3 * W]
    xp = jnp.concatenate([halo_ref[...], x], axis=0)
    cw = convw_ref[...]
    y = x * cw[CONV_WIDTH - 1:CONV_WIDTH]
    for j in range(CONV_WIDTH - 1):
        o = HALO - (CONV_WIDTH - 1) + j
        y = y + xp[o:o + tt] * cw[j:j + 1]
    halo_ref[...] = x[tt - HALO:tt]
    y = _silu(y)
    for h in range(GDN_HEADS):
        q = y[:, h * D:(h + 1) * D]
        k = y[:, W + h * D:W + (h + 1) * D]
        q = q * lax.rsqrt(jnp.sum(q * q, axis=-1, keepdims=True) + NORM_EPS) * (D ** -0.5)
        k = k * lax.rsqrt(jnp.sum(k * k, axis=-1, keepdims=True) + NORM_EPS)
        qkv_s[:, h * D:(h + 1) * D] = q
        qkv_s[:, W + h * D:W + (h + 1) * D] = k
    qkv_s[:, 2 * W:3 * W] = y[:, 2 * W:3 * W]

    ab = ab_ref[...]
    g_s[...] = -jnp.exp(alog_ref[...]) * _softplus(ab + dtb_ref[...])
    beta_s[...] = _sigmoid(ab)

    row, col = _tri_masks(C)
    causal = row >= col
    strict = row > col
    tril_bf = causal.astype(BF16)
    nw = nw_ref[...]

    def chunk_body(c, carry):
        r0 = pl.multiple_of(c * C, C)
        rows = pl.ds(r0, C)
        gc = _mm_const_lhs(tril_bf, g_s[rows, :])
        gc_t = gc.T
        beta_c = beta_s[rows, :]
        for h in range(GDN_HEADS):
            q = qkv_s[rows, h * D:(h + 1) * D]
            k = qkv_s[rows, W + h * D:W + (h + 1) * D]
            v = qkv_s[rows, 2 * W + h * D:2 * W + (h + 1) * D]
            g_col = gc[:, h:h + 1]
            g_row = gc_t[h:h + 1, :]
            g_last = gc[C - 1:C, h:h + 1]
            decay = jnp.where(causal, jnp.exp(jnp.minimum(g_col - g_row, 0.0)), 0.0)
            beta = beta_c[:, GDN_HEADS + h:GDN_HEADS + h + 1]
            k_beta = k * beta
            m = jnp.where(strict, _mm_bf(k_beta, k, NT) * decay, 0.0)
            t_inv = _unit_lower_inverse(m, row, col)
            e_g = jnp.exp(g_col)
            u = _mm_x3(t_inv, v * beta)
            w = _mm_x3(t_inv, k_beta * e_g)
            attn = _mm_bf(q, k, NT) * decay
            q_dec = q * e_g
            k_dec = k * jnp.exp(g_last - g_col)
            s = state_ref[h]
            v_new = u - _mm_bf(w, s)
            o = _mm_bf(q_dec, s) + _mm_bf(attn, v_new)
            state_ref[h] = s * jnp.exp(g_last) + _mm_bf(k_dec, v_new, TN)
            z = qkvz_ref[rows, 3 * W + h * D:3 * W + (h + 1) * D]
            o_ref[rows, h * D:(h + 1) * D] = _rms(o) * nw * _silu(z)
        return carry

    lax.fori_loop(0, tt // C, chunk_body, 0)


def _gdn(qkvz, ab, conv_w, a_log_row, dt_row, norm_w, bsz, seq, tt=256):
    n_tok = qkvz.shape[0]
    W = GDN_WIDTH
    per_b = seq // tt
    row = lambda b, j: (b * per_b + j, 0)
    const = lambda b, j: (0, 0)
    return pl.pallas_call(
        functools.partial(_gdn_kernel, tt=tt),
        out_shape=jax.ShapeDtypeStruct((n_tok, W), F32),
        grid=(bsz, per_b),
        in_specs=[
            pl.BlockSpec((tt, 4 * W), row),
            pl.BlockSpec((tt, LANES), row),
            pl.BlockSpec(conv_w.shape, const),
            pl.BlockSpec((1, LANES), const),
            pl.BlockSpec((1, LANES), const),
            pl.BlockSpec((1, GDN_HEAD_DIM), const),
        ],
        out_specs=pl.BlockSpec((tt, W), row),
        scratch_shapes=[
            pltpu.VMEM((HALO, 3 * W), F32),
            pltpu.VMEM((tt, 3 * W), F32),
            pltpu.VMEM((tt, LANES), F32),
            pltpu.VMEM((tt, LANES), F32),
            pltpu.VMEM((GDN_HEADS, GDN_HEAD_DIM, GDN_HEAD_DIM), F32),
        ],
        compiler_params=pltpu.CompilerParams(
            dimension_semantics=("arbitrary", "arbitrary"), vmem_limit_bytes=VMEM_LIMIT),
        name="gdn",
    )(qkvz, ab, conv_w, a_log_row, dt_row, norm_w)


def _rwkv_kernel(rw_ref, mu_ref, w0_ref, w2_ref, a0_ref, a2_ref, g2_ref, kk_ref, ka_ref, rk_ref,
                 lnw_ref, lnb_ref, o_ref,
                 carry_ref, r_s, k_s, v_s, kk_s, b_s, lw_s, y_s, bonus_s, g_s, state_ref, *, tt):
    W = RWKV_WIDTH
    C = CHUNK
    P2 = 2 * RWKV_HEAD_DIM

    @pl.when(pl.program_id(1) == 0)
    def _():
        carry_ref[...] = jnp.zeros_like(carry_ref)
        state_ref[...] = jnp.zeros_like(state_ref)

    cf = rw_ref[...]
    trow = lax.broadcasted_iota(jnp.int32, (tt, 1), 0)
    prev = jnp.where(trow == 0, carry_ref[HALO - 1:HALO, :], pltpu.roll(cf, 1, axis=0))
    carry_ref[...] = cf[tt - HALO:tt]
    cf = cf + mu_ref[...] * (prev - cf)

    r = cf[:, 0:W]
    k = cf[:, W:2 * W]
    v = cf[:, 2 * W:3 * W]
    wd_ad = cf[:, 3 * W:3 * W + LANES]
    gd = cf[:, 3 * W + LANES:3 * W + 2 * LANES]

    hrow, hcol = _tri_masks(W)
    head_ones = ((hrow >> 6) == (hcol >> 6)).astype(BF16)

    w_log = -_softplus(-(w0_ref[...] + _mm_bf(jnp.tanh(wd_ad), w2_ref[...]))) - 0.5
    a = _sigmoid(a0_ref[...] + _mm_bf(wd_ad, a2_ref[...]))
    g_s[...] = _mm_bf(_sigmoid(gd), g2_ref[...])
    kkv = k * kk_ref[...]
    kk = kkv * lax.rsqrt(_mm_const_rhs(kkv * kkv, head_ones) + NORM_EPS)
    kmod = k * (1.0 + (a - 1.0) * ka_ref[...])
    bonus_s[...] = _mm_const_rhs(r * kmod * rk_ref[...], head_ones) * v
    r_s[...] = r
    k_s[...] = kmod
    v_s[...] = v
    kk_s[...] = kk
    b_s[...] = kk * a
    lw_s[...] = -jnp.exp(w_log)

    row, col = _tri_masks(C)
    causal = row >= col
    strict = row > col
    tril_bf = causal.astype(BF16)
    lane = lax.broadcasted_iota(jnp.int32, (1, P2), 1)
    first = lane < RWKV_HEAD_DIM
    brow, bcol = _tri_masks(P2)
    blockdiag = (brow >> 6) == (bcol >> 6)

    def chunk_body(c, carry):
        r0 = pl.multiple_of(c * C, C)
        rows = pl.ds(r0, C)
        lw = lw_s[rows, :]
        lc = _mm_const_lhs(tril_bf, lw)
        lc_t = lc.T
        e_inc = jnp.exp(lc)
        e_inv = jnp.exp(-lc)
        e_exc = jnp.exp(lc - lw)
        g_end = jnp.exp(lc[C - 1:C, :])
        r_t = r_s[rows, :] * e_inc
        k_t = k_s[rows, :] * e_inv
        b_t = b_s[rows, :] * e_inv
        kk_t = kk_s[rows, :] * e_exc
        k_end = k_t * g_end
        b_end = b_t * g_end
        vv = v_s[rows, :]
        for p in range(RWKV_HEADS // 2):
            ln = slice(p * P2, (p + 1) * P2)
            rp, kp, bp, kkp, vp = r_t[:, ln], k_t[:, ln], b_t[:, ln], kk_t[:, ln], vv[:, ln]
            s = state_ref[p]
            x = _mm_bf(kkp, s)
            y = _mm_bf(rp, s)
            halves = []
            for sel in (first, jnp.logical_not(first)):
                kk_h = jnp.where(sel, kkp, 0.0)
                r_h = jnp.where(sel, rp, 0.0)
                a_kb = jnp.where(strict, _mm_bf(kk_h, bp, NT), 0.0)
                a_kk = jnp.where(strict, _mm_bf(kk_h, kp, NT), 0.0)
                a_rb = jnp.where(causal, _mm_bf(r_h, bp, NT), 0.0)
                a_rk = jnp.where(causal, _mm_bf(r_h, kp, NT), 0.0)
                t_inv = _unit_lower_inverse(a_kb, row, col)
                p_h = -_mm_x3(t_inv, x + _mm_bf(a_kk, vp))
                y_h = _mm_bf(a_rb, p_h) + _mm_bf(a_rk, vp)
                halves.append((p_h, y_h))
            pp = jnp.where(first, halves[0][0], halves[1][0])
            y = y + jnp.where(first, halves[0][1], halves[1][1])
            g_col = jnp.exp(lc_t[p * P2:(p + 1) * P2, C - 1:C])
            upd = _mm_bf(b_end[:, ln], pp, TN) + _mm_bf(k_end[:, ln], vp, TN)
            state_ref[p] = s * g_col + jnp.where(blockdiag, upd, 0.0)
            y_s[rows, ln] = y
        return carry

    lax.fori_loop(0, tt // C, chunk_body, 0)

    y = y_s[...]
    inv_n = 1.0 / RWKV_HEAD_DIM
    mean = _mm_const_rhs(y, head_ones) * inv_n
    d = y - mean
    var = _mm_const_rhs(d * d, head_ones) * inv_n
    yn = d * lax.rsqrt(var + RWKV_GN_EPS) * lnw_ref[...] + lnb_ref[...]
    o_ref[...] = (yn + bonus_s[...]) * g_s[...]


def _rwkv(rw, mu, w0, w2p, a0, a2p, g2, k_k, k_a, r_k, ln_w, ln_b, bsz, seq, tt=256):
    n_tok, cols = rw.shape
    W = RWKV_WIDTH
    per_b = seq // tt
    row = lambda b, j: (b * per_b + j, 0)
    const = lambda b, j: (0, 0)
    vec = pl.BlockSpec((1, W), const)
    return pl.pallas_call(
        functools.partial(_rwkv_kernel, tt=tt),
        out_shape=jax.ShapeDtypeStruct((n_tok, W), F32),
        grid=(bsz, per_b),
        in_specs=[
            pl.BlockSpec((tt, cols), row),
            pl.BlockSpec((1, cols), const),
            vec,
            pl.BlockSpec(w2p.shape, const),
            vec,
            pl.BlockSpec(a2p.shape, const),
            pl.BlockSpec(g2.shape, const),
            vec, vec, vec, vec, vec,
        ],
        out_specs=pl.BlockSpec((tt, W), row),
        scratch_shapes=[pltpu.VMEM((HALO, cols), F32)]
        + [pltpu.VMEM((tt, W), F32) for _ in range(9)]
        + [pltpu.VMEM((RWKV_HEADS // 2, 2 * RWKV_HEAD_DIM, 2 * RWKV_HEAD_DIM), F32)],
        compiler_params=pltpu.CompilerParams(
            dimension_semantics=("arbitrary", "arbitrary"), vmem_limit_bytes=VMEM_LIMIT),
        name="rwkv7",
    )(rw, mu, w0, w2p, a0, a2p, g2, k_k, k_a, r_k, ln_w, ln_b)


def _rglru_kernel(gb_ref, xb_ref, convw_ref, convb_ref, wa_ref, ba_ref, wx_ref, bx_ref, lam_ref, o_ref,
                  halo_ref, h_ref, *, tt):
    @pl.when(pl.program_id(1) == 0)
    def _():
        halo_ref[...] = jnp.zeros_like(halo_ref)
        h_ref[...] = jnp.zeros_like(h_ref)

    x = xb_ref[...]
    xp = jnp.concatenate([halo_ref[...], x], axis=0)
    cw = convw_ref[...]
    xc = x * cw[CONV_WIDTH - 1:CONV_WIDTH] + convb_ref[...]
    for j in range(CONV_WIDTH - 1):
        o = HALO - (CONV_WIDTH - 1) + j
        xc = xc + xp[o:o + tt] * cw[j:j + 1]
    halo_ref[...] = x[tt - HALO:tt]

    blk = x.shape[1] // LRU_BLOCKS
    xc_bf = xc.astype(BF16)
    ra = jnp.concatenate(
        [_mm(xc_bf[:, n * blk:(n + 1) * blk], wa_ref[n]) for n in range(LRU_BLOCKS)], axis=1)
    ix = jnp.concatenate(
        [_mm(xc_bf[:, n * blk:(n + 1) * blk], wx_ref[n]) for n in range(LRU_BLOCKS)], axis=1)
    r = _sigmoid(ra + ba_ref[...])
    i = _sigmoid(ix + bx_ref[...])
    log_a = -LRU_C * r * _softplus(-lam_ref[...])
    a = jnp.exp(log_a)
    u = xc * i * jnp.sqrt(1.0 - jnp.exp(2.0 * log_a))

    trow = lax.broadcasted_iota(jnp.int32, (tt, 1), 0)
    d = 1
    while d < tt:
        keep = trow >= d
        a_sh = jnp.where(keep, pltpu.roll(a, d, axis=0), 1.0)
        u_sh = jnp.where(keep, pltpu.roll(u, d, axis=0), 0.0)
        u = a * u_sh + u
        a = a * a_sh
        d *= 2
    h = u + a * h_ref[HALO - 1:HALO, :]
    h_ref[...] = h[tt - HALO:tt]
    o_ref[...] = h * _gelu_tanh(gb_ref[...])


def _rglru(gb, xb, conv_w, conv_b, wa_bf, ba, wx_bf, bx, lam, bsz, seq, tt=256):
    n_tok, width = xb.shape
    per_b = seq // tt
    row = lambda b, j: (b * per_b + j, 0)
    const = lambda b, j: (0, 0)
    const3 = lambda b, j: (0, 0, 0)
    vec = pl.BlockSpec((1, width), const)
    return pl.pallas_call(
        functools.partial(_rglru_kernel, tt=tt),
        out_shape=jax.ShapeDtypeStruct((n_tok, width), F32),
        grid=(bsz, per_b),
        in_specs=[
            pl.BlockSpec((tt, width), row),
            pl.BlockSpec((tt, width), row),
            pl.BlockSpec(conv_w.shape, const),
            vec,
            pl.BlockSpec(wa_bf.shape, const3),
            vec,
            pl.BlockSpec(wx_bf.shape, const3),
            vec,
            vec,
        ],
        out_specs=pl.BlockSpec((tt, width), row),
        scratch_shapes=[pltpu.VMEM((HALO, width), F32), pltpu.VMEM((HALO, width), F32)],
        compiler_params=pltpu.CompilerParams(
            dimension_semantics=("arbitrary", "arbitrary"), vmem_limit_bytes=VMEM_LIMIT),
        name="rglru",
    )(gb, xb, conv_w, conv_b, wa_bf, ba, wx_bf, bx, lam)


def _pad_lanes(v):
    return jnp.pad(v, (0, LANES - v.shape[0])).reshape(1, LANES)


def kernel(x, c, norm_pre, norm_post, ada_w, ada_b, ffn_w_gate, ffn_w_up, ffn_w_down, mix_w_in, mix_w_out, gdn_conv_w, gdn_a_log, gdn_dt_bias, gdn_norm_w, rwkv_mu, rwkv_w0, rwkv_w2, rwkv_a0, rwkv_a2, rwkv_g2, rwkv_k_k, rwkv_k_a, rwkv_r_k, rwkv_ln_w, rwkv_ln_b, lru_w_in, lru_conv_w, lru_conv_b, lru_wa, lru_ba, lru_wx, lru_bx, lru_lambda, lru_w_out):
    bsz, seq, d = x.shape
    depth = norm_pre.shape[0]
    x2 = x.reshape(bsz * seq, d)
    mods = _ada_params(c, ada_w, ada_b)

    def mod(layer, sub):
        m = mods[layer * 2 + sub]
        return (m[:, None, 0:d], m[:, None, d:2 * d], m[:, None, 2 * d:3 * d])

    GW = GDN_WIDTH
    for layer in range(depth):
        j = layer // 2
        shift, scale, gate = mod(layer, 0)
        nw_pre = norm_pre[layer, 0].reshape(1, d)
        nw_post = norm_post[layer, 0].reshape(1, d)
        if layer % 2 == 0:
            w_in = mix_w_in[j]
            n_gdn = 4 * GW + 2 * GDN_HEADS
            w_cat = jnp.concatenate(
                [w_in[:, 0:4 * GW], w_in[:, n_gdn:],
                 jnp.pad(w_in[:, 4 * GW:n_gdn], ((0, 0), (0, LANES - 2 * GDN_HEADS)))], axis=1).astype(BF16)
            qkvz, rw, ab = _norm_proj(x2, nw_pre, shift, scale, w_cat, (4 * GW, RWKV_COLS, LANES), seq)
            out_a = _gdn(qkvz, ab, gdn_conv_w[j], _pad_lanes(gdn_a_log[j]), _pad_lanes(gdn_dt_bias[j]),
                         gdn_norm_w[j].reshape(1, GDN_HEAD_DIM), bsz, seq)
            w2p = jnp.pad(rwkv_w2[j], ((0, LANES - DECAY_LORA), (0, 0))).astype(BF16)
            a2p = jnp.pad(rwkv_a2[j], ((DECAY_LORA, LANES - DECAY_LORA - AAA_LORA), (0, 0))).astype(BF16)
            vec = lambda t: t.reshape(1, RWKV_WIDTH)
            out_b = _rwkv(rw, rwkv_mu[j].reshape(1, RWKV_COLS), vec(rwkv_w0[j]), w2p, vec(rwkv_a0[j]), a2p,
                          rwkv_g2[j].astype(BF16), vec(rwkv_k_k[j]), vec(rwkv_k_a[j]), vec(rwkv_r_k[j]),
                          vec(rwkv_ln_w[j]), vec(rwkv_ln_b[j]), bsz, seq)
            w_out = mix_w_out[j].astype(BF16)
            x2 = _proj_residual(x2, [out_a, out_b], [w_out[0:GW], w_out[GW:]], gate, nw_post, seq)
        else:
            width = lru_w_in.shape[2] // 2
            gb, xb = _norm_proj(x2, nw_pre, shift, scale, lru_w_in[j].astype(BF16), (width, width), seq)
            vec = lambda t: t.reshape(1, width)
            y = _rglru(gb, xb, lru_conv_w[j], vec(lru_conv_b[j]), lru_wa[j].astype(BF16), vec(lru_ba[j]),
                       lru_wx[j].astype(BF16), vec(lru_bx[j]), vec(lru_lambda[j]), bsz, seq)
            x2 = _proj_residual(x2, [y], [lru_w_out[j].astype(BF16)], gate, nw_post, seq)
        shift, scale, gate = mod(layer, 1)
        x2 = _ffn(x2, norm_pre[layer, 1].reshape(1, d), shift, scale, ffn_w_gate[layer].astype(BF16),
                  ffn_w_up[layer].astype(BF16), ffn_w_down[layer].astype(BF16), gate,
                  norm_post[layer, 1].reshape(1, d), seq)
    return x2.reshape(bsz, seq, d)
```

```python
import functools

import jax
import jax.numpy as jnp
from jax import lax
from jax.experimental import pallas as pl
from jax.experimental.pallas import tpu as pltpu

F32 = jnp.float32
BF16 = jnp.bfloat16

NORM_EPS = 1e-6
GDN_HEADS = 4
GDN_HEAD_DIM = 128
GDN_WIDTH = GDN_HEADS * GDN_HEAD_DIM
CHUNK = 64
CONV_WIDTH = 4
RWKV_HEADS = 8
RWKV_HEAD_DIM = 64
RWKV_WIDTH = RWKV_HEADS * RWKV_HEAD_DIM
DECAY_LORA = 64
AAA_LORA = 64
GATE_LORA = 128
RWKV_COLS = 3 * RWKV_WIDTH + DECAY_LORA + AAA_LORA + GATE_LORA
RWKV_GN_EPS = 64e-5
LRU_BLOCKS = 4
LRU_C = 8.0
LANES = 128
HALO = 8
VMEM_LIMIT = 56 * 1024 * 1024

NN = (((1,), (0,)), ((), ()))
NT = (((1,), (1,)), ((), ()))
TN = (((0,), (0,)), ((), ()))


def _mm(a, b, dims=NN):
    return lax.dot_general(a, b, dims, preferred_element_type=F32)


def _mm_bf(a, b, dims=NN):
    return _mm(a.astype(BF16), b.astype(BF16), dims)


def _split2(x):
    hi = x.astype(BF16)
    lo = (x - hi.astype(F32)).astype(BF16)
    return hi, lo


def _mm_x3(a, b, dims=NN):
    ah, al = _split2(a)
    bh, bl = _split2(b)
    return _mm(ah, bh, dims) + (_mm(ah, bl, dims) + _mm(al, bh, dims))


def _mm_const_lhs(c_bf, x):
    hi = x.astype(BF16)
    r1 = x - hi.astype(F32)
    mid = r1.astype(BF16)
    lo = (r1 - mid.astype(F32)).astype(BF16)
    return _mm(c_bf, hi) + (_mm(c_bf, mid) + _mm(c_bf, lo))


def _mm_const_rhs(x, c_bf):
    hi = x.astype(BF16)
    r1 = x - hi.astype(F32)
    mid = r1.astype(BF16)
    lo = (r1 - mid.astype(F32)).astype(BF16)
    return _mm(hi, c_bf) + (_mm(mid, c_bf) + _mm(lo, c_bf))


def _sigmoid(x):
    return 1.0 / (1.0 + jnp.exp(-x))


def _silu(x):
    return x * _sigmoid(x)


def _softplus(x):
    return jnp.maximum(x, 0.0) + jnp.log1p(jnp.exp(-jnp.abs(x)))


def _gelu_tanh(x):
    return 0.5 * x * (1.0 + jnp.tanh(0.7978845608028654 * (x + 0.044715 * (x * x * x))))


def _rms(x):
    return x * lax.rsqrt(jnp.mean(x * x, axis=-1, keepdims=True) + NORM_EPS)


def _tri_masks(n):
    row = lax.broadcasted_iota(jnp.int32, (n, n), 0)
    col = lax.broadcasted_iota(jnp.int32, (n, n), 1)
    return row, col


def _unit_lower_inverses(ms, row, col):
    eye = (row == col).astype(F32)
    same = (row >> 3) == (col >> 3)
    m8 = [jnp.where(same, m, 0.0) for m in ms]
    m2 = [_mm_x3(a, a) for a in m8]
    m4 = [_mm_x3(a, a) for a in m2]
    inv = [_mm_x3(eye - a, eye + b) for a, b in zip(m8, m2)]
    inv = [_mm_x3(a, eye + b) for a, b in zip(inv, m4)]
    shift = 3
    while (1 << shift) < CHUNK:
        pair = (row >> (shift + 1)) == (col >> (shift + 1))
        off_mask = jnp.logical_and(pair, jnp.logical_not(same))
        tmp = [_mm_x3(a, jnp.where(off_mask, m, 0.0)) for a, m in zip(inv, ms)]
        inv = [a - _mm_x3(t, a) for a, t in zip(inv, tmp)]
        same = pair
        shift += 1
    return inv


def _ada_kernel(c_ref, w_ref, b_ref, o_ref):
    s = _silu(c_ref[...])
    o_ref[0] = _mm_x3(s, w_ref[0]) + b_ref[0]


def _ada_params(c, ada_w, ada_b):
    n_l, n_s, d, d3 = ada_w.shape
    n = n_l * n_s
    bsz = c.shape[0]
    tn = 1024
    return pl.pallas_call(
        _ada_kernel,
        out_shape=jax.ShapeDtypeStruct((n, bsz, d3), F32),
        grid=(n, d3 // tn),
        in_specs=[
            pl.BlockSpec((bsz, d), lambda i, j: (0, 0)),
            pl.BlockSpec((1, d, tn), lambda i, j: (i, 0, j)),
            pl.BlockSpec((1, 1, tn), lambda i, j: (i, 0, j)),
        ],
        out_specs=pl.BlockSpec((1, bsz, tn), lambda i, j: (i, 0, j)),
        compiler_params=pltpu.CompilerParams(vmem_limit_bytes=VMEM_LIMIT),
        name="ada_params",
    )(c, ada_w.reshape(n, d, d3), ada_b.reshape(n, 1, d3))


def _norm_proj_kernel(x_ref, nw_ref, shift_ref, scale_ref, w_ref, *out_refs):
    h = _rms(x_ref[...]) * nw_ref[...]
    h = (h * (1.0 + scale_ref[0]) + shift_ref[0]).astype(BF16)
    off = 0
    for o_ref in out_refs:
        n = o_ref.shape[1]
        o_ref[...] = _mm(h, w_ref[:, off:off + n])
        off += n


def _norm_proj(x2, nw, shift, scale, w_bf, splits, seq, tm=256):
    n_tok, d = x2.shape
    per_b = seq // tm
    return pl.pallas_call(
        _norm_proj_kernel,
        out_shape=[jax.ShapeDtypeStruct((n_tok, n), F32) for n in splits],
        grid=(n_tok // tm,),
        in_specs=[
            pl.BlockSpec((tm, d), lambda i: (i, 0)),
            pl.BlockSpec((1, d), lambda i: (0, 0)),
            pl.BlockSpec((1, 1, d), lambda i: (i // per_b, 0, 0)),
            pl.BlockSpec((1, 1, d), lambda i: (i // per_b, 0, 0)),
            pl.BlockSpec(w_bf.shape, lambda i: (0, 0), pipeline_mode=pl.Buffered(1)),
        ],
        out_specs=[pl.BlockSpec((tm, n), lambda i: (i, 0)) for n in splits],
        compiler_params=pltpu.CompilerParams(vmem_limit_bytes=VMEM_LIMIT),
        name="norm_proj",
    )(x2, nw, shift, scale, w_bf)


def _proj_residual_kernel(*refs, n_in):
    x_ref = refs[0]
    a_refs = refs[1:1 + n_in]
    w_refs = refs[1 + n_in:1 + 2 * n_in]
    gate_ref, nw_ref, o_ref = refs[1 + 2 * n_in:]
    y = _mm(a_refs[0][...].astype(BF16), w_refs[0][...])
    for a_ref, w_ref in zip(a_refs[1:], w_refs[1:]):
        y = y + _mm(a_ref[...].astype(BF16), w_ref[...])
    o_ref[...] = x_ref[...] + gate_ref[0] * (_rms(y) * nw_ref[...])


def _proj_residual(x2, acts, ws_bf, gate, nw, seq, tm=512):
    n_tok, d = x2.shape
    per_b = seq // tm
    n_in = len(acts)
    in_specs = [pl.BlockSpec((tm, d), lambda i: (i, 0))]
    in_specs += [pl.BlockSpec((tm, a.shape[1]), lambda i: (i, 0)) for a in acts]
    in_specs += [pl.BlockSpec(w.shape, lambda i: (0, 0), pipeline_mode=pl.Buffered(1)) for w in ws_bf]
    in_specs += [pl.BlockSpec((1, 1, d), lambda i: (i // per_b, 0, 0)), pl.BlockSpec((1, d), lambda i: (0, 0))]
    return pl.pallas_call(
        functools.partial(_proj_residual_kernel, n_in=n_in),
        out_shape=jax.ShapeDtypeStruct((n_tok, d), F32),
        grid=(n_tok // tm,),
        in_specs=in_specs,
        out_specs=pl.BlockSpec((tm, d), lambda i: (i, 0)),
        compiler_params=pltpu.CompilerParams(vmem_limit_bytes=VMEM_LIMIT),
        name="proj_residual",
    )(x2, *acts, *ws_bf, gate, nw)


def _ffn_kernel(x_ref, nw_pre_ref, shift_ref, scale_ref, wg_ref, wu_ref, wd_ref, gate_ref, nw_post_ref, o_ref):
    x = x_ref[...]
    h = _rms(x) * nw_pre_ref[...]
    h = (h * (1.0 + scale_ref[0]) + shift_ref[0]).astype(BF16)
    g = _mm(h, wg_ref[...])
    u = _mm(h, wu_ref[...])
    act = (_silu(g) * u).astype(BF16)
    y = _mm(act, wd_ref[...])
    o_ref[...] = x + gate_ref[0] * (_rms(y) * nw_post_ref[...])


def _ffn(x2, nw_pre, shift, scale, wg_bf, wu_bf, wd_bf, gate, nw_post, seq, tm=256):
    n_tok, d = x2.shape
    per_b = seq // tm
    row = lambda i: (i, 0)
    const = lambda i: (0, 0)
    per_batch = lambda i: (i // per_b, 0, 0)
    return pl.pallas_call(
        _ffn_kernel,
        out_shape=jax.ShapeDtypeStruct((n_tok, d), F32),
        grid=(n_tok // tm,),
        in_specs=[
            pl.BlockSpec((tm, d), row),
            pl.BlockSpec((1, d), const),
            pl.BlockSpec((1, 1, d), per_batch),
            pl.BlockSpec((1, 1, d), per_batch),
            pl.BlockSpec(wg_bf.shape, const, pipeline_mode=pl.Buffered(1)),
            pl.BlockSpec(wu_bf.shape, const, pipeline_mode=pl.Buffered(1)),
            pl.BlockSpec(wd_bf.shape, const, pipeline_mode=pl.Buffered(1)),
            pl.BlockSpec((1, 1, d), per_batch),
            pl.BlockSpec((1, d), const),
        ],
        out_specs=pl.BlockSpec((tm, d), row),
        compiler_params=pltpu.CompilerParams(vmem_limit_bytes=VMEM_LIMIT),
        name="ffn",
    )(x2, nw_pre, shift, scale, wg_bf, wu_bf, wd_bf, gate, nw_post)


def _gdn_kernel(qkvz_ref, ab_ref, convw_ref, alog_ref, dtb_ref, nw_ref, o_ref,
                halo_ref, qkv_s, u_s, w_s, qd_s, kd_s, attn_s, o_s, state_ref, *, tt):
    W = GDN_WIDTH
    D = GDN_HEAD_DIM
    C = CHUNK
    H = GDN_HEADS
    n_c = tt // C

    @pl.when(pl.program_id(1) == 0)
    def _():
        halo_ref[...] = jnp.zeros_like(halo_ref)
        state_ref[...] = jnp.zeros_like(state_ref)

    x = qkvz_ref[:, 0:3 * W]
    xp = jnp.concatenate([halo_ref[...], x], axis=0)
    cw = convw_ref[...]
    y = x * cw[CONV_WIDTH - 1:CONV_WIDTH]
    for j in range(CONV_WIDTH - 1):
        o = HALO - (CONV_WIDTH - 1) + j
        y = y + xp[o:o + tt] * cw[j:j + 1]
    halo_ref[...] = x[tt - HALO:tt]
    y = _silu(y)
    for h in range(H):
        q = y[:, h * D:(h + 1) * D]
        k = y[:, W + h * D:W + (h + 1) * D]
        q = q * lax.rsqrt(jnp.sum(q * q, axis=-1, keepdims=True) + NORM_EPS) * (D ** -0.5)
        k = k * lax.rsqrt(jnp.sum(k * k, axis=-1, keepdims=True) + NORM_EPS)
        qkv_s[:, h * D:(h + 1) * D] = q
        qkv_s[:, W + h * D:W + (h + 1) * D] = k
    qkv_s[:, 2 * W:3 * W] = y[:, 2 * W:3 * W]

    ab = ab_ref[...]
    g = -jnp.exp(alog_ref[...]) * _softplus(ab + dtb_ref[...])
    beta_all = _sigmoid(ab)

    row, col = _tri_masks(tt)
    same_chunk = (row >> 6) == (col >> 6)
    causal = jnp.logical_and(same_chunk, row >= col)
    strict = jnp.logical_and(same_chunk, row > col)
    gc = _mm_const_lhs(causal.astype(BF16), g)
    gc_t = gc.T
    g_last = jnp.concatenate(
        [jnp.broadcast_to(gc[c * C + C - 1:c * C + C, :], (C, LANES)) for c in range(n_c)], axis=0)
    e_g = jnp.exp(gc)
    e_rem = jnp.exp(g_last - gc)
    qs = [qkv_s[:, h * D:(h + 1) * D] for h in range(H)]
    ks = [qkv_s[:, W + h * D:W + (h + 1) * D] for h in range(H)]
    vs = [qkv_s[:, 2 * W + h * D:2 * W + (h + 1) * D] for h in range(H)]
    betas = [beta_all[:, H + h:H + h + 1] for h in range(H)]
    k_betas = [k * b for k, b in zip(ks, betas)]
    decays = [jnp.where(causal, jnp.exp(jnp.minimum(gc[:, h:h + 1] - gc_t[h:h + 1, :], 0.0)), 0.0)
              for h in range(H)]
    ms = [jnp.where(strict, _mm_bf(kb, k, NT) * dc, 0.0) for kb, k, dc in zip(k_betas, ks, decays)]
    t_invs = _unit_lower_inverses(ms, row, col)
    for h in range(H):
        rhs = jnp.concatenate([vs[h] * betas[h], k_betas[h] * e_g[:, h:h + 1]], axis=1)
        uw = _mm_x3(t_invs[h], rhs)
        u_s[h] = uw[:, 0:D]
        w_s[h] = uw[:, D:2 * D]
        attn_s[h] = _mm_bf(qs[h], ks[h], NT) * decays[h]
        qd_s[h] = qs[h] * e_g[:, h:h + 1]
        kd_s[h] = ks[h] * e_rem[:, h:h + 1]

    for c in range(n_c):
        rows = slice(c * C, (c + 1) * C)
        win = slice((c // 2) * 2 * C, (c // 2 + 1) * 2 * C)
        ss = [state_ref[h] for h in range(H)]
        wq = [_mm_bf(jnp.concatenate([w_s[h, rows, :], qd_s[h, rows, :]], axis=0), ss[h]) for h in range(H)]
        v_new = [u_s[h, rows, :] - wq[h][0:C] for h in range(H)]
        kv = [_mm_bf(kd_s[h, rows, :], v_new[h], TN) for h in range(H)]
        for h in range(H):
            gl = jnp.exp(gc[c * C + C - 1:c * C + C, h:h + 1])
            state_ref[h] = ss[h] * gl + kv[h]
            vv = jnp.concatenate([v_new[h], v_new[h]], axis=0)
            o_s[rows, h * D:(h + 1) * D] = wq[h][C:2 * C] + _mm_bf(attn_s[h, rows, win], vv)

    nw = nw_ref[...]
    for h in range(H):
        z = qkvz_ref[:, 3 * W + h * D:3 * W + (h + 1) * D]
        o_ref[:, h * D:(h + 1) * D] = _rms(o_s[:, h * D:(h + 1) * D]) * nw * _silu(z)


def _gdn(qkvz, ab, conv_w, a_log_row, dt_row, norm_w, bsz, seq, tt=256):
    n_tok = qkvz.shape[0]
    W = GDN_WIDTH
    per_b = seq // tt
    row = lambda b, j: (b * per_b + j, 0)
    const = lambda b, j: (0, 0)
    return pl.pallas_call(
        functools.partial(_gdn_kernel, tt=tt),
        out_shape=jax.ShapeDtypeStruct((n_tok, W), F32),
        grid=(bsz, per_b),
        in_specs=[
            pl.BlockSpec((tt, 4 * W), row),
            pl.BlockSpec((tt, LANES), row),
            pl.BlockSpec(conv_w.shape, const),
            pl.BlockSpec((1, LANES), const),
            pl.BlockSpec((1, LANES), const),
            pl.BlockSpec((1, GDN_HEAD_DIM), const),
        ],
        out_specs=pl.BlockSpec((tt, W), row),
        scratch_shapes=[
            pltpu.VMEM((HALO, 3 * W), F32),
            pltpu.VMEM((tt, 3 * W), F32),
            pltpu.VMEM((GDN_HEADS, tt, GDN_HEAD_DIM), F32),
            pltpu.VMEM((GDN_HEADS, tt, GDN_HEAD_DIM), F32),
            pltpu.VMEM((GDN_HEADS, tt, GDN_HEAD_DIM), F32),
            pltpu.VMEM((GDN_HEADS, tt, GDN_HEAD_DIM), F32),
            pltpu.VMEM((GDN_HEADS, tt, tt), F32),
            pltpu.VMEM((tt, W), F32),
            pltpu.VMEM((GDN_HEADS, GDN_HEAD_DIM, GDN_HEAD_DIM), F32),
        ],
        compiler_params=pltpu.CompilerParams(
            dimension_semantics=("arbitrary", "arbitrary"), vmem_limit_bytes=VMEM_LIMIT),
        name="gdn",
    )(qkvz, ab, conv_w, a_log_row, dt_row, norm_w)


def _rwkv_kernel(rw_ref, mu_ref, w0_ref, w2_ref, a0_ref, a2_ref, g2_ref, kk_ref, ka_ref, rk_ref,
                 lnw_ref, lnb_ref, o_ref,
                 carry_ref, r_s, k_s, b_s, kk_s, kend_s, bend_s, v_s, y_s, t_s, arb_s, state_ref, *, tt):
    W = RWKV_WIDTH
    C = CHUNK
    NH = RWKV_HEADS
    P2 = 2 * RWKV_HEAD_DIM
    n_c = tt // C
    n_p = NH // 2

    @pl.when(pl.program_id(1) == 0)
    def _():
        carry_ref[...] = jnp.zeros_like(carry_ref)
        state_ref[...] = jnp.zeros_like(state_ref)

    cf = rw_ref[...]
    trow = lax.broadcasted_iota(jnp.int32, (tt, 1), 0)
    prev = jnp.where(trow == 0, carry_ref[HALO - 1:HALO, :], pltpu.roll(cf, 1, axis=0))
    carry_ref[...] = cf[tt - HALO:tt]
    cf = cf + mu_ref[...] * (prev - cf)

    r = cf[:, 0:W]
    k = cf[:, W:2 * W]
    v = cf[:, 2 * W:3 * W]
    wd_ad = cf[:, 3 * W:3 * W + LANES]
    gd = cf[:, 3 * W + LANES:3 * W + 2 * LANES]

    hrow, hcol = _tri_masks(W)
    head_ones = ((hrow >> 6) == (hcol >> 6)).astype(BF16)

    w_log = -_softplus(-(w0_ref[...] + _mm_bf(jnp.tanh(wd_ad), w2_ref[...]))) - 0.5
    a = _sigmoid(a0_ref[...] + _mm_bf(wd_ad, a2_ref[...]))
    gate = _mm_bf(_sigmoid(gd), g2_ref[...])
    kkv = k * kk_ref[...]
    kk = kkv * lax.rsqrt(_mm_const_rhs(kkv * kkv, head_ones) + NORM_EPS)
    kmod = k * (1.0 + (a - 1.0) * ka_ref[...])
    bonus = _mm_const_rhs(r * kmod * rk_ref[...], head_ones) * v
    lw = -jnp.exp(w_log)

    row, col = _tri_masks(tt)
    same_chunk = (row >> 6) == (col >> 6)
    causal = jnp.logical_and(same_chunk, row >= col)
    strict = jnp.logical_and(same_chunk, row > col)
    lc = _mm_const_lhs(causal.astype(BF16), lw)
    lc_t = lc.T
    l_last = jnp.concatenate(
        [jnp.broadcast_to(lc[c * C + C - 1:c * C + C, :], (C, W)) for c in range(n_c)], axis=0)
    e_inv = jnp.exp(-lc)
    e_rem = jnp.exp(l_last - lc)
    b = kk * a
    r_s[...] = r * jnp.exp(lc)
    k_s[...] = kmod * e_inv
    b_s[...] = b * e_inv
    kk_s[...] = kk * jnp.exp(lc - lw)
    kend_s[...] = kmod * e_rem
    bend_s[...] = b * e_rem
    v_s[...] = v

    lane = lax.broadcasted_iota(jnp.int32, (1, P2), 1)
    first = lane < RWKV_HEAD_DIM
    sels = [first if h % 2 == 0 else jnp.logical_not(first) for h in range(NH)]
    lns = [slice((h // 2) * P2, (h // 2 + 1) * P2) for h in range(NH)]
    kk_h = [jnp.where(sels[h], kk_s[:, lns[h]], 0.0) for h in range(NH)]
    r_h = [jnp.where(sels[h], r_s[:, lns[h]], 0.0) for h in range(NH)]
    a_kb = [jnp.where(strict, _mm_bf(kk_h[h], b_s[:, lns[h]], NT), 0.0) for h in range(NH)]
    t_inv = _unit_lower_inverses(a_kb, row, col)
    a_kk = [jnp.where(strict, _mm_bf(kk_h[h], k_s[:, lns[h]], NT), 0.0) for h in range(NH)]
    av = [_mm_bf(a_kk[h], v_s[:, lns[h]]) for h in range(NH)]
    u = [_mm_x3(t_inv[h], av[h]) for h in range(NH)]
    a_rk = [jnp.where(causal, _mm_bf(r_h[h], k_s[:, lns[h]], NT), 0.0) for h in range(NH)]
    rv = [_mm_bf(a_rk[h], v_s[:, lns[h]]) for h in range(NH)]
    for h in range(NH):
        t_s[h] = t_inv[h]
        arb_s[h] = jnp.where(causal, _mm_bf(r_h[h], b_s[:, lns[h]], NT), 0.0)
    u_p = [jnp.where(first, u[2 * p], u[2 * p + 1]) for p in range(n_p)]
    rv_p = [jnp.where(first, rv[2 * p], rv[2 * p + 1]) for p in range(n_p)]

    brow, bcol = _tri_masks(P2)
    blockdiag = (brow >> 6) == (bcol >> 6)
    for c in range(n_c):
        rows = slice(c * C, (c + 1) * C)
        win = slice((c // 2) * 2 * C, (c // 2 + 1) * 2 * C)
        pl_ = [slice(p * P2, (p + 1) * P2) for p in range(n_p)]
        ss = [state_ref[p] for p in range(n_p)]
        x = [_mm_bf(kk_s[rows, pl_[p]], ss[p]) for p in range(n_p)]
        ys = [_mm_bf(r_s[rows, pl_[p]], ss[p]) for p in range(n_p)]
        x2 = [jnp.concatenate([xx, xx], axis=0) for xx in x]
        tx = [jnp.where(first, _mm_x3(t_s[2 * p, rows, win], x2[p]), _mm_x3(t_s[2 * p + 1, rows, win], x2[p]))
              for p in range(n_p)]
        pm = [-(tx[p] + u_p[p][rows]) for p in range(n_p)]
        upd = [_mm_bf(jnp.concatenate([bend_s[rows, pl_[p]], kend_s[rows, pl_[p]]], axis=0),
                      jnp.concatenate([pm[p], v_s[rows, pl_[p]]], axis=0), TN) for p in range(n_p)]
        for p in range(n_p):
            g_col = jnp.exp(lc_t[p * P2:(p + 1) * P2, c * C + C - 1:c * C + C])
            state_ref[p] = ss[p] * g_col + jnp.where(blockdiag, upd[p], 0.0)
            p2 = jnp.concatenate([pm[p], pm[p]], axis=0)
            y_s[rows, pl_[p]] = ys[p] + rv_p[p][rows] + jnp.where(
                first, _mm_bf(arb_s[2 * p, rows, win], p2), _mm_bf(arb_s[2 * p + 1, rows, win], p2))

    y = y_s[...]
    inv_n = 1.0 / RWKV_HEAD_DIM
    mean = _mm_const_rhs(y, head_ones) * inv_n
    d = y - mean
    var = _mm_const_rhs(d * d, head_ones) * inv_n
    yn = d * lax.rsqrt(var + RWKV_GN_EPS) * lnw_ref[...] + lnb_ref[...]
    o_ref[...] = (yn + bonus) * gate


def _rwkv(rw, mu, w0, w2p, a0, a2p, g2, k_k, k_a, r_k, ln_w, ln_b, bsz, seq, tt=256):
    n_tok, cols = rw.shape
    W = RWKV_WIDTH
    per_b = seq // tt
    row = lambda b, j: (b * per_b + j, 0)
    const = lambda b, j: (0, 0)
    vec = pl.BlockSpec((1, W), const)
    return pl.pallas_call(
        functools.partial(_rwkv_kernel, tt=tt),
        out_shape=jax.ShapeDtypeStruct((n_tok, W), F32),
        grid=(bsz, per_b),
        in_specs=[
            pl.BlockSpec((tt, cols), row),
            pl.BlockSpec((1, cols), const),
            vec,
            pl.BlockSpec(w2p.shape, const),
            vec,
            pl.BlockSpec(a2p.shape, const),
            pl.BlockSpec(g2.shape, const),
            vec, vec, vec, vec, vec,
        ],
        out_specs=pl.BlockSpec((tt, W), row),
        scratch_shapes=[pltpu.VMEM((HALO, cols), F32)]
        + [pltpu.VMEM((tt, W), F32) for _ in range(8)]
        + [pltpu.VMEM((RWKV_HEADS, tt, tt), F32) for _ in range(2)]
        + [pltpu.VMEM((RWKV_HEADS // 2, 2 * RWKV_HEAD_DIM, 2 * RWKV_HEAD_DIM), F32)],
        compiler_params=pltpu.CompilerParams(
            dimension_semantics=("arbitrary", "arbitrary"), vmem_limit_bytes=VMEM_LIMIT),
        name="rwkv7",
    )(rw, mu, w0, w2p, a0, a2p, g2, k_k, k_a, r_k, ln_w, ln_b)


def _rglru_kernel(gb_ref, xb_ref, convw_ref, convb_ref, wa_ref, ba_ref, wx_ref, bx_ref, lam_ref, o_ref,
                  halo_ref, h_ref, *, tt):
    @pl.when(pl.program_id(1) == 0)
    def _():
        halo_ref[...] = jnp.zeros_like(halo_ref)
        h_ref[...] = jnp.zeros_like(h_ref)

    x = xb_ref[...]
    xp = jnp.concatenate([halo_ref[...], x], axis=0)
    cw = convw_ref[...]
    xc = x * cw[CONV_WIDTH - 1:CONV_WIDTH] + convb_ref[...]
    for j in range(CONV_WIDTH - 1):
        o = HALO - (CONV_WIDTH - 1) + j
        xc = xc + xp[o:o + tt] * cw[j:j + 1]
    halo_ref[...] = x[tt - HALO:tt]

    blk = x.shape[1] // LRU_BLOCKS
    xc_bf = xc.astype(BF16)
    ra = jnp.concatenate(
        [_mm(xc_bf[:, n * blk:(n + 1) * blk], wa_ref[n]) for n in range(LRU_BLOCKS)], axis=1)
    ix = jnp.concatenate(
        [_mm(xc_bf[:, n * blk:(n + 1) * blk], wx_ref[n]) for n in range(LRU_BLOCKS)], axis=1)
    r = _sigmoid(ra + ba_ref[...])
    i = _sigmoid(ix + bx_ref[...])
    log_a = -LRU_C * r * _softplus(-lam_ref[...])
    a = jnp.exp(log_a)
    u = xc * i * jnp.sqrt(1.0 - jnp.exp(2.0 * log_a))

    trow = lax.broadcasted_iota(jnp.int32, (tt, 1), 0)
    d = 1
    while d < tt:
        keep = trow >= d
        a_sh = jnp.where(keep, pltpu.roll(a, d, axis=0), 1.0)
        u_sh = jnp.where(keep, pltpu.roll(u, d, axis=0), 0.0)
        u = a * u_sh + u
        a = a * a_sh
        d *= 2
    h = u + a * h_ref[HALO - 1:HALO, :]
    h_ref[...] = h[tt - HALO:tt]
    o_ref[...] = h * _gelu_tanh(gb_ref[...])


def _rglru(gb, xb, conv_w, conv_b, wa_bf, ba, wx_bf, bx, lam, bsz, seq, tt=256):
    n_tok, width = xb.shape
    per_b = seq // tt
    row = lambda b, j: (b * per_b + j, 0)
    const = lambda b, j: (0, 0)
    const3 = lambda b, j: (0, 0, 0)
    vec = pl.BlockSpec((1, width), const)
    return pl.pallas_call(
        functools.partial(_rglru_kernel, tt=tt),
        out_shape=jax.ShapeDtypeStruct((n_tok, width), F32),
        grid=(bsz, per_b),
        in_specs=[
            pl.BlockSpec((tt, width), row),
            pl.BlockSpec((tt, width), row),
            pl.BlockSpec(conv_w.shape, const),
            vec,
            pl.BlockSpec(wa_bf.shape, const3),
            vec,
            pl.BlockSpec(wx_bf.shape, const3),
            vec,
            vec,
        ],
        out_specs=pl.BlockSpec((tt, width), row),
        scratch_shapes=[pltpu.VMEM((HALO, width), F32), pltpu.VMEM((HALO, width), F32)],
        compiler_params=pltpu.CompilerParams(
            dimension_semantics=("arbitrary", "arbitrary"), vmem_limit_bytes=VMEM_LIMIT),
        name="rglru",
    )(gb, xb, conv_w, conv_b, wa_bf, ba, wx_bf, bx, lam)


def _pad_lanes(v):
    return jnp.pad(v, (0, LANES - v.shape[0])).reshape(1, LANES)


def kernel(x, c, norm_pre, norm_post, ada_w, ada_b, ffn_w_gate, ffn_w_up, ffn_w_down, mix_w_in, mix_w_out, gdn_conv_w, gdn_a_log, gdn_dt_bias, gdn_norm_w, rwkv_mu, rwkv_w0, rwkv_w2, rwkv_a0, rwkv_a2, rwkv_g2, rwkv_k_k, rwkv_k_a, rwkv_r_k, rwkv_ln_w, rwkv_ln_b, lru_w_in, lru_conv_w, lru_conv_b, lru_wa, lru_ba, lru_wx, lru_bx, lru_lambda, lru_w_out):
    bsz, seq, d = x.shape
    depth = norm_pre.shape[0]
    x2 = x.reshape(bsz * seq, d)
    mods = _ada_params(c, ada_w, ada_b)

    def mod(layer, sub):
        m = mods[layer * 2 + sub]
        return (m[:, None, 0:d], m[:, None, d:2 * d], m[:, None, 2 * d:3 * d])

    GW = GDN_WIDTH
    for layer in range(depth):
        j = layer // 2
        shift, scale, gate = mod(layer, 0)
        nw_pre = norm_pre[layer, 0].reshape(1, d)
        nw_post = norm_post[layer, 0].reshape(1, d)
        if layer % 2 == 0:
            w_in = mix_w_in[j]
            n_gdn = 4 * GW + 2 * GDN_HEADS
            w_cat = jnp.concatenate(
                [w_in[:, 0:4 * GW], w_in[:, n_gdn:],
                 jnp.pad(w_in[:, 4 * GW:n_gdn], ((0, 0), (0, LANES - 2 * GDN_HEADS)))], axis=1).astype(BF16)
            qkvz, rw, ab = _norm_proj(x2, nw_pre, shift, scale, w_cat, (4 * GW, RWKV_COLS, LANES), seq)
            out_a = _gdn(qkvz, ab, gdn_conv_w[j], _pad_lanes(gdn_a_log[j]), _pad_lanes(gdn_dt_bias[j]),
                         gdn_norm_w[j].reshape(1, GDN_HEAD_DIM), bsz, seq)
            w2p = jnp.pad(rwkv_w2[j], ((0, LANES - DECAY_LORA), (0, 0))).astype(BF16)
            a2p = jnp.pad(rwkv_a2[j], ((DECAY_LORA, LANES - DECAY_LORA - AAA_LORA), (0, 0))).astype(BF16)
            vec = lambda t: t.reshape(1, RWKV_WIDTH)
            out_b = _rwkv(rw, rwkv_mu[j].reshape(1, RWKV_COLS), vec(rwkv_w0[j]), w2p, vec(rwkv_a0[j]), a2p,
                          rwkv_g2[j].astype(BF16), vec(rwkv_k_k[j]), vec(rwkv_k_a[j]), vec(rwkv_r_k[j]),
                          vec(rwkv_ln_w[j]), vec(rwkv_ln_b[j]), bsz, seq)
            w_out = mix_w_out[j].astype(BF16)
            x2 = _proj_residual(x2, [out_a, out_b], [w_out[0:GW], w_out[GW:]], gate, nw_post, seq)
        else:
            width = lru_w_in.shape[2] // 2
            gb, xb = _norm_proj(x2, nw_pre, shift, scale, lru_w_in[j].astype(BF16), (width, width), seq)
            vec = lambda t: t.reshape(1, width)
            y = _rglru(gb, xb, lru_conv_w[j], vec(lru_conv_b[j]), lru_wa[j].astype(BF16), vec(lru_ba[j]),
                       lru_wx[j].astype(BF16), vec(lru_bx[j]), vec(lru_lambda[j]), bsz, seq)
            x2 = _proj_residual(x2, [y], [lru_w_out[j].astype(BF16)], gate, nw_post, seq)
        shift, scale, gate = mod(layer, 1)
        x2 = _ffn(x2, norm_pre[layer, 1].reshape(1, d), shift, scale, ffn_w_gate[layer].astype(BF16),
                  ffn_w_up[layer].astype(BF16), ffn_w_down[layer].astype(BF16), gate,
                  norm_post[layer, 1].reshape(1, d), seq)
    return x2.reshape(bsz, seq, d)
```

```python
import functools

import jax
import jax.numpy as jnp
from jax import lax
from jax.experimental import pallas as pl
from jax.experimental.pallas import tpu as pltpu

F32 = jnp.float32
BF16 = jnp.bfloat16

NORM_EPS = 1e-6
GDN_HEADS = 4
GDN_HEAD_DIM = 128
GDN_WIDTH = GDN_HEADS * GDN_HEAD_DIM
CHUNK = 64
CONV_WIDTH = 4
RWKV_HEADS = 8
RWKV_HEAD_DIM = 64
RWKV_WIDTH = RWKV_HEADS * RWKV_HEAD_DIM
DECAY_LORA = 64
AAA_LORA = 64
GATE_LORA = 128
RWKV_COLS = 3 * RWKV_WIDTH + DECAY_LORA + AAA_LORA + GATE_LORA
RWKV_GN_EPS = 64e-5
LRU_BLOCKS = 4
LRU_C = 8.0
LANES = 128
SUB = 8
HALO = SUB
VMEM_LIMIT = 56 * 1024 * 1024

NN = (((1,), (0,)), ((), ()))
NT = (((1,), (1,)), ((), ()))
TN = (((0,), (0,)), ((), ()))


def _mm(a, b, dims=NN):
    return lax.dot_general(a, b, dims, preferred_element_type=F32)


def _mm_bf(a, b, dims=NN):
    return _mm(a.astype(BF16), b.astype(BF16), dims)


def _split2(x):
    hi = x.astype(BF16)
    lo = (x - hi.astype(F32)).astype(BF16)
    return hi, lo


def _mm_x3(a, b, dims=NN):
    ah, al = _split2(a)
    bh, bl = _split2(b)
    return _mm(ah, bh, dims) + (_mm(ah, bl, dims) + _mm(al, bh, dims))


def _mm_const_lhs(c_bf, x):
    hi = x.astype(BF16)
    r1 = x - hi.astype(F32)
    mid = r1.astype(BF16)
    lo = (r1 - mid.astype(F32)).astype(BF16)
    return _mm(c_bf, hi) + (_mm(c_bf, mid) + _mm(c_bf, lo))


def _mm_const_rhs(x, c_bf):
    hi, lo = _split2(x)
    return _mm(hi, c_bf) + _mm(lo, c_bf)


def _sigmoid(x):
    return 1.0 / (1.0 + jnp.exp(-x))


def _silu(x):
    return x * _sigmoid(x)


def _softplus(x):
    return jnp.maximum(x, 0.0) + jnp.log1p(jnp.exp(-jnp.abs(x)))


def _gelu_tanh(x):
    return 0.5 * x * (1.0 + jnp.tanh(0.7978845608028654 * (x + 0.044715 * (x * x * x))))


def _rms(x):
    return x * lax.rsqrt(jnp.mean(x * x, axis=-1, keepdims=True) + NORM_EPS)


def _tri_masks(n):
    row = lax.broadcasted_iota(jnp.int32, (n, n), 0)
    col = lax.broadcasted_iota(jnp.int32, (n, n), 1)
    return row, col


def _unit_lower_inverses(ms, row, col):
    n = ms[0].shape[0]
    n_c = n // CHUNK
    same_chunk = (row >> 6) == (col >> 6)

    def pack(x):
        out = x[0:CHUNK]
        for c in range(1, n_c):
            out = out + x[c * CHUNK:(c + 1) * CHUNK]
        return out

    def expand(xp):
        return jnp.where(same_chunk, jnp.concatenate([xp] * n_c, axis=0), 0.0)

    eye = (row == col).astype(F32)
    eye_p = pack(eye)
    same = (row >> 3) == (col >> 3)
    ms_p = [pack(m) for m in ms]
    m8 = [jnp.where(same, m, 0.0) for m in ms]
    m8_p = [pack(a) for a in m8]
    m2_p = [_mm_bf(ap, a) for ap, a in zip(m8_p, m8)]
    m2 = [expand(ap) for ap in m2_p]
    m4 = [expand(_mm_bf(ap, a)) for ap, a in zip(m2_p, m2)]
    inv_p = [_mm_bf(eye_p - ap, eye + b) for ap, b in zip(m8_p, m2)]
    inv_p = [_mm_bf(ap, eye + b) for ap, b in zip(inv_p, m4)]
    shift = 3
    while (1 << shift) < CHUNK:
        pair = (row >> (shift + 1)) == (col >> (shift + 1))
        off_mask = jnp.logical_and(pair, jnp.logical_not(same))
        tmp_p = [_mm_bf(ap, jnp.where(off_mask, m, 0.0)) for ap, m in zip(inv_p, ms)]
        inv_p = [ap - _mm_bf(tp, expand(ap)) for ap, tp in zip(inv_p, tmp_p)]
        same = pair
        shift += 1
    res_p = [eye_p - (ap + _mm_x3(mp, expand(ap))) for ap, mp in zip(inv_p, ms_p)]
    return [expand(ap + _mm_bf(ap, expand(rp))) for ap, rp in zip(inv_p, res_p)]


def _ada_kernel(c_ref, w_ref, b_ref, o_ref):
    s = _silu(c_ref[...])
    o_ref[0] = _mm_x3(s, w_ref[0]) + b_ref[0]


def _ada_params(c, ada_w, ada_b):
    n_l, n_s, d, d3 = ada_w.shape
    n = n_l * n_s
    bsz = c.shape[0]
    tn = 1024
    return pl.pallas_call(
        _ada_kernel,
        out_shape=jax.ShapeDtypeStruct((n, bsz, d3), F32),
        grid=(n, d3 // tn),
        in_specs=[
            pl.BlockSpec((bsz, d), lambda i, j: (0, 0)),
            pl.BlockSpec((1, d, tn), lambda i, j: (i, 0, j)),
            pl.BlockSpec((1, 1, tn), lambda i, j: (i, 0, j)),
        ],
        out_specs=pl.BlockSpec((1, bsz, tn), lambda i, j: (i, 0, j)),
        compiler_params=pltpu.CompilerParams(vmem_limit_bytes=VMEM_LIMIT),
        name="ada_params",
    )(c, ada_w.reshape(n, d, d3), ada_b.reshape(n, 1, d3))


def _norm_proj_kernel(x_ref, nw_ref, shift_ref, scale_ref, w_ref, *out_refs):
    h = _rms(x_ref[...]) * nw_ref[...]
    h = (h * (1.0 + scale_ref[0]) + shift_ref[0]).astype(BF16)
    off = 0
    for o_ref in out_refs:
        n = o_ref.shape[1]
        o_ref[...] = _mm(h, w_ref[:, off:off + n])
        off += n


def _norm_proj(x2, nw, shift, scale, w_bf, splits, seq, tm=256):
    n_tok, d = x2.shape
    per_b = seq // tm
    return pl.pallas_call(
        _norm_proj_kernel,
        out_shape=[jax.ShapeDtypeStruct((n_tok, n), F32) for n in splits],
        grid=(n_tok // tm,),
        in_specs=[
            pl.BlockSpec((tm, d), lambda i: (i, 0)),
            pl.BlockSpec((1, d), lambda i: (0, 0)),
            pl.BlockSpec((1, 1, d), lambda i: (i // per_b, 0, 0)),
            pl.BlockSpec((1, 1, d), lambda i: (i // per_b, 0, 0)),
            pl.BlockSpec(w_bf.shape, lambda i: (0, 0), pipeline_mode=pl.Buffered(1)),
        ],
        out_specs=[pl.BlockSpec((tm, n), lambda i: (i, 0)) for n in splits],
        compiler_params=pltpu.CompilerParams(vmem_limit_bytes=VMEM_LIMIT),
        name="norm_proj",
    )(x2, nw, shift, scale, w_bf)


def _out_ffn_kernel(*refs, n_in):
    x_ref = refs[0]
    a_refs = refs[1:1 + n_in]
    w_refs = refs[1 + n_in:1 + 2 * n_in]
    mod_ref, nw_ref, wg_ref, wu_ref, wd_ref, o_ref = refs[1 + 2 * n_in:]
    mod = mod_ref[0]
    nw = nw_ref[...]
    y = _mm(a_refs[0][...].astype(BF16), w_refs[0][...])
    for a_ref, w_ref in zip(a_refs[1:], w_refs[1:]):
        y = y + _mm(a_ref[...].astype(BF16), w_ref[...])
    x = x_ref[...] + mod[0:1] * (_rms(y) * nw[0:1])
    h = _rms(x) * nw[1:2]
    h = (h * (1.0 + mod[2:3]) + mod[1:2]).astype(BF16)
    g = _mm(h, wg_ref[...])
    u = _mm(h, wu_ref[...])
    act = (_silu(g) * u).astype(BF16)
    y = _mm(act, wd_ref[...])
    o_ref[...] = x + mod[3:4] * (_rms(y) * nw[2:3])


def _out_ffn(x2, acts, ws_bf, mod, nw, wg_bf, wu_bf, wd_bf, seq, tm=256):
    n_tok, d = x2.shape
    per_b = seq // tm
    row = lambda i: (i, 0)
    const = lambda i: (0, 0)
    resident = lambda w: pl.BlockSpec(w.shape, const, pipeline_mode=pl.Buffered(1))
    in_specs = [pl.BlockSpec((tm, d), row)]
    in_specs += [pl.BlockSpec((tm, a.shape[1]), row) for a in acts]
    in_specs += [resident(w) for w in ws_bf]
    in_specs += [pl.BlockSpec((1,) + mod.shape[1:], lambda i: (i // per_b, 0, 0)),
                 pl.BlockSpec(nw.shape, const), resident(wg_bf), resident(wu_bf), resident(wd_bf)]
    return pl.pallas_call(
        functools.partial(_out_ffn_kernel, n_in=len(acts)),
        out_shape=jax.ShapeDtypeStruct((n_tok, d), F32),
        grid=(n_tok // tm,),
        in_specs=in_specs,
        out_specs=pl.BlockSpec((tm, d), row),
        compiler_params=pltpu.CompilerParams(vmem_limit_bytes=VMEM_LIMIT),
        name="out_ffn",
    )(x2, *acts, *ws_bf, mod, nw, wg_bf, wu_bf, wd_bf)


def _gdn_kernel(qkvz_ref, ab_ref, convw_ref, alog_ref, dtb_ref, nw_ref, o_ref,
                halo_ref, qkv_s, u_s, w_s, qd_s, kd_s, attn_s, o_s, state_ref, *, tt):
    W = GDN_WIDTH
    D = GDN_HEAD_DIM
    C = CHUNK
    H = GDN_HEADS
    n_c = tt // C

    @pl.when(pl.program_id(1) == 0)
    def _():
        halo_ref[...] = jnp.zeros_like(halo_ref)
        state_ref[...] = jnp.zeros_like(state_ref)

    x = qkvz_ref[:, 0:3 * W]
    xp = jnp.concatenate([halo_ref[...], x], axis=0)
    cw = convw_ref[...]
    y = x * cw[CONV_WIDTH - 1:CONV_WIDTH]
    for j in range(CONV_WIDTH - 1):
        o = HALO - (CONV_WIDTH - 1) + j
        y = y + xp[o:o + tt] * cw[j:j + 1]
    halo_ref[...] = x[tt - HALO:tt]
    y = _silu(y)
    for h in range(H):
        q = y[:, h * D:(h + 1) * D]
        k = y[:, W + h * D:W + (h + 1) * D]
        q = q * lax.rsqrt(jnp.sum(q * q, axis=-1, keepdims=True) + NORM_EPS) * (D ** -0.5)
        k = k * lax.rsqrt(jnp.sum(k * k, axis=-1, keepdims=True) + NORM_EPS)
        qkv_s[:, h * D:(h + 1) * D] = q
        qkv_s[:, W + h * D:W + (h + 1) * D] = k
    qkv_s[:, 2 * W:3 * W] = y[:, 2 * W:3 * W]

    ab = ab_ref[...]
    g = -jnp.exp(alog_ref[...]) * _softplus(ab + dtb_ref[...])
    beta_all = _sigmoid(ab)

    row, col = _tri_masks(tt)
    same_chunk = (row >> 6) == (col >> 6)
    causal = jnp.logical_and(same_chunk, row >= col)
    strict = jnp.logical_and(same_chunk, row > col)
    gc = _mm_const_lhs(causal.astype(BF16), g)
    gc_t = gc.T
    g_last = jnp.concatenate(
        [jnp.broadcast_to(gc[c * C + C - 1:c * C + C, :], (C, LANES)) for c in range(n_c)], axis=0)
    e_g = jnp.exp(gc)
    e_rem = jnp.exp(g_last - gc)
    qs = [qkv_s[:, h * D:(h + 1) * D] for h in range(H)]
    ks = [qkv_s[:, W + h * D:W + (h + 1) * D] for h in range(H)]
    vs = [qkv_s[:, 2 * W + h * D:2 * W + (h + 1) * D] for h in range(H)]
    betas = [beta_all[:, H + h:H + h + 1] for h in range(H)]
    k_betas = [k * b for k, b in zip(ks, betas)]
    decays = [jnp.where(causal, jnp.exp(jnp.minimum(gc[:, h:h + 1] - gc_t[h:h + 1, :], 0.0)), 0.0)
              for h in range(H)]
    ms = [jnp.where(strict, _mm_bf(kb, k, NT) * dc, 0.0) for kb, k, dc in zip(k_betas, ks, decays)]
    t_invs = _unit_lower_inverses(ms, row, col)
    for h in range(H):
        rhs = jnp.concatenate([vs[h] * betas[h], k_betas[h] * e_g[:, h:h + 1]], axis=1)
        uw = _mm_x3(t_invs[h], rhs)
        u_s[h] = uw[:, 0:D]
        w_s[h] = uw[:, D:2 * D]
        attn_s[h] = _mm_bf(qs[h], ks[h], NT) * decays[h]
        qd_s[h] = qs[h] * e_g[:, h:h + 1]
        kd_s[h] = ks[h] * e_rem[:, h:h + 1]

    for c in range(n_c):
        rows = slice(c * C, (c + 1) * C)
        win = slice((c // 2) * 2 * C, (c // 2 + 1) * 2 * C)
        ss = [state_ref[h] for h in range(H)]
        wq = [_mm_bf(jnp.concatenate([w_s[h, rows, :], qd_s[h, rows, :]], axis=0), ss[h]) for h in range(H)]
        v_new = [u_s[h, rows, :] - wq[h][0:C] for h in range(H)]
        kv = [_mm_bf(kd_s[h, rows, :], v_new[h], TN) for h in range(H)]
        for h in range(H):
            gl = jnp.exp(gc[c * C + C - 1:c * C + C, h:h + 1])
            state_ref[h] = ss[h] * gl + kv[h]
            vv = jnp.concatenate([v_new[h], v_new[h]], axis=0)
            o_s[rows, h * D:(h + 1) * D] = wq[h][C:2 * C] + _mm_bf(attn_s[h, rows, win], vv)

    nw = nw_ref[...]
    for h in range(H):
        z = qkvz_ref[:, 3 * W + h * D:3 * W + (h + 1) * D]
        o_ref[:, h * D:(h + 1) * D] = _rms(o_s[:, h * D:(h + 1) * D]) * nw * _silu(z)


def _gdn(qkvz, ab, conv_w, a_log_row, dt_row, norm_w, bsz, seq, tt=256):
    n_tok = qkvz.shape[0]
    W = GDN_WIDTH
    per_b = seq // tt
    row = lambda b, j: (b * per_b + j, 0)
    const = lambda b, j: (0, 0)
    return pl.pallas_call(
        functools.partial(_gdn_kernel, tt=tt),
        out_shape=jax.ShapeDtypeStruct((n_tok, W), F32),
        grid=(bsz, per_b),
        in_specs=[
            pl.BlockSpec((tt, 4 * W), row),
            pl.BlockSpec((tt, LANES), row),
            pl.BlockSpec(conv_w.shape, const),
            pl.BlockSpec((1, LANES), const),
            pl.BlockSpec((1, LANES), const),
            pl.BlockSpec((1, GDN_HEAD_DIM), const),
        ],
        out_specs=pl.BlockSpec((tt, W), row),
        scratch_shapes=[
            pltpu.VMEM((HALO, 3 * W), F32),
            pltpu.VMEM((tt, 3 * W), F32),
            pltpu.VMEM((GDN_HEADS, tt, GDN_HEAD_DIM), F32),
            pltpu.VMEM((GDN_HEADS, tt, GDN_HEAD_DIM), F32),
            pltpu.VMEM((GDN_HEADS, tt, GDN_HEAD_DIM), F32),
            pltpu.VMEM((GDN_HEADS, tt, GDN_HEAD_DIM), F32),
            pltpu.VMEM((GDN_HEADS, tt, tt), F32),
            pltpu.VMEM((tt, W), F32),
            pltpu.VMEM((GDN_HEADS, GDN_HEAD_DIM, GDN_HEAD_DIM), F32),
        ],
        compiler_params=pltpu.CompilerParams(
            dimension_semantics=("arbitrary", "arbitrary"), vmem_limit_bytes=VMEM_LIMIT),
        name="gdn",
    )(qkvz, ab, conv_w, a_log_row, dt_row, norm_w)


def _rwkv_kernel(rw_ref, mu_ref, w0_ref, w2_ref, a0_ref, a2_ref, g2_ref, kk_ref, ka_ref, rk_ref,
                 lnw_ref, lnb_ref, o_ref,
                 carry_ref, r_s, k_s, b_s, kk_s, kend_s, bend_s, v_s, y_s, u_s, w_s, rv_s, arb_s, state_ref, *, tt):
    W = RWKV_WIDTH
    C = CHUNK
    NH = RWKV_HEADS
    P2 = 2 * RWKV_HEAD_DIM
    n_c = tt // C
    n_p = NH // 2

    @pl.when(pl.program_id(1) == 0)
    def _():
        carry_ref[...] = jnp.zeros_like(carry_ref)
        state_ref[...] = jnp.zeros_like(state_ref)

    cf = rw_ref[...]
    trow = lax.broadcasted_iota(jnp.int32, (tt, 1), 0)
    prev = jnp.where(trow == 0, carry_ref[HALO - 1:HALO, :], pltpu.roll(cf, 1, axis=0))
    carry_ref[...] = cf[tt - HALO:tt]
    cf = cf + mu_ref[...] * (prev - cf)

    r = cf[:, 0:W]
    k = cf[:, W:2 * W]
    v = cf[:, 2 * W:3 * W]
    wd_ad = cf[:, 3 * W:3 * W + LANES]
    gd = cf[:, 3 * W + LANES:3 * W + 2 * LANES]

    hrow, hcol = _tri_masks(W)
    head_ones = ((hrow >> 6) == (hcol >> 6)).astype(BF16)

    w_log = -_softplus(-(w0_ref[...] + _mm_bf(jnp.tanh(wd_ad), w2_ref[...]))) - 0.5
    a = _sigmoid(a0_ref[...] + _mm_bf(wd_ad, a2_ref[...]))
    gate = _mm_bf(_sigmoid(gd), g2_ref[...])
    kkv = k * kk_ref[...]
    kk = kkv * lax.rsqrt(_mm_const_rhs(kkv * kkv, head_ones) + NORM_EPS)
    kmod = k * (1.0 + (a - 1.0) * ka_ref[...])
    bonus = _mm_const_rhs(r * kmod * rk_ref[...], head_ones) * v
    lw = -jnp.exp(w_log)

    row, col = _tri_masks(tt)
    same_chunk = (row >> 6) == (col >> 6)
    causal = jnp.logical_and(same_chunk, row >= col)
    strict = jnp.logical_and(same_chunk, row > col)
    lc = _mm_const_lhs(causal.astype(BF16), lw)
    lc_t = lc.T
    l_last = jnp.concatenate(
        [jnp.broadcast_to(lc[c * C + C - 1:c * C + C, :], (C, W)) for c in range(n_c)], axis=0)
    e_inv = jnp.exp(-lc)
    e_rem = jnp.exp(l_last - lc)
    b = kk * a
    r_s[...] = r * jnp.exp(lc)
    k_s[...] = kmod * e_inv
    b_s[...] = b * e_inv
    kk_s[...] = kk * jnp.exp(lc - lw)
    kend_s[...] = kmod * e_rem
    bend_s[...] = b * e_rem
    v_s[...] = v

    lane = lax.broadcasted_iota(jnp.int32, (1, P2), 1)
    first = lane < RWKV_HEAD_DIM
    sels = [first if h % 2 == 0 else jnp.logical_not(first) for h in range(NH)]
    lns = [slice((h // 2) * P2, (h // 2 + 1) * P2) for h in range(NH)]
    kk_h = [jnp.where(sels[h], kk_s[:, lns[h]], 0.0) for h in range(NH)]
    r_h = [jnp.where(sels[h], r_s[:, lns[h]], 0.0) for h in range(NH)]
    a_kb = [jnp.where(strict, _mm_bf(kk_h[h], b_s[:, lns[h]], NT), 0.0) for h in range(NH)]
    t_inv = _unit_lower_inverses(a_kb, row, col)
    a_kk = [jnp.where(strict, _mm_bf(kk_h[h], k_s[:, lns[h]], NT), 0.0) for h in range(NH)]
    av = [_mm_bf(a_kk[h], v_s[:, lns[h]]) for h in range(NH)]
    uw = [_mm_x3(t_inv[h], jnp.concatenate([av[h], kk_h[h]], axis=1)) for h in range(NH)]
    a_rk = [jnp.where(causal, _mm_bf(r_h[h], k_s[:, lns[h]], NT), 0.0) for h in range(NH)]
    rv = [_mm_bf(a_rk[h], v_s[:, lns[h]]) for h in range(NH)]
    for h in range(NH):
        arb_s[h] = jnp.where(causal, _mm_bf(r_h[h], b_s[:, lns[h]], NT), 0.0)
    for p in range(n_p):
        ln = slice(p * P2, (p + 1) * P2)
        u_s[:, ln] = jnp.where(first, uw[2 * p][:, 0:P2], uw[2 * p + 1][:, 0:P2])
        w_s[:, ln] = jnp.where(first, uw[2 * p][:, P2:2 * P2], uw[2 * p + 1][:, P2:2 * P2])
        rv_s[:, ln] = jnp.where(first, rv[2 * p], rv[2 * p + 1])

    brow, bcol = _tri_masks(P2)
    blockdiag = (brow >> 6) == (bcol >> 6)
    for c in range(n_c):
        rows = slice(c * C, (c + 1) * C)
        win = slice((c // 2) * 2 * C, (c // 2 + 1) * 2 * C)
        pl_ = [slice(p * P2, (p + 1) * P2) for p in range(n_p)]
        ss = [state_ref[p] for p in range(n_p)]
        rw = [_mm_bf(jnp.concatenate([r_s[rows, pl_[p]], w_s[rows, pl_[p]]], axis=0), ss[p]) for p in range(n_p)]
        pm = [-(rw[p][C:2 * C] + u_s[rows, pl_[p]]) for p in range(n_p)]
        upd = [_mm_bf(jnp.concatenate([bend_s[rows, pl_[p]], kend_s[rows, pl_[p]]], axis=0),
                      jnp.concatenate([pm[p], v_s[rows, pl_[p]]], axis=0), TN) for p in range(n_p)]
        for p in range(n_p):
            g_col = jnp.exp(lc_t[p * P2:(p + 1) * P2, c * C + C - 1:c * C + C])
            state_ref[p] = ss[p] * g_col + jnp.where(blockdiag, upd[p], 0.0)
            p2 = jnp.concatenate([pm[p], pm[p]], axis=0)
            y_s[rows, pl_[p]] = rw[p][0:C] + rv_s[rows, pl_[p]] + jnp.where(
                first, _mm_bf(arb_s[2 * p, rows, win], p2), _mm_bf(arb_s[2 * p + 1, rows, win], p2))

    y = y_s[...]
    inv_n = 1.0 / RWKV_HEAD_DIM
    mean = _mm_const_rhs(y, head_ones) * inv_n
    d = y - mean
    var = _mm_const_rhs(d * d, head_ones) * inv_n
    yn = d * lax.rsqrt(var + RWKV_GN_EPS) * lnw_ref[...] + lnb_ref[...]
    o_ref[...] = (yn + bonus) * gate


def _rwkv(rw, mu, w0, w2p, a0, a2p, g2, k_k, k_a, r_k, ln_w, ln_b, bsz, seq, tt=256):
    n_tok, cols = rw.shape
    W = RWKV_WIDTH
    per_b = seq // tt
    row = lambda b, j: (b * per_b + j, 0)
    const = lambda b, j: (0, 0)
    vec = pl.BlockSpec((1, W), const)
    return pl.pallas_call(
        functools.partial(_rwkv_kernel, tt=tt),
        out_shape=jax.ShapeDtypeStruct((n_tok, W), F32),
        grid=(bsz, per_b),
        in_specs=[
            pl.BlockSpec((tt, cols), row),
            pl.BlockSpec((1, cols), const),
            vec,
            pl.BlockSpec(w2p.shape, const),
            vec,
            pl.BlockSpec(a2p.shape, const),
            pl.BlockSpec(g2.shape, const),
            vec, vec, vec, vec, vec,
        ],
        out_specs=pl.BlockSpec((tt, W), row),
        scratch_shapes=[pltpu.VMEM((HALO, cols), F32)]
        + [pltpu.VMEM((tt, W), F32) for _ in range(11)]
        + [pltpu.VMEM((RWKV_HEADS, tt, tt), F32)]
        + [pltpu.VMEM((RWKV_HEADS // 2, 2 * RWKV_HEAD_DIM, 2 * RWKV_HEAD_DIM), F32)],
        compiler_params=pltpu.CompilerParams(
            dimension_semantics=("arbitrary", "arbitrary"), vmem_limit_bytes=VMEM_LIMIT),
        name="rwkv7",
    )(rw, mu, w0, w2p, a0, a2p, g2, k_k, k_a, r_k, ln_w, ln_b)


def _rglru_kernel(gb_ref, xb_ref, convw_ref, convb_ref, wa_ref, ba_ref, wx_ref, bx_ref, lam_ref, o_ref,
                  halo_ref, h_ref, *, tt):
    @pl.when(pl.program_id(1) == 0)
    def _():
        halo_ref[...] = jnp.zeros_like(halo_ref)
        h_ref[...] = jnp.zeros_like(h_ref)

    x = xb_ref[...]
    xp = jnp.concatenate([halo_ref[...], x], axis=0)
    cw = convw_ref[...]
    xc = x * cw[CONV_WIDTH - 1:CONV_WIDTH] + convb_ref[...]
    for j in range(CONV_WIDTH - 1):
        o = HALO - (CONV_WIDTH - 1) + j
        xc = xc + xp[o:o + tt] * cw[j:j + 1]
    halo_ref[...] = x[tt - HALO:tt]

    blk = x.shape[1] // LRU_BLOCKS
    xc_bf = xc.astype(BF16)
    ra = jnp.concatenate(
        [_mm(xc_bf[:, n * blk:(n + 1) * blk], wa_ref[n]) for n in range(LRU_BLOCKS)], axis=1)
    ix = jnp.concatenate(
        [_mm(xc_bf[:, n * blk:(n + 1) * blk], wx_ref[n]) for n in range(LRU_BLOCKS)], axis=1)
    r = _sigmoid(ra + ba_ref[...])
    i = _sigmoid(ix + bx_ref[...])
    log_a = -LRU_C * r * _softplus(-lam_ref[...])
    a = jnp.exp(log_a)
    u = xc * i * jnp.sqrt(1.0 - jnp.exp(2.0 * log_a))

    width = x.shape[1]
    a = a.reshape(tt // SUB, SUB, width)
    u = u.reshape(tt // SUB, SUB, width)
    sub = lax.broadcasted_iota(jnp.int32, (1, SUB, 1), 1)
    d = 1
    while d < SUB:
        keep = sub >= d
        a_sh = jnp.where(keep, pltpu.roll(a, d, axis=1), 1.0)
        u_sh = jnp.where(keep, pltpu.roll(u, d, axis=1), 0.0)
        u = a * u_sh + u
        a = a * a_sh
        d *= 2
    gate = _gelu_tanh(gb_ref[...])
    carry = h_ref[...]
    for g in range(tt // SUB):
        rows = slice(g * SUB, (g + 1) * SUB)
        h = u[g] + a[g] * carry
        o_ref[rows, :] = h * gate[rows]
        carry = jnp.broadcast_to(h[SUB - 1:SUB, :], h.shape)
    h_ref[...] = carry


def _rglru(gb, xb, conv_w, conv_b, wa_bf, ba, wx_bf, bx, lam, bsz, seq, tt=256):
    n_tok, width = xb.shape
    per_b = seq // tt
    row = lambda b, j: (b * per_b + j, 0)
    const = lambda b, j: (0, 0)
    const3 = lambda b, j: (0, 0, 0)
    vec = pl.BlockSpec((1, width), const)
    return pl.pallas_call(
        functools.partial(_rglru_kernel, tt=tt),
        out_shape=jax.ShapeDtypeStruct((n_tok, width), F32),
        grid=(bsz, per_b),
        in_specs=[
            pl.BlockSpec((tt, width), row),
            pl.BlockSpec((tt, width), row),
            pl.BlockSpec(conv_w.shape, const),
            vec,
            pl.BlockSpec(wa_bf.shape, const3),
            vec,
            pl.BlockSpec(wx_bf.shape, const3),
            vec,
            vec,
        ],
        out_specs=pl.BlockSpec((tt, width), row),
        scratch_shapes=[pltpu.VMEM((HALO, width), F32), pltpu.VMEM((HALO, width), F32)],
        compiler_params=pltpu.CompilerParams(
            dimension_semantics=("arbitrary", "arbitrary"), vmem_limit_bytes=VMEM_LIMIT),
        name="rglru",
    )(gb, xb, conv_w, conv_b, wa_bf, ba, wx_bf, bx, lam)


def _pad_lanes(v):
    return jnp.pad(v, (0, LANES - v.shape[0])).reshape(1, LANES)


def kernel(x, c, norm_pre, norm_post, ada_w, ada_b, ffn_w_gate, ffn_w_up, ffn_w_down, mix_w_in, mix_w_out, gdn_conv_w, gdn_a_log, gdn_dt_bias, gdn_norm_w, rwkv_mu, rwkv_w0, rwkv_w2, rwkv_a0, rwkv_a2, rwkv_g2, rwkv_k_k, rwkv_k_a, rwkv_r_k, rwkv_ln_w, rwkv_ln_b, lru_w_in, lru_conv_w, lru_conv_b, lru_wa, lru_ba, lru_wx, lru_bx, lru_lambda, lru_w_out):
    bsz, seq, d = x.shape
    depth = norm_pre.shape[0]
    x2 = x.reshape(bsz * seq, d)
    mods = _ada_params(c, ada_w, ada_b)

    def mod(layer, sub):
        m = mods[layer * 2 + sub]
        return (m[:, None, 0:d], m[:, None, d:2 * d], m[:, None, 2 * d:3 * d])

    GW = GDN_WIDTH
    for layer in range(depth):
        j = layer // 2
        shift, scale, gate = mod(layer, 0)
        shift2, scale2, gate2 = mod(layer, 1)
        mod_rows = jnp.concatenate([gate, shift2, scale2, gate2], axis=1)
        nw_rows = jnp.stack([norm_post[layer, 0], norm_pre[layer, 1], norm_post[layer, 1]])
        nw_pre = norm_pre[layer, 0].reshape(1, d)
        ffn_w = (ffn_w_gate[layer].astype(BF16), ffn_w_up[layer].astype(BF16), ffn_w_down[layer].astype(BF16))
        if layer % 2 == 0:
            w_in = mix_w_in[j]
            n_gdn = 4 * GW + 2 * GDN_HEADS
            w_cat = jnp.concatenate(
                [w_in[:, 0:4 * GW], w_in[:, n_gdn:],
                 jnp.pad(w_in[:, 4 * GW:n_gdn], ((0, 0), (0, LANES - 2 * GDN_HEADS)))], axis=1).astype(BF16)
            qkvz, rw, ab = _norm_proj(x2, nw_pre, shift, scale, w_cat, (4 * GW, RWKV_COLS, LANES), seq)
            out_a = _gdn(qkvz, ab, gdn_conv_w[j], _pad_lanes(gdn_a_log[j]), _pad_lanes(gdn_dt_bias[j]),
                         gdn_norm_w[j].reshape(1, GDN_HEAD_DIM), bsz, seq)
            w2p = jnp.pad(rwkv_w2[j], ((0, LANES - DECAY_LORA), (0, 0))).astype(BF16)
            a2p = jnp.pad(rwkv_a2[j], ((DECAY_LORA, LANES - DECAY_LORA - AAA_LORA), (0, 0))).astype(BF16)
            vec = lambda t: t.reshape(1, RWKV_WIDTH)
            out_b = _rwkv(rw, rwkv_mu[j].reshape(1, RWKV_COLS), vec(rwkv_w0[j]), w2p, vec(rwkv_a0[j]), a2p,
                          rwkv_g2[j].astype(BF16), vec(rwkv_k_k[j]), vec(rwkv_k_a[j]), vec(rwkv_r_k[j]),
                          vec(rwkv_ln_w[j]), vec(rwkv_ln_b[j]), bsz, seq)
            w_out = mix_w_out[j].astype(BF16)
            x2 = _out_ffn(x2, [out_a, out_b], [w_out[0:GW], w_out[GW:]], mod_rows, nw_rows, *ffn_w, seq)
        else:
            width = lru_w_in.shape[2] // 2
            gb, xb = _norm_proj(x2, nw_pre, shift, scale, lru_w_in[j].astype(BF16), (width, width), seq)
            vec = lambda t: t.reshape(1, width)
            y = _rglru(gb, xb, lru_conv_w[j], vec(lru_conv_b[j]), lru_wa[j].astype(BF16), vec(lru_ba[j]),
                       lru_wx[j].astype(BF16), vec(lru_bx[j]), vec(lru_lambda[j]), bsz, seq)
            x2 = _out_ffn(x2, [y], [lru_w_out[j].astype(BF16)], mod_rows, nw_rows, *ffn_w, seq)
    return x2.reshape(bsz, seq, d)
```

```python
import functools

import jax
import jax.numpy as jnp
from jax import lax
from jax.experimental import pallas as pl
from jax.experimental.pallas import tpu as pltpu

F32 = jnp.float32
BF16 = jnp.bfloat16

NORM_EPS = 1e-6
GDN_HEADS = 4
GDN_HEAD_DIM = 128
GDN_WIDTH = GDN_HEADS * GDN_HEAD_DIM
CHUNK = 64
CONV_WIDTH = 4
RWKV_HEADS = 8
RWKV_HEAD_DIM = 64
RWKV_WIDTH = RWKV_HEADS * RWKV_HEAD_DIM
DECAY_LORA = 64
AAA_LORA = 64
GATE_LORA = 128
RWKV_COLS = 3 * RWKV_WIDTH + DECAY_LORA + AAA_LORA + GATE_LORA
RWKV_GN_EPS = 64e-5
LRU_BLOCKS = 4
LRU_C = 8.0
LANES = 128
MXU_DIM = 256
SUB = 8
HALO = SUB
VMEM_LIMIT = 56 * 1024 * 1024

NN = (((1,), (0,)), ((), ()))
NT = (((1,), (1,)), ((), ()))
TN = (((0,), (0,)), ((), ()))


def _mm(a, b, dims=NN):
    return lax.dot_general(a, b, dims, preferred_element_type=F32)


def _mm_bf(a, b, dims=NN):
    return _mm(a.astype(BF16), b.astype(BF16), dims)


def _split2(x):
    hi = x.astype(BF16)
    lo = (x - hi.astype(F32)).astype(BF16)
    return hi, lo


def _mm_x3(a, b, dims=NN):
    ah, al = _split2(a)
    bh, bl = _split2(b)
    return _mm(ah, bh, dims) + (_mm(ah, bl, dims) + _mm(al, bh, dims))


def _mm_const_lhs(c_bf, x):
    hi = x.astype(BF16)
    r1 = x - hi.astype(F32)
    mid = r1.astype(BF16)
    lo = (r1 - mid.astype(F32)).astype(BF16)
    return _mm(c_bf, hi) + (_mm(c_bf, mid) + _mm(c_bf, lo))


def _mm_const_rhs(x, c_bf):
    hi, lo = _split2(x)
    return _mm(hi, c_bf) + _mm(lo, c_bf)


def _sigmoid(x):
    return 1.0 / (1.0 + jnp.exp(-x))


def _silu(x):
    return x * _sigmoid(x)


def _softplus(x):
    return jnp.maximum(x, 0.0) + jnp.log1p(jnp.exp(-jnp.abs(x)))


def _gelu_tanh(x):
    return 0.5 * x * (1.0 + jnp.tanh(0.7978845608028654 * (x + 0.044715 * (x * x * x))))


def _rms(x):
    return x * lax.rsqrt(jnp.mean(x * x, axis=-1, keepdims=True) + NORM_EPS)


def _tri_masks(n):
    row = lax.broadcasted_iota(jnp.int32, (n, n), 0)
    col = lax.broadcasted_iota(jnp.int32, (n, n), 1)
    return row, col


def _unit_lower_inverses(ms, row, col):
    n = ms[0].shape[0]
    n_c = n // CHUNK
    same_chunk = (row >> 6) == (col >> 6)

    def pack(x):
        out = x[0:CHUNK]
        for c in range(1, n_c):
            out = out + x[c * CHUNK:(c + 1) * CHUNK]
        return out

    def expand(xp):
        return jnp.where(same_chunk, jnp.concatenate([xp] * n_c, axis=0), 0.0)

    eye = (row == col).astype(F32)
    eye_p = pack(eye)
    same = (row >> 3) == (col >> 3)
    ms_p = [pack(m) for m in ms]
    m8 = [jnp.where(same, m, 0.0) for m in ms]
    m8_p = [pack(a) for a in m8]
    m2_p = [_mm_bf(ap, a) for ap, a in zip(m8_p, m8)]
    m2 = [expand(ap) for ap in m2_p]
    m4 = [expand(_mm_bf(ap, a)) for ap, a in zip(m2_p, m2)]
    inv_p = [_mm_bf(eye_p - ap, eye + b) for ap, b in zip(m8_p, m2)]
    inv_p = [_mm_bf(ap, eye + b) for ap, b in zip(inv_p, m4)]
    shift = 3
    while (1 << shift) < CHUNK:
        pair = (row >> (shift + 1)) == (col >> (shift + 1))
        off_mask = jnp.logical_and(pair, jnp.logical_not(same))
        tmp_p = [_mm_bf(ap, jnp.where(off_mask, m, 0.0)) for ap, m in zip(inv_p, ms)]
        inv_p = [ap - _mm_bf(tp, expand(ap)) for ap, tp in zip(inv_p, tmp_p)]
        same = pair
        shift += 1
    res_p = [eye_p - (ap + _mm_x3(mp, expand(ap))) for ap, mp in zip(inv_p, ms_p)]
    return [expand(ap + _mm_bf(ap, expand(rp))) for ap, rp in zip(inv_p, res_p)]


def _ada_kernel(c_ref, w_ref, b_ref, o_ref):
    s = _silu(c_ref[...])
    o_ref[0] = _mm_x3(s, w_ref[0]) + b_ref[0]


def _ada_params(c, ada_w, ada_b):
    n_l, n_s, d, d3 = ada_w.shape
    n = n_l * n_s
    bsz = c.shape[0]
    tn = 1024
    return pl.pallas_call(
        _ada_kernel,
        out_shape=jax.ShapeDtypeStruct((n, bsz, d3), F32),
        grid=(n, d3 // tn),
        in_specs=[
            pl.BlockSpec((bsz, d), lambda i, j: (0, 0)),
            pl.BlockSpec((1, d, tn), lambda i, j: (i, 0, j)),
            pl.BlockSpec((1, 1, tn), lambda i, j: (i, 0, j)),
        ],
        out_specs=pl.BlockSpec((1, bsz, tn), lambda i, j: (i, 0, j)),
        compiler_params=pltpu.CompilerParams(vmem_limit_bytes=VMEM_LIMIT),
        name="ada_params",
    )(c, ada_w.reshape(n, d, d3), ada_b.reshape(n, 1, d3))


def _norm_proj_kernel(x_ref, nw_ref, shift_ref, scale_ref, w_ref, *out_refs):
    h = _rms(x_ref[...]) * nw_ref[...]
    h = (h * (1.0 + scale_ref[0]) + shift_ref[0]).astype(BF16)
    off = 0
    for o_ref in out_refs:
        n = o_ref.shape[1]
        o_ref[...] = _mm(h, w_ref[:, off:off + n])
        off += n


def _norm_proj(x2, nw, shift, scale, w_bf, splits, seq, tm=256):
    n_tok, d = x2.shape
    per_b = seq // tm
    return pl.pallas_call(
        _norm_proj_kernel,
        out_shape=[jax.ShapeDtypeStruct((n_tok, n), F32) for n in splits],
        grid=(n_tok // tm,),
        in_specs=[
            pl.BlockSpec((tm, d), lambda i: (i, 0)),
            pl.BlockSpec((1, d), lambda i: (0, 0)),
            pl.BlockSpec((1, 1, d), lambda i: (i // per_b, 0, 0)),
            pl.BlockSpec((1, 1, d), lambda i: (i // per_b, 0, 0)),
            pl.BlockSpec(w_bf.shape, lambda i: (0, 0), pipeline_mode=pl.Buffered(1)),
        ],
        out_specs=[pl.BlockSpec((tm, n), lambda i: (i, 0)) for n in splits],
        compiler_params=pltpu.CompilerParams(vmem_limit_bytes=VMEM_LIMIT),
        name="norm_proj",
    )(x2, nw, shift, scale, w_bf)


def _out_ffn_kernel(*refs, n_in):
    x_ref = refs[0]
    a_refs = refs[1:1 + n_in]
    w_refs = refs[1 + n_in:1 + 2 * n_in]
    mod_ref, nw_ref, wg_ref, wu_ref, wd_ref, o_ref = refs[1 + 2 * n_in:]
    mod = mod_ref[0]
    nw = nw_ref[...]
    tm = x_ref.shape[0]
    halves = [slice(0, tm // 2), slice(tm // 2, tm)]
    ys = []
    for rows in halves:
        y = _mm(a_refs[0][rows, :].astype(BF16), w_refs[0][...])
        for a_ref, w_ref in zip(a_refs[1:], w_refs[1:]):
            y = y + _mm(a_ref[rows, :].astype(BF16), w_ref[...])
        ys.append(y)
    xs = [x_ref[rows, :] + mod[0:1] * (_rms(y) * nw[0:1]) for rows, y in zip(halves, ys)]
    hs = [((_rms(x) * nw[1:2]) * (1.0 + mod[2:3]) + mod[1:2]).astype(BF16) for x in xs]
    gs = [_mm(h, wg_ref[...]) for h in hs]
    us = [_mm(h, wu_ref[...]) for h in hs]
    acts = [(_silu(g) * u).astype(BF16) for g, u in zip(gs, us)]
    ys = [_mm(act, wd_ref[...]) for act in acts]
    for rows, x, y in zip(halves, xs, ys):
        o_ref[rows, :] = x + mod[3:4] * (_rms(y) * nw[2:3])


def _out_ffn(x2, acts, ws_bf, mod, nw, wg_bf, wu_bf, wd_bf, seq, tm=512):
    n_tok, d = x2.shape
    per_b = seq // tm
    row = lambda i: (i, 0)
    const = lambda i: (0, 0)
    resident = lambda w: pl.BlockSpec(w.shape, const, pipeline_mode=pl.Buffered(1))
    in_specs = [pl.BlockSpec((tm, d), row)]
    in_specs += [pl.BlockSpec((tm, a.shape[1]), row) for a in acts]
    in_specs += [resident(w) for w in ws_bf]
    in_specs += [pl.BlockSpec((1,) + mod.shape[1:], lambda i: (i // per_b, 0, 0)),
                 pl.BlockSpec(nw.shape, const), resident(wg_bf), resident(wu_bf), resident(wd_bf)]
    return pl.pallas_call(
        functools.partial(_out_ffn_kernel, n_in=len(acts)),
        out_shape=jax.ShapeDtypeStruct((n_tok, d), F32),
        grid=(n_tok // tm,),
        in_specs=in_specs,
        out_specs=pl.BlockSpec((tm, d), row),
        compiler_params=pltpu.CompilerParams(vmem_limit_bytes=VMEM_LIMIT),
        name="out_ffn",
    )(x2, *acts, *ws_bf, mod, nw, wg_bf, wu_bf, wd_bf)


def _gdn_kernel(qkvz_ref, ab_ref, convw_ref, alog_ref, dtb_ref, nw_ref, o_ref,
                halo_ref, qkv_s, u_s, w_s, qd_s, kd_s, attn_s, o_s, state_ref, *, tt):
    W = GDN_WIDTH
    D = GDN_HEAD_DIM
    C = CHUNK
    H = GDN_HEADS
    n_c = tt // C

    @pl.when(pl.program_id(1) == 0)
    def _():
        halo_ref[...] = jnp.zeros_like(halo_ref)
        state_ref[...] = jnp.zeros_like(state_ref)

    x = qkvz_ref[:, 0:3 * W]
    xp = jnp.concatenate([halo_ref[...], x], axis=0)
    cw = convw_ref[...]
    y = x * cw[CONV_WIDTH - 1:CONV_WIDTH]
    for j in range(CONV_WIDTH - 1):
        o = HALO - (CONV_WIDTH - 1) + j
        y = y + xp[o:o + tt] * cw[j:j + 1]
    halo_ref[...] = x[tt - HALO:tt]
    y = _silu(y)
    for h in range(H):
        q = y[:, h * D:(h + 1) * D]
        k = y[:, W + h * D:W + (h + 1) * D]
        q = q * lax.rsqrt(jnp.sum(q * q, axis=-1, keepdims=True) + NORM_EPS) * (D ** -0.5)
        k = k * lax.rsqrt(jnp.sum(k * k, axis=-1, keepdims=True) + NORM_EPS)
        qkv_s[:, h * D:(h + 1) * D] = q
        qkv_s[:, W + h * D:W + (h + 1) * D] = k
    qkv_s[:, 2 * W:3 * W] = y[:, 2 * W:3 * W]

    ab = ab_ref[...]
    g = -jnp.exp(alog_ref[...]) * _softplus(ab + dtb_ref[...])
    beta_all = _sigmoid(ab)

    row, col = _tri_masks(tt)
    same_chunk = (row >> 6) == (col >> 6)
    causal = jnp.logical_and(same_chunk, row >= col)
    strict = jnp.logical_and(same_chunk, row > col)
    gc = _mm_const_lhs(causal.astype(BF16), g)
    gc_t = gc.T
    g_last = jnp.concatenate(
        [jnp.broadcast_to(gc[c * C + C - 1:c * C + C, :], (C, LANES)) for c in range(n_c)], axis=0)
    e_g = jnp.exp(gc)
    e_rem = jnp.exp(g_last - gc)
    qs = [qkv_s[:, h * D:(h + 1) * D] for h in range(H)]
    ks = [qkv_s[:, W + h * D:W + (h + 1) * D] for h in range(H)]
    vs = [qkv_s[:, 2 * W + h * D:2 * W + (h + 1) * D] for h in range(H)]
    betas = [beta_all[:, H + h:H + h + 1] for h in range(H)]
    k_betas = [k * b for k, b in zip(ks, betas)]
    decays = [jnp.where(causal, jnp.exp(jnp.minimum(gc[:, h:h + 1] - gc_t[h:h + 1, :], 0.0)), 0.0)
              for h in range(H)]
    ms = [jnp.where(strict, _mm_bf(kb, k, NT) * dc, 0.0) for kb, k, dc in zip(k_betas, ks, decays)]
    t_invs = _unit_lower_inverses(ms, row, col)
    for h in range(H):
        rhs = jnp.concatenate([vs[h] * betas[h], k_betas[h] * e_g[:, h:h + 1]], axis=1)
        uw = _mm_x3(t_invs[h], rhs)
        u_s[h] = uw[:, 0:D]
        w_s[h] = uw[:, D:2 * D]
        attn_s[h] = _mm_bf(qs[h], ks[h], NT) * decays[h]
        qd_s[h] = qs[h] * e_g[:, h:h + 1]
        kd_s[h] = ks[h] * e_rem[:, h:h + 1]

    for c in range(n_c):
        rows = slice(c * C, (c + 1) * C)
        win = slice((c // 2) * 2 * C, (c // 2 + 1) * 2 * C)
        ss = [state_ref[h] for h in range(H)]
        wq = [_mm_bf(jnp.concatenate([w_s[h, rows, :], qd_s[h, rows, :]], axis=0), ss[h]) for h in range(H)]
        v_new = [u_s[h, rows, :] - wq[h][0:C] for h in range(H)]
        kv = [_mm_bf(kd_s[h, rows, :], v_new[h], TN) for h in range(H)]
        for h in range(H):
            gl = jnp.exp(gc[c * C + C - 1:c * C + C, h:h + 1])
            state_ref[h] = ss[h] * gl + kv[h]
            vv = jnp.concatenate([v_new[h], v_new[h]], axis=0)
            o_s[rows, h * D:(h + 1) * D] = wq[h][C:2 * C] + _mm_bf(attn_s[h, rows, win], vv)

    nw = nw_ref[...]
    for h in range(H):
        z = qkvz_ref[:, 3 * W + h * D:3 * W + (h + 1) * D]
        o_ref[:, h * D:(h + 1) * D] = _rms(o_s[:, h * D:(h + 1) * D]) * nw * _silu(z)


def _gdn(qkvz, ab, conv_w, a_log_row, dt_row, norm_w, bsz, seq, tt=256):
    n_tok = qkvz.shape[0]
    W = GDN_WIDTH
    per_b = seq // tt
    row = lambda b, j: (b * per_b + j, 0)
    const = lambda b, j: (0, 0)
    return pl.pallas_call(
        functools.partial(_gdn_kernel, tt=tt),
        out_shape=jax.ShapeDtypeStruct((n_tok, W), F32),
        grid=(bsz, per_b),
        in_specs=[
            pl.BlockSpec((tt, 4 * W), row),
            pl.BlockSpec((tt, LANES), row),
            pl.BlockSpec(conv_w.shape, const),
            pl.BlockSpec((1, LANES), const),
            pl.BlockSpec((1, LANES), const),
            pl.BlockSpec((1, GDN_HEAD_DIM), const),
        ],
        out_specs=pl.BlockSpec((tt, W), row),
        scratch_shapes=[
            pltpu.VMEM((HALO, 3 * W), F32),
            pltpu.VMEM((tt, 3 * W), F32),
            pltpu.VMEM((GDN_HEADS, tt, GDN_HEAD_DIM), F32),
            pltpu.VMEM((GDN_HEADS, tt, GDN_HEAD_DIM), F32),
            pltpu.VMEM((GDN_HEADS, tt, GDN_HEAD_DIM), F32),
            pltpu.VMEM((GDN_HEADS, tt, GDN_HEAD_DIM), F32),
            pltpu.VMEM((GDN_HEADS, tt, tt), F32),
            pltpu.VMEM((tt, W), F32),
            pltpu.VMEM((GDN_HEADS, GDN_HEAD_DIM, GDN_HEAD_DIM), F32),
        ],
        compiler_params=pltpu.CompilerParams(
            dimension_semantics=("arbitrary", "arbitrary"), vmem_limit_bytes=VMEM_LIMIT),
        name="gdn",
    )(qkvz, ab, conv_w, a_log_row, dt_row, norm_w)


def _rwkv_kernel(rw_ref, mu_ref, w0_ref, w2_ref, a0_ref, a2_ref, g2_ref, kk_ref, ka_ref, rk_ref,
                 lnw_ref, lnb_ref, o_ref,
                 carry_ref, r_s, k_s, b_s, kk_s, kend_s, bend_s, v_s, y_s, u_s, w_s, rv_s, bonus_s, gate_s,
                 lct_s, arb_s, state_ref, *, tt, nb):
    W = RWKV_WIDTH
    C = CHUNK
    NH = RWKV_HEADS
    P2 = 2 * RWKV_HEAD_DIM
    n_c = tt // C
    n_p = NH // 2

    @pl.when(pl.program_id(1) == 0)
    def _():
        carry_ref[...] = jnp.zeros_like(carry_ref)
        state_ref[...] = jnp.zeros_like(state_ref)

    hrow, hcol = _tri_masks(MXU_DIM)
    head_ones = ((hrow >> 6) == (hcol >> 6)).astype(BF16)

    def head_sums(x):
        return jnp.concatenate(
            [_mm_const_rhs(x[:, o:o + MXU_DIM], head_ones) for o in range(0, W, MXU_DIM)], axis=1)

    row, col = _tri_masks(tt)
    same_chunk = (row >> 6) == (col >> 6)
    causal = jnp.logical_and(same_chunk, row >= col)
    strict = jnp.logical_and(same_chunk, row > col)
    tril_bf = causal.astype(BF16)
    trow = lax.broadcasted_iota(jnp.int32, (tt, 1), 0)

    for bi in range(nb):
        cf = rw_ref[bi]
        prev = jnp.where(trow == 0, carry_ref[bi, HALO - 1:HALO, :], pltpu.roll(cf, 1, axis=0))
        carry_ref[bi] = cf[tt - HALO:tt]
        cf = cf + mu_ref[...] * (prev - cf)

        r = cf[:, 0:W]
        k = cf[:, W:2 * W]
        v = cf[:, 2 * W:3 * W]
        wd_ad = cf[:, 3 * W:3 * W + LANES]
        gd = cf[:, 3 * W + LANES:3 * W + 2 * LANES]

        w_log = -_softplus(-(w0_ref[...] + _mm_bf(jnp.tanh(wd_ad), w2_ref[...]))) - 0.5
        a = _sigmoid(a0_ref[...] + _mm_bf(wd_ad, a2_ref[...]))
        gate_s[bi] = _mm_bf(_sigmoid(gd), g2_ref[...])
        kkv = k * kk_ref[...]
        kk = kkv * lax.rsqrt(head_sums(kkv * kkv) + NORM_EPS)
        kmod = k * (1.0 + (a - 1.0) * ka_ref[...])
        bonus_s[bi] = head_sums(r * kmod * rk_ref[...]) * v
        lw = -jnp.exp(w_log)

        lc = _mm_const_lhs(tril_bf, lw)
        lct_s[bi] = lc.T
        l_last = jnp.concatenate(
            [jnp.broadcast_to(lc[c * C + C - 1:c * C + C, :], (C, W)) for c in range(n_c)], axis=0)
        e_inv = jnp.exp(-lc)
        e_rem = jnp.exp(l_last - lc)
        b = kk * a
        r_s[bi] = r * jnp.exp(lc)
        k_s[bi] = kmod * e_inv
        b_s[bi] = b * e_inv
        kk_s[bi] = kk * jnp.exp(lc - lw)
        kend_s[bi] = kmod * e_rem
        bend_s[bi] = b * e_rem
        v_s[bi] = v

    lane = lax.broadcasted_iota(jnp.int32, (1, P2), 1)
    first = lane < RWKV_HEAD_DIM
    units = [(bi, h) for bi in range(nb) for h in range(NH)]
    sel = lambda h: first if h % 2 == 0 else jnp.logical_not(first)
    lns = lambda h: slice((h // 2) * P2, (h // 2 + 1) * P2)
    kk_h = [jnp.where(sel(h), kk_s[bi, :, lns(h)], 0.0) for bi, h in units]
    r_h = [jnp.where(sel(h), r_s[bi, :, lns(h)], 0.0) for bi, h in units]
    a_kb = [jnp.where(strict, _mm_bf(x, b_s[bi, :, lns(h)], NT), 0.0) for x, (bi, h) in zip(kk_h, units)]
    t_inv = _unit_lower_inverses(a_kb, row, col)
    a_kk = [jnp.where(strict, _mm_bf(x, k_s[bi, :, lns(h)], NT), 0.0) for x, (bi, h) in zip(kk_h, units)]
    av = [_mm_bf(x, v_s[bi, :, lns(h)]) for x, (bi, h) in zip(a_kk, units)]
    uw = [_mm_x3(t, jnp.concatenate([x, y], axis=1)) for t, x, y in zip(t_inv, av, kk_h)]
    a_rk = [jnp.where(causal, _mm_bf(x, k_s[bi, :, lns(h)], NT), 0.0) for x, (bi, h) in zip(r_h, units)]
    rv = [_mm_bf(x, v_s[bi, :, lns(h)]) for x, (bi, h) in zip(a_rk, units)]
    for i, (bi, h) in enumerate(units):
        arb_s[i] = jnp.where(causal, _mm_bf(r_h[i], b_s[bi, :, lns(h)], NT), 0.0)
    pairs = [(bi, p) for bi in range(nb) for p in range(n_p)]
    for bi, p in pairs:
        ln = slice(p * P2, (p + 1) * P2)
        i = bi * NH + 2 * p
        u_s[bi, :, ln] = jnp.where(first, uw[i][:, 0:P2], uw[i + 1][:, 0:P2])
        w_s[bi, :, ln] = jnp.where(first, uw[i][:, P2:2 * P2], uw[i + 1][:, P2:2 * P2])
        rv_s[bi, :, ln] = jnp.where(first, rv[i], rv[i + 1])

    brow, bcol = _tri_masks(P2)
    blockdiag = (brow >> 6) == (bcol >> 6)
    for c in range(n_c):
        rows = slice(c * C, (c + 1) * C)
        win = slice((c // 2) * 2 * C, (c // 2 + 1) * 2 * C)
        lp = lambda p: slice(p * P2, (p + 1) * P2)
        ss = [state_ref[bi * n_p + p] for bi, p in pairs]
        rw = [_mm_bf(jnp.concatenate([r_s[bi, rows, lp(p)], w_s[bi, rows, lp(p)]], axis=0), s)
              for (bi, p), s in zip(pairs, ss)]
        pm = [-(x[C:2 * C] + u_s[bi, rows, lp(p)]) for (bi, p), x in zip(pairs, rw)]
        upd = [_mm_bf(jnp.concatenate([bend_s[bi, rows, lp(p)], kend_s[bi, rows, lp(p)]], axis=0),
                      jnp.concatenate([x, v_s[bi, rows, lp(p)]], axis=0), TN) for (bi, p), x in zip(pairs, pm)]
        for i, (bi, p) in enumerate(pairs):
            g_col = jnp.exp(lct_s[bi, p * P2:(p + 1) * P2, c * C + C - 1:c * C + C])
            state_ref[bi * n_p + p] = ss[i] * g_col + jnp.where(blockdiag, upd[i], 0.0)
            p2 = jnp.concatenate([pm[i], pm[i]], axis=0)
            h0 = bi * NH + 2 * p
            y_s[bi, rows, lp(p)] = rw[i][0:C] + rv_s[bi, rows, lp(p)] + jnp.where(
                first, _mm_bf(arb_s[h0, rows, win], p2), _mm_bf(arb_s[h0 + 1, rows, win], p2))

    inv_n = 1.0 / RWKV_HEAD_DIM
    for bi in range(nb):
        y = y_s[bi]
        mean = head_sums(y) * inv_n
        d = y - mean
        var = head_sums(d * d) * inv_n
        yn = d * lax.rsqrt(var + RWKV_GN_EPS) * lnw_ref[...] + lnb_ref[...]
        o_ref[bi] = (yn + bonus_s[bi]) * gate_s[bi]


def _rwkv(rw, mu, w0, w2p, a0, a2p, g2, k_k, k_a, r_k, ln_w, ln_b, bsz, seq, tt=256, nb=2):
    cols = rw.shape[1]
    W = RWKV_WIDTH
    tile = lambda b, j: (b, j, 0)
    const = lambda b, j: (0, 0)
    vec = pl.BlockSpec((1, W), const)
    out = pl.pallas_call(
        functools.partial(_rwkv_kernel, tt=tt, nb=nb),
        out_shape=jax.ShapeDtypeStruct((bsz, seq, W), F32),
        grid=(bsz // nb, seq // tt),
        in_specs=[
            pl.BlockSpec((nb, tt, cols), tile),
            pl.BlockSpec((1, cols), const),
            vec,
            pl.BlockSpec(w2p.shape, const),
            vec,
            pl.BlockSpec(a2p.shape, const),
            pl.BlockSpec(g2.shape, const),
            vec, vec, vec, vec, vec,
        ],
        out_specs=pl.BlockSpec((nb, tt, W), tile),
        scratch_shapes=[pltpu.VMEM((nb, HALO, cols), F32)]
        + [pltpu.VMEM((nb, tt, W), F32) for _ in range(13)]
        + [pltpu.VMEM((nb, W, tt), F32),
           pltpu.VMEM((nb * RWKV_HEADS, tt, tt), F32),
           pltpu.VMEM((nb * RWKV_HEADS // 2, 2 * RWKV_HEAD_DIM, 2 * RWKV_HEAD_DIM), F32)],
        compiler_params=pltpu.CompilerParams(
            dimension_semantics=("arbitrary", "arbitrary"), vmem_limit_bytes=VMEM_LIMIT),
        name="rwkv7",
    )(rw.reshape(bsz, seq, cols), mu, w0, w2p, a0, a2p, g2, k_k, k_a, r_k, ln_w, ln_b)
    return out.reshape(bsz * seq, W)


def _rglru_kernel(gb_ref, xb_ref, convw_ref, convb_ref, wa_ref, ba_ref, wx_ref, bx_ref, lam_ref, o_ref,
                  halo_ref, h_ref, *, tt):
    @pl.when(pl.program_id(1) == 0)
    def _():
        halo_ref[...] = jnp.zeros_like(halo_ref)
        h_ref[...] = jnp.zeros_like(h_ref)

    x = xb_ref[...]
    xp = jnp.concatenate([halo_ref[...], x], axis=0)
    cw = convw_ref[...]
    xc = x * cw[CONV_WIDTH - 1:CONV_WIDTH] + convb_ref[...]
    for j in range(CONV_WIDTH - 1):
        o = HALO - (CONV_WIDTH - 1) + j
        xc = xc + xp[o:o + tt] * cw[j:j + 1]
    halo_ref[...] = x[tt - HALO:tt]

    blk = x.shape[1] // LRU_BLOCKS
    xc_bf = xc.astype(BF16)
    ra = jnp.concatenate(
        [_mm(xc_bf[:, n * blk:(n + 1) * blk], wa_ref[n]) for n in range(LRU_BLOCKS)], axis=1)
    ix = jnp.concatenate(
        [_mm(xc_bf[:, n * blk:(n + 1) * blk], wx_ref[n]) for n in range(LRU_BLOCKS)], axis=1)
    r = _sigmoid(ra + ba_ref[...])
    i = _sigmoid(ix + bx_ref[...])
    log_a = -LRU_C * r * _softplus(-lam_ref[...])
    a = jnp.exp(log_a)
    u = xc * i * jnp.sqrt(1.0 - jnp.exp(2.0 * log_a))

    width = x.shape[1]
    a = a.reshape(tt // SUB, SUB, width)
    u = u.reshape(tt // SUB, SUB, width)
    sub = lax.broadcasted_iota(jnp.int32, (1, SUB, 1), 1)
    d = 1
    while d < SUB:
        keep = sub >= d
        a_sh = jnp.where(keep, pltpu.roll(a, d, axis=1), 1.0)
        u_sh = jnp.where(keep, pltpu.roll(u, d, axis=1), 0.0)
        u = a * u_sh + u
        a = a * a_sh
        d *= 2
    gate = _gelu_tanh(gb_ref[...])
    carry = h_ref[...]
    for g in range(tt // SUB):
        rows = slice(g * SUB, (g + 1) * SUB)
        h = u[g] + a[g] * carry
        o_ref[rows, :] = h * gate[rows]
        carry = jnp.broadcast_to(h[SUB - 1:SUB, :], h.shape)
    h_ref[...] = carry


def _rglru(gb, xb, conv_w, conv_b, wa_bf, ba, wx_bf, bx, lam, bsz, seq, tt=256):
    n_tok, width = xb.shape
    per_b = seq // tt
    row = lambda b, j: (b * per_b + j, 0)
    const = lambda b, j: (0, 0)
    const3 = lambda b, j: (0, 0, 0)
    vec = pl.BlockSpec((1, width), const)
    return pl.pallas_call(
        functools.partial(_rglru_kernel, tt=tt),
        out_shape=jax.ShapeDtypeStruct((n_tok, width), F32),
        grid=(bsz, per_b),
        in_specs=[
            pl.BlockSpec((tt, width), row),
            pl.BlockSpec((tt, width), row),
            pl.BlockSpec(conv_w.shape, const),
            vec,
            pl.BlockSpec(wa_bf.shape, const3),
            vec,
            pl.BlockSpec(wx_bf.shape, const3),
            vec,
            vec,
        ],
        out_specs=pl.BlockSpec((tt, width), row),
        scratch_shapes=[pltpu.VMEM((HALO, width), F32), pltpu.VMEM((HALO, width), F32)],
        compiler_params=pltpu.CompilerParams(
            dimension_semantics=("arbitrary", "arbitrary"), vmem_limit_bytes=VMEM_LIMIT),
        name="rglru",
    )(gb, xb, conv_w, conv_b, wa_bf, ba, wx_bf, bx, lam)


def _pad_lanes(v):
    return jnp.pad(v, (0, LANES - v.shape[0])).reshape(1, LANES)


def kernel(x, c, norm_pre, norm_post, ada_w, ada_b, ffn_w_gate, ffn_w_up, ffn_w_down, mix_w_in, mix_w_out, gdn_conv_w, gdn_a_log, gdn_dt_bias, gdn_norm_w, rwkv_mu, rwkv_w0, rwkv_w2, rwkv_a0, rwkv_a2, rwkv_g2, rwkv_k_k, rwkv_k_a, rwkv_r_k, rwkv_ln_w, rwkv_ln_b, lru_w_in, lru_conv_w, lru_conv_b, lru_wa, lru_ba, lru_wx, lru_bx, lru_lambda, lru_w_out):
    bsz, seq, d = x.shape
    depth = norm_pre.shape[0]
    x2 = x.reshape(bsz * seq, d)
    mods = _ada_params(c, ada_w, ada_b)

    def mod(layer, sub):
        m = mods[layer * 2 + sub]
        return (m[:, None, 0:d], m[:, None, d:2 * d], m[:, None, 2 * d:3 * d])

    GW = GDN_WIDTH
    for layer in range(depth):
        j = layer // 2
        shift, scale, gate = mod(layer, 0)
        shift2, scale2, gate2 = mod(layer, 1)
        mod_rows = jnp.concatenate([gate, shift2, scale2, gate2], axis=1)
        nw_rows = jnp.stack([norm_post[layer, 0], norm_pre[layer, 1], norm_post[layer, 1]])
        nw_pre = norm_pre[layer, 0].reshape(1, d)
        ffn_w = (ffn_w_gate[layer].astype(BF16), ffn_w_up[layer].astype(BF16), ffn_w_down[layer].astype(BF16))
        if layer % 2 == 0:
            w_in = mix_w_in[j]
            n_gdn = 4 * GW + 2 * GDN_HEADS
            w_cat = jnp.concatenate(
                [w_in[:, 0:4 * GW], w_in[:, n_gdn:],
                 jnp.pad(w_in[:, 4 * GW:n_gdn], ((0, 0), (0, LANES - 2 * GDN_HEADS)))], axis=1).astype(BF16)
            qkvz, rw, ab = _norm_proj(x2, nw_pre, shift, scale, w_cat, (4 * GW, RWKV_COLS, LANES), seq)
            out_a = _gdn(qkvz, ab, gdn_conv_w[j], _pad_lanes(gdn_a_log[j]), _pad_lanes(gdn_dt_bias[j]),
                         gdn_norm_w[j].reshape(1, GDN_HEAD_DIM), bsz, seq)
            w2p = jnp.pad(rwkv_w2[j], ((0, LANES - DECAY_LORA), (0, 0))).astype(BF16)
            a2p = jnp.pad(rwkv_a2[j], ((DECAY_LORA, LANES - DECAY_LORA - AAA_LORA), (0, 0))).astype(BF16)
            vec = lambda t: t.reshape(1, RWKV_WIDTH)
            out_b = _rwkv(rw, rwkv_mu[j].reshape(1, RWKV_COLS), vec(rwkv_w0[j]), w2p, vec(rwkv_a0[j]), a2p,
                          rwkv_g2[j].astype(BF16), vec(rwkv_k_k[j]), vec(rwkv_k_a[j]), vec(rwkv_r_k[j]),
                          vec(rwkv_ln_w[j]), vec(rwkv_ln_b[j]), bsz, seq)
            w_out = mix_w_out[j].astype(BF16)
            x2 = _out_ffn(x2, [out_a, out_b], [w_out[0:GW], w_out[GW:]], mod_rows, nw_rows, *ffn_w, seq)
        else:
            width = lru_w_in.shape[2] // 2
            gb, xb = _norm_proj(x2, nw_pre, shift, scale, lru_w_in[j].astype(BF16), (width, width), seq)
            vec = lambda t: t.reshape(1, width)
            y = _rglru(gb, xb, lru_conv_w[j], vec(lru_conv_b[j]), lru_wa[j].astype(BF16), vec(lru_ba[j]),
                       lru_wx[j].astype(BF16), vec(lru_bx[j]), vec(lru_lambda[j]), bsz, seq)
            x2 = _out_ffn(x2, [y], [lru_w_out[j].astype(BF16)], mod_rows, nw_rows, *ffn_w, seq)
    return x2.reshape(bsz, seq, d)
```

```python
import functools

import jax
import jax.numpy as jnp
from jax import lax
from jax.experimental import pallas as pl
from jax.experimental.pallas import tpu as pltpu

F32 = jnp.float32
BF16 = jnp.bfloat16

NORM_EPS = 1e-6
GDN_HEADS = 4
GDN_HEAD_DIM = 128
GDN_WIDTH = GDN_HEADS * GDN_HEAD_DIM
CHUNK = 64
CONV_WIDTH = 4
RWKV_HEADS = 8
RWKV_HEAD_DIM = 64
RWKV_WIDTH = RWKV_HEADS * RWKV_HEAD_DIM
DECAY_LORA = 64
AAA_LORA = 64
GATE_LORA = 128
RWKV_COLS = 3 * RWKV_WIDTH + DECAY_LORA + AAA_LORA + GATE_LORA
RWKV_GN_EPS = 64e-5
LRU_BLOCKS = 4
LRU_C = 8.0
LANES = 128
MXU_DIM = 256
SUB = 8
HALO = SUB
VMEM_LIMIT = 56 * 1024 * 1024

NN = (((1,), (0,)), ((), ()))
NT = (((1,), (1,)), ((), ()))
TN = (((0,), (0,)), ((), ()))


def _mm(a, b, dims=NN):
    return lax.dot_general(a, b, dims, preferred_element_type=F32)


def _mm_bf(a, b, dims=NN):
    return _mm(a.astype(BF16), b.astype(BF16), dims)


def _split2(x):
    hi = x.astype(BF16)
    lo = (x - hi.astype(F32)).astype(BF16)
    return hi, lo


def _mm_x3(a, b, dims=NN):
    ah, al = _split2(a)
    bh, bl = _split2(b)
    return _mm(ah, bh, dims) + (_mm(ah, bl, dims) + _mm(al, bh, dims))


def _mm_const_lhs(c_bf, x):
    hi = x.astype(BF16)
    r1 = x - hi.astype(F32)
    mid = r1.astype(BF16)
    lo = (r1 - mid.astype(F32)).astype(BF16)
    return _mm(c_bf, hi) + (_mm(c_bf, mid) + _mm(c_bf, lo))


def _mm_const_rhs(x, c_bf):
    hi, lo = _split2(x)
    return _mm(hi, c_bf) + _mm(lo, c_bf)


def _sigmoid(x):
    return 1.0 / (1.0 + jnp.exp(-x))


def _silu(x):
    return x * _sigmoid(x)


def _softplus(x):
    return jnp.maximum(x, 0.0) + jnp.log1p(jnp.exp(-jnp.abs(x)))


def _gelu_tanh(x):
    return 0.5 * x * (1.0 + jnp.tanh(0.7978845608028654 * (x + 0.044715 * (x * x * x))))


def _rms(x):
    return x * lax.rsqrt(jnp.mean(x * x, axis=-1, keepdims=True) + NORM_EPS)


def _tri_masks(n):
    row = lax.broadcasted_iota(jnp.int32, (n, n), 0)
    col = lax.broadcasted_iota(jnp.int32, (n, n), 1)
    return row, col


def _pack_blocks(x):
    out = x[0:CHUNK]
    for c in range(1, x.shape[0] // CHUNK):
        out = out + x[c * CHUNK:(c + 1) * CHUNK]
    return out


def _expand_blocks(xp_bf, mask_bf):
    return jnp.concatenate([xp_bf] * (xp_bf.shape[1] // CHUNK), axis=0) * mask_bf


def _unit_lower_inverses(ms_p, mask_bf):
    n = ms_p[0].shape[1]
    prow = lax.broadcasted_iota(jnp.int32, (CHUNK, n), 0)
    pcol = lax.broadcasted_iota(jnp.int32, (CHUNK, n), 1) & (CHUNK - 1)
    eye_p = (prow == pcol).astype(F32)
    expand = lambda xp: _expand_blocks(xp.astype(BF16), mask_bf)
    mm = lambda xp, y_bf: _mm(xp.astype(BF16), y_bf)
    same = (prow >> 3) == (pcol >> 3)
    m8_p = [jnp.where(same, mp, 0.0) for mp in ms_p]
    m2_p = [mm(ap, expand(ap)) for ap in m8_p]
    m4_p = [mm(ap, expand(ap)) for ap in m2_p]
    inv_p = [mm(eye_p - ap, expand(eye_p + bp)) for ap, bp in zip(m8_p, m2_p)]
    inv_p = [mm(ap, expand(eye_p + bp)) for ap, bp in zip(inv_p, m4_p)]
    shift = 3
    while (1 << shift) < CHUNK:
        pair = (prow >> (shift + 1)) == (pcol >> (shift + 1))
        off_mask = jnp.logical_and(pair, jnp.logical_not(same))
        tmp_p = [mm(ap, expand(jnp.where(off_mask, mp, 0.0))) for ap, mp in zip(inv_p, ms_p)]
        inv_p = [ap - mm(tp, expand(ap)) for ap, tp in zip(inv_p, tmp_p)]
        same = pair
        shift += 1
    a_split = [_split2(ap) for ap in inv_p]
    m_split = [_split2(mp) for mp in ms_p]
    e_hi = [_expand_blocks(a_hi, mask_bf) for a_hi, _ in a_split]
    e_lo = [_expand_blocks(a_lo, mask_bf) for _, a_lo in a_split]
    prod = [_mm(m_hi, eh) + (_mm(m_hi, el) + _mm(m_lo, eh)) for (m_hi, m_lo), eh, el in zip(m_split, e_hi, e_lo)]
    res = [expand(eye_p - (ap + pr)) for ap, pr in zip(inv_p, prod)]
    return [ap + _mm(a_hi, r) for ap, (a_hi, _), r in zip(inv_p, a_split, res)]


def _ada_kernel(c_ref, w_ref, b_ref, o_ref):
    s = _silu(c_ref[...])
    o_ref[0] = _mm_x3(s, w_ref[0]) + b_ref[0]


def _ada_params(c, ada_w, ada_b):
    n_l, n_s, d, d3 = ada_w.shape
    n = n_l * n_s
    bsz = c.shape[0]
    tn = 1024
    return pl.pallas_call(
        _ada_kernel,
        out_shape=jax.ShapeDtypeStruct((n, bsz, d3), F32),
        grid=(n, d3 // tn),
        in_specs=[
            pl.BlockSpec((bsz, d), lambda i, j: (0, 0)),
            pl.BlockSpec((1, d, tn), lambda i, j: (i, 0, j)),
            pl.BlockSpec((1, 1, tn), lambda i, j: (i, 0, j)),
        ],
        out_specs=pl.BlockSpec((1, bsz, tn), lambda i, j: (i, 0, j)),
        compiler_params=pltpu.CompilerParams(vmem_limit_bytes=VMEM_LIMIT),
        name="ada_params",
    )(c, ada_w.reshape(n, d, d3), ada_b.reshape(n, 1, d3))


def _norm_proj_kernel(x_ref, nw_ref, shift_ref, scale_ref, w_ref, *out_refs):
    h = _rms(x_ref[...]) * nw_ref[...]
    h = (h * (1.0 + scale_ref[0]) + shift_ref[0]).astype(BF16)
    off = 0
    for o_ref in out_refs:
        n = o_ref.shape[1]
        o_ref[...] = _mm(h, w_ref[:, off:off + n])
        off += n


def _norm_proj(x2, nw, shift, scale, w_bf, splits, seq, tm=256):
    n_tok, d = x2.shape
    per_b = seq // tm
    return pl.pallas_call(
        _norm_proj_kernel,
        out_shape=[jax.ShapeDtypeStruct((n_tok, n), F32) for n in splits],
        grid=(n_tok // tm,),
        in_specs=[
            pl.BlockSpec((tm, d), lambda i: (i, 0)),
            pl.BlockSpec((1, d), lambda i: (0, 0)),
            pl.BlockSpec((1, 1, d), lambda i: (i // per_b, 0, 0)),
            pl.BlockSpec((1, 1, d), lambda i: (i // per_b, 0, 0)),
            pl.BlockSpec(w_bf.shape, lambda i: (0, 0), pipeline_mode=pl.Buffered(1)),
        ],
        out_specs=[pl.BlockSpec((tm, n), lambda i: (i, 0)) for n in splits],
        compiler_params=pltpu.CompilerParams(vmem_limit_bytes=VMEM_LIMIT),
        name="norm_proj",
    )(x2, nw, shift, scale, w_bf)


def _out_ffn_kernel(*refs, n_in):
    x_ref = refs[0]
    a_refs = refs[1:1 + n_in]
    w_refs = refs[1 + n_in:1 + 2 * n_in]
    mod_ref, nw_ref, wg_ref, wu_ref, wd_ref, o_ref = refs[1 + 2 * n_in:]
    mod = mod_ref[0]
    nw = nw_ref[...]
    tm = x_ref.shape[0]
    halves = [slice(0, tm // 2), slice(tm // 2, tm)]
    ys = []
    for rows in halves:
        y = _mm(a_refs[0][rows, :].astype(BF16), w_refs[0][...])
        for a_ref, w_ref in zip(a_refs[1:], w_refs[1:]):
            y = y + _mm(a_ref[rows, :].astype(BF16), w_ref[...])
        ys.append(y)
    xs = [x_ref[rows, :] + mod[0:1] * (_rms(y) * nw[0:1]) for rows, y in zip(halves, ys)]
    hs = [((_rms(x) * nw[1:2]) * (1.0 + mod[2:3]) + mod[1:2]).astype(BF16) for x in xs]
    gs = [_mm(h, wg_ref[...]) for h in hs]
    us = [_mm(h, wu_ref[...]) for h in hs]
    acts = [(_silu(g) * u).astype(BF16) for g, u in zip(gs, us)]
    ys = [_mm(act, wd_ref[...]) for act in acts]
    for rows, x, y in zip(halves, xs, ys):
        o_ref[rows, :] = x + mod[3:4] * (_rms(y) * nw[2:3])


def _out_ffn(x2, acts, ws_bf, mod, nw, wg_bf, wu_bf, wd_bf, seq, tm=512):
    n_tok, d = x2.shape
    per_b = seq // tm
    row = lambda i: (i, 0)
    const = lambda i: (0, 0)
    resident = lambda w: pl.BlockSpec(w.shape, const, pipeline_mode=pl.Buffered(1))
    in_specs = [pl.BlockSpec((tm, d), row)]
    in_specs += [pl.BlockSpec((tm, a.shape[1]), row) for a in acts]
    in_specs += [resident(w) for w in ws_bf]
    in_specs += [pl.BlockSpec((1,) + mod.shape[1:], lambda i: (i // per_b, 0, 0)),
                 pl.BlockSpec(nw.shape, const), resident(wg_bf), resident(wu_bf), resident(wd_bf)]
    return pl.pallas_call(
        functools.partial(_out_ffn_kernel, n_in=len(acts)),
        out_shape=jax.ShapeDtypeStruct((n_tok, d), F32),
        grid=(n_tok // tm,),
        in_specs=in_specs,
        out_specs=pl.BlockSpec((tm, d), row),
        compiler_params=pltpu.CompilerParams(vmem_limit_bytes=VMEM_LIMIT),
        name="out_ffn",
    )(x2, *acts, *ws_bf, mod, nw, wg_bf, wu_bf, wd_bf)


def _gdn_kernel(qkvz_ref, ab_ref, convw_ref, alog_ref, dtb_ref, nw_ref, o_ref,
                halo_ref, qkv_s, u_s, w_s, qd_s, kd_s, attn_s, o_s, state_ref, *, tt):
    W = GDN_WIDTH
    D = GDN_HEAD_DIM
    C = CHUNK
    H = GDN_HEADS
    n_c = tt // C

    @pl.when(pl.program_id(1) == 0)
    def _():
        halo_ref[...] = jnp.zeros_like(halo_ref)
        state_ref[...] = jnp.zeros_like(state_ref)

    x = qkvz_ref[:, 0:3 * W]
    xp = jnp.concatenate([halo_ref[...], x], axis=0)
    cw = convw_ref[...]
    y = x * cw[CONV_WIDTH - 1:CONV_WIDTH]
    for j in range(CONV_WIDTH - 1):
        o = HALO - (CONV_WIDTH - 1) + j
        y = y + xp[o:o + tt] * cw[j:j + 1]
    halo_ref[...] = x[tt - HALO:tt]
    y = _silu(y)
    for h in range(H):
        q = y[:, h * D:(h + 1) * D]
        k = y[:, W + h * D:W + (h + 1) * D]
        q = q * lax.rsqrt(jnp.sum(q * q, axis=-1, keepdims=True) + NORM_EPS) * (D ** -0.5)
        k = k * lax.rsqrt(jnp.sum(k * k, axis=-1, keepdims=True) + NORM_EPS)
        qkv_s[:, h * D:(h + 1) * D] = q
        qkv_s[:, W + h * D:W + (h + 1) * D] = k
    qkv_s[:, 2 * W:3 * W] = y[:, 2 * W:3 * W]

    ab = ab_ref[...]
    g = -jnp.exp(alog_ref[...]) * _softplus(ab + dtb_ref[...])
    beta_all = _sigmoid(ab)

    row, col = _tri_masks(tt)
    same_chunk = (row >> 6) == (col >> 6)
    causal = jnp.logical_and(same_chunk, row >= col)
    strict = jnp.logical_and(same_chunk, row > col)
    gc = _mm_const_lhs(causal.astype(BF16), g)
    gc_t = gc.T
    g_last = jnp.concatenate(
        [jnp.broadcast_to(gc[c * C + C - 1:c * C + C, :], (C, LANES)) for c in range(n_c)], axis=0)
    e_g = jnp.exp(gc)
    e_rem = jnp.exp(g_last - gc)
    qs = [qkv_s[:, h * D:(h + 1) * D] for h in range(H)]
    ks = [qkv_s[:, W + h * D:W + (h + 1) * D] for h in range(H)]
    vs = [qkv_s[:, 2 * W + h * D:2 * W + (h + 1) * D] for h in range(H)]
    betas = [beta_all[:, H + h:H + h + 1] for h in range(H)]
    k_betas = [k * b for k, b in zip(ks, betas)]
    decays = [jnp.where(causal, jnp.exp(jnp.minimum(gc[:, h:h + 1] - gc_t[h:h + 1, :], 0.0)), 0.0)
              for h in range(H)]
    ms = [jnp.where(strict, _mm_bf(kb, k, NT) * dc, 0.0) for kb, k, dc in zip(k_betas, ks, decays)]
    mask_bf = same_chunk.astype(BF16)
    t_ps = _unit_lower_inverses([_pack_blocks(m) for m in ms], mask_bf)
    for h in range(H):
        rhs = jnp.concatenate([vs[h] * betas[h], k_betas[h] * e_g[:, h:h + 1]], axis=1)
        t_hi, t_lo = _split2(t_ps[h])
        rhs_hi, rhs_lo = _split2(rhs)
        t_hi = _expand_blocks(t_hi, mask_bf)
        uw = _mm(t_hi, rhs_hi) + (_mm(t_hi, rhs_lo) + _mm(_expand_blocks(t_lo, mask_bf), rhs_hi))
        u_s[h] = uw[:, 0:D]
        w_s[h] = uw[:, D:2 * D]
        attn_s[h] = _mm_bf(qs[h], ks[h], NT) * decays[h]
        qd_s[h] = qs[h] * e_g[:, h:h + 1]
        kd_s[h] = ks[h] * e_rem[:, h:h + 1]

    for c in range(n_c):
        rows = slice(c * C, (c + 1) * C)
        win = slice((c // 2) * 2 * C, (c // 2 + 1) * 2 * C)
        ss = [state_ref[h] for h in range(H)]
        wq = [_mm_bf(jnp.concatenate([w_s[h, rows, :], qd_s[h, rows, :]], axis=0), ss[h]) for h in range(H)]
        v_new = [u_s[h, rows, :] - wq[h][0:C] for h in range(H)]
        kv = [_mm_bf(kd_s[h, rows, :], v_new[h], TN) for h in range(H)]
        for h in range(H):
            gl = jnp.exp(gc[c * C + C - 1:c * C + C, h:h + 1])
            state_ref[h] = ss[h] * gl + kv[h]
            vv = jnp.concatenate([v_new[h], v_new[h]], axis=0)
            o_s[rows, h * D:(h + 1) * D] = wq[h][C:2 * C] + _mm_bf(attn_s[h, rows, win], vv)

    nw = nw_ref[...]
    for h in range(H):
        z = qkvz_ref[:, 3 * W + h * D:3 * W + (h + 1) * D]
        o_ref[:, h * D:(h + 1) * D] = _rms(o_s[:, h * D:(h + 1) * D]) * nw * _silu(z)


def _gdn(qkvz, ab, conv_w, a_log_row, dt_row, norm_w, bsz, seq, tt=256):
    n_tok = qkvz.shape[0]
    W = GDN_WIDTH
    per_b = seq // tt
    row = lambda b, j: (b * per_b + j, 0)
    const = lambda b, j: (0, 0)
    return pl.pallas_call(
        functools.partial(_gdn_kernel, tt=tt),
        out_shape=jax.ShapeDtypeStruct((n_tok, W), F32),
        grid=(bsz, per_b),
        in_specs=[
            pl.BlockSpec((tt, 4 * W), row),
            pl.BlockSpec((tt, LANES), row),
            pl.BlockSpec(conv_w.shape, const),
            pl.BlockSpec((1, LANES), const),
            pl.BlockSpec((1, LANES), const),
            pl.BlockSpec((1, GDN_HEAD_DIM), const),
        ],
        out_specs=pl.BlockSpec((tt, W), row),
        scratch_shapes=[
            pltpu.VMEM((HALO, 3 * W), F32),
            pltpu.VMEM((tt, 3 * W), F32),
            pltpu.VMEM((GDN_HEADS, tt, GDN_HEAD_DIM), F32),
            pltpu.VMEM((GDN_HEADS, tt, GDN_HEAD_DIM), F32),
            pltpu.VMEM((GDN_HEADS, tt, GDN_HEAD_DIM), F32),
            pltpu.VMEM((GDN_HEADS, tt, GDN_HEAD_DIM), F32),
            pltpu.VMEM((GDN_HEADS, tt, tt), F32),
            pltpu.VMEM((tt, W), F32),
            pltpu.VMEM((GDN_HEADS, GDN_HEAD_DIM, GDN_HEAD_DIM), F32),
        ],
        compiler_params=pltpu.CompilerParams(
            dimension_semantics=("arbitrary", "arbitrary"), vmem_limit_bytes=VMEM_LIMIT),
        name="gdn",
    )(qkvz, ab, conv_w, a_log_row, dt_row, norm_w)


def _rwkv_kernel(rw_ref, mu_ref, w0_ref, w2_ref, a0_ref, a2_ref, g2_ref, kk_ref, ka_ref, rk_ref,
                 lnw_ref, lnb_ref, o_ref,
                 carry_ref, r_s, k_s, b_s, kk_s, kend_s, bend_s, v_s, y_s, u_s, w_s, rv_s, bonus_s, gate_s,
                 lct_s, arb_s, state_ref, *, tt, nb):
    W = RWKV_WIDTH
    C = CHUNK
    NH = RWKV_HEADS
    P2 = 2 * RWKV_HEAD_DIM
    n_c = tt // C
    n_p = NH // 2

    @pl.when(pl.program_id(1) == 0)
    def _():
        carry_ref[...] = jnp.zeros_like(carry_ref)
        state_ref[...] = jnp.zeros_like(state_ref)

    hrow, hcol = _tri_masks(MXU_DIM)
    head_ones = ((hrow >> 6) == (hcol >> 6)).astype(BF16)

    def head_sums(x):
        return jnp.concatenate(
            [_mm_const_rhs(x[:, o:o + MXU_DIM], head_ones) for o in range(0, W, MXU_DIM)], axis=1)

    row, col = _tri_masks(tt)
    same_chunk = (row >> 6) == (col >> 6)
    causal = jnp.logical_and(same_chunk, row >= col)
    strict = jnp.logical_and(same_chunk, row > col)
    tril_bf = causal.astype(BF16)
    trow = lax.broadcasted_iota(jnp.int32, (tt, 1), 0)

    for bi in range(nb):
        cf = rw_ref[bi]
        prev = jnp.where(trow == 0, carry_ref[bi, HALO - 1:HALO, :], pltpu.roll(cf, 1, axis=0))
        carry_ref[bi] = cf[tt - HALO:tt]
        cf = cf + mu_ref[...] * (prev - cf)

        r = cf[:, 0:W]
        k = cf[:, W:2 * W]
        v = cf[:, 2 * W:3 * W]
        wd_ad = cf[:, 3 * W:3 * W + LANES]
        gd = cf[:, 3 * W + LANES:3 * W + 2 * LANES]

        w_log = -_softplus(-(w0_ref[...] + _mm_bf(jnp.tanh(wd_ad), w2_ref[...]))) - 0.5
        a = _sigmoid(a0_ref[...] + _mm_bf(wd_ad, a2_ref[...]))
        gate_s[bi] = _mm_bf(_sigmoid(gd), g2_ref[...])
        kkv = k * kk_ref[...]
        kk = kkv * lax.rsqrt(head_sums(kkv * kkv) + NORM_EPS)
        kmod = k * (1.0 + (a - 1.0) * ka_ref[...])
        bonus_s[bi] = head_sums(r * kmod * rk_ref[...]) * v
        lw = -jnp.exp(w_log)

        lc = _mm_const_lhs(tril_bf, lw)
        lct_s[bi] = lc.T
        l_last = jnp.concatenate(
            [jnp.broadcast_to(lc[c * C + C - 1:c * C + C, :], (C, W)) for c in range(n_c)], axis=0)
        e_inv = jnp.exp(-lc)
        e_rem = jnp.exp(l_last - lc)
        b = kk * a
        r_s[bi] = r * jnp.exp(lc)
        k_s[bi] = kmod * e_inv
        b_s[bi] = b * e_inv
        kk_s[bi] = kk * jnp.exp(lc - lw)
        kend_s[bi] = kmod * e_rem
        bend_s[bi] = b * e_rem
        v_s[bi] = v

    lane = lax.broadcasted_iota(jnp.int32, (1, P2), 1)
    first = lane < RWKV_HEAD_DIM
    prow = lax.broadcasted_iota(jnp.int32, (C, 2 * P2), 0)
    pcol = lax.broadcasted_iota(jnp.int32, (C, 2 * P2), 1) & (C - 1)
    strict_p = prow > pcol
    causal_p = prow >= pcol

    def per_head_rows(x):
        return jnp.concatenate([jnp.where(first, x, 0.0), jnp.where(first, 0.0, x)], axis=0)

    units = [(bi, p, c) for bi in range(nb) for p in range(n_p) for c in range(n_c)]
    rows_of = lambda c: slice(c * C, (c + 1) * C)
    lanes_of = lambda p: slice(p * P2, (p + 1) * P2)
    cut = lambda ref, u: ref[u[0], rows_of(u[2]), lanes_of(u[1])]

    prod = [_mm_bf(jnp.concatenate([cut(kk_s, u), cut(r_s, u)], axis=0),
                   jnp.concatenate([per_head_rows(cut(b_s, u)), per_head_rows(cut(k_s, u))], axis=0), NT)
            for u in units]
    kbkk = [jnp.where(strict_p, x[0:C], 0.0) for x in prod]
    rbrk = [jnp.where(causal_p, x[C:2 * C], 0.0) for x in prod]
    for i in range(len(units)):
        arb_s[i] = rbrk[i][:, 0:P2]
    avrv = [_mm_bf(jnp.concatenate([x[:, P2:2 * P2], y[:, P2:2 * P2]], axis=0), per_head_rows(cut(v_s, u)))
            for x, y, u in zip(kbkk, rbrk, units)]
    for u, x in zip(units, avrv):
        rv_s[u[0], rows_of(u[2]), lanes_of(u[1])] = x[C:2 * C]
    prow2, pcol2 = _tri_masks(2 * P2)
    t_p = _unit_lower_inverses(
        [jnp.concatenate([kbkk[i][:, 0:P2], kbkk[i + 1][:, 0:P2]], axis=1) for i in range(0, len(units), 2)],
        ((prow2 >> 6) == (pcol2 >> 6)).astype(BF16))
    first2 = jnp.concatenate([first, first], axis=1)
    for i, u in enumerate(units):
        x = jnp.concatenate([avrv[i][0:C], cut(kk_s, u)], axis=1)
        x = jnp.concatenate([jnp.where(first2, x, 0.0), jnp.where(first2, 0.0, x)], axis=0)
        uw = _mm_x3(t_p[i // 2][:, (i % 2) * P2:(i % 2 + 1) * P2], x)
        u_s[u[0], rows_of(u[2]), lanes_of(u[1])] = uw[:, 0:P2]
        w_s[u[0], rows_of(u[2]), lanes_of(u[1])] = uw[:, P2:2 * P2]

    brow, bcol = _tri_masks(P2)
    blockdiag = (brow >> 6) == (bcol >> 6)
    pairs = [(bi, p) for bi in range(nb) for p in range(n_p)]
    for c in range(n_c):
        us = [(bi, p, c) for bi, p in pairs]
        ss = [state_ref[bi * n_p + p] for bi, p in pairs]
        rw = [_mm_bf(jnp.concatenate([cut(r_s, u), cut(w_s, u)], axis=0), s) for u, s in zip(us, ss)]
        pm = [-(x[C:2 * C] + cut(u_s, u)) for u, x in zip(us, rw)]
        upd = [_mm_bf(jnp.concatenate([cut(bend_s, u), cut(kend_s, u)], axis=0),
                      jnp.concatenate([x, cut(v_s, u)], axis=0), TN) for u, x in zip(us, pm)]
        for i, (bi, p) in enumerate(pairs):
            g_col = jnp.exp(lct_s[bi, p * P2:(p + 1) * P2, c * C + C - 1:c * C + C])
            state_ref[bi * n_p + p] = ss[i] * g_col + jnp.where(blockdiag, upd[i], 0.0)
            y_s[bi, rows_of(c), lanes_of(p)] = rw[i][0:C] + cut(rv_s, us[i]) + _mm_bf(
                arb_s[(bi * n_p + p) * n_c + c], per_head_rows(pm[i]))

    inv_n = 1.0 / RWKV_HEAD_DIM
    for bi in range(nb):
        y = y_s[bi]
        mean = head_sums(y) * inv_n
        d = y - mean
        var = head_sums(d * d) * inv_n
        yn = d * lax.rsqrt(var + RWKV_GN_EPS) * lnw_ref[...] + lnb_ref[...]
        o_ref[bi] = (yn + bonus_s[bi]) * gate_s[bi]


def _rwkv(rw, mu, w0, w2p, a0, a2p, g2, k_k, k_a, r_k, ln_w, ln_b, bsz, seq, tt=256, nb=2):
    cols = rw.shape[1]
    W = RWKV_WIDTH
    tile = lambda b, j: (b, j, 0)
    const = lambda b, j: (0, 0)
    vec = pl.BlockSpec((1, W), const)
    out = pl.pallas_call(
        functools.partial(_rwkv_kernel, tt=tt, nb=nb),
        out_shape=jax.ShapeDtypeStruct((bsz, seq, W), F32),
        grid=(bsz // nb, seq // tt),
        in_specs=[
            pl.BlockSpec((nb, tt, cols), tile),
            pl.BlockSpec((1, cols), const),
            vec,
            pl.BlockSpec(w2p.shape, const),
            vec,
            pl.BlockSpec(a2p.shape, const),
            pl.BlockSpec(g2.shape, const),
            vec, vec, vec, vec, vec,
        ],
        out_specs=pl.BlockSpec((nb, tt, W), tile),
        scratch_shapes=[pltpu.VMEM((nb, HALO, cols), F32)]
        + [pltpu.VMEM((nb, tt, W), F32) for _ in range(13)]
        + [pltpu.VMEM((nb, W, tt), F32),
           pltpu.VMEM((nb * (RWKV_HEADS // 2) * (tt // CHUNK), CHUNK, 2 * RWKV_HEAD_DIM), F32),
           pltpu.VMEM((nb * RWKV_HEADS // 2, 2 * RWKV_HEAD_DIM, 2 * RWKV_HEAD_DIM), F32)],
        compiler_params=pltpu.CompilerParams(
            dimension_semantics=("arbitrary", "arbitrary"), vmem_limit_bytes=VMEM_LIMIT),
        name="rwkv7",
    )(rw.reshape(bsz, seq, cols), mu, w0, w2p, a0, a2p, g2, k_k, k_a, r_k, ln_w, ln_b)
    return out.reshape(bsz * seq, W)


def _rglru_kernel(gb_ref, xb_ref, convw_ref, convb_ref, wa_ref, ba_ref, wx_ref, bx_ref, lam_ref, o_ref,
                  halo_ref, h_ref, *, tt):
    @pl.when(pl.program_id(1) == 0)
    def _():
        halo_ref[...] = jnp.zeros_like(halo_ref)
        h_ref[...] = jnp.zeros_like(h_ref)

    x = xb_ref[...]
    xp = jnp.concatenate([halo_ref[...], x], axis=0)
    cw = convw_ref[...]
    xc = x * cw[CONV_WIDTH - 1:CONV_WIDTH] + convb_ref[...]
    for j in range(CONV_WIDTH - 1):
        o = HALO - (CONV_WIDTH - 1) + j
        xc = xc + xp[o:o + tt] * cw[j:j + 1]
    halo_ref[...] = x[tt - HALO:tt]

    blk = x.shape[1] // LRU_BLOCKS
    xc_bf = xc.astype(BF16)
    ra = jnp.concatenate(
        [_mm(xc_bf[:, n * blk:(n + 1) * blk], wa_ref[n]) for n in range(LRU_BLOCKS)], axis=1)
    ix = jnp.concatenate(
        [_mm(xc_bf[:, n * blk:(n + 1) * blk], wx_ref[n]) for n in range(LRU_BLOCKS)], axis=1)
    r = _sigmoid(ra + ba_ref[...])
    i = _sigmoid(ix + bx_ref[...])
    log_a = -LRU_C * r * _softplus(-lam_ref[...])
    a = jnp.exp(log_a)
    u = xc * i * jnp.sqrt(1.0 - jnp.exp(2.0 * log_a))

    width = x.shape[1]
    a = a.reshape(tt // SUB, SUB, width)
    u = u.reshape(tt // SUB, SUB, width)
    sub = lax.broadcasted_iota(jnp.int32, (1, SUB, 1), 1)
    d = 1
    while d < SUB:
        keep = sub >= d
        a_sh = jnp.where(keep, pltpu.roll(a, d, axis=1), 1.0)
        u_sh = jnp.where(keep, pltpu.roll(u, d, axis=1), 0.0)
        u = a * u_sh + u
        a = a * a_sh
        d *= 2
    gate = _gelu_tanh(gb_ref[...])
    carry = h_ref[...]
    for g in range(tt // SUB):
        rows = slice(g * SUB, (g + 1) * SUB)
        h = u[g] + a[g] * carry
        o_ref[rows, :] = h * gate[rows]
        carry = jnp.broadcast_to(h[SUB - 1:SUB, :], h.shape)
    h_ref[...] = carry


def _rglru(gb, xb, conv_w, conv_b, wa_bf, ba, wx_bf, bx, lam, bsz, seq, tt=256):
    n_tok, width = xb.shape
    per_b = seq // tt
    row = lambda b, j: (b * per_b + j, 0)
    const = lambda b, j: (0, 0)
    const3 = lambda b, j: (0, 0, 0)
    vec = pl.BlockSpec((1, width), const)
    return pl.pallas_call(
        functools.partial(_rglru_kernel, tt=tt),
        out_shape=jax.ShapeDtypeStruct((n_tok, width), F32),
        grid=(bsz, per_b),
        in_specs=[
            pl.BlockSpec((tt, width), row),
            pl.BlockSpec((tt, width), row),
            pl.BlockSpec(conv_w.shape, const),
            vec,
            pl.BlockSpec(wa_bf.shape, const3),
            vec,
            pl.BlockSpec(wx_bf.shape, const3),
            vec,
            vec,
        ],
        out_specs=pl.BlockSpec((tt, width), row),
        scratch_shapes=[pltpu.VMEM((HALO, width), F32), pltpu.VMEM((HALO, width), F32)],
        compiler_params=pltpu.CompilerParams(
            dimension_semantics=("arbitrary", "arbitrary"), vmem_limit_bytes=VMEM_LIMIT),
        name="rglru",
    )(gb, xb, conv_w, conv_b, wa_bf, ba, wx_bf, bx, lam)


def _pad_lanes(v):
    return jnp.pad(v, (0, LANES - v.shape[0])).reshape(1, LANES)


def kernel(x, c, norm_pre, norm_post, ada_w, ada_b, ffn_w_gate, ffn_w_up, ffn_w_down, mix_w_in, mix_w_out, gdn_conv_w, gdn_a_log, gdn_dt_bias, gdn_norm_w, rwkv_mu, rwkv_w0, rwkv_w2, rwkv_a0, rwkv_a2, rwkv_g2, rwkv_k_k, rwkv_k_a, rwkv_r_k, rwkv_ln_w, rwkv_ln_b, lru_w_in, lru_conv_w, lru_conv_b, lru_wa, lru_ba, lru_wx, lru_bx, lru_lambda, lru_w_out):
    bsz, seq, d = x.shape
    depth = norm_pre.shape[0]
    x2 = x.reshape(bsz * seq, d)
    mods = _ada_params(c, ada_w, ada_b)

    def mod(layer, sub):
        m = mods[layer * 2 + sub]
        return (m[:, None, 0:d], m[:, None, d:2 * d], m[:, None, 2 * d:3 * d])

    GW = GDN_WIDTH
    for layer in range(depth):
        j = layer // 2
        shift, scale, gate = mod(layer, 0)
        shift2, scale2, gate2 = mod(layer, 1)
        mod_rows = jnp.concatenate([gate, shift2, scale2, gate2], axis=1)
        nw_rows = jnp.stack([norm_post[layer, 0], norm_pre[layer, 1], norm_post[layer, 1]])
        nw_pre = norm_pre[layer, 0].reshape(1, d)
        ffn_w = (ffn_w_gate[layer].astype(BF16), ffn_w_up[layer].astype(BF16), ffn_w_down[layer].astype(BF16))
        if layer % 2 == 0:
            w_in = mix_w_in[j]
            n_gdn = 4 * GW + 2 * GDN_HEADS
            w_cat = jnp.concatenate(
                [w_in[:, 0:4 * GW], w_in[:, n_gdn:],
                 jnp.pad(w_in[:, 4 * GW:n_gdn], ((0, 0), (0, LANES - 2 * GDN_HEADS)))], axis=1).astype(BF16)
            qkvz, rw, ab = _norm_proj(x2, nw_pre, shift, scale, w_cat, (4 * GW, RWKV_COLS, LANES), seq)
            out_a = _gdn(qkvz, ab, gdn_conv_w[j], _pad_lanes(gdn_a_log[j]), _pad_lanes(gdn_dt_bias[j]),
                         gdn_norm_w[j].reshape(1, GDN_HEAD_DIM), bsz, seq)
            w2p = jnp.pad(rwkv_w2[j], ((0, LANES - DECAY_LORA), (0, 0))).astype(BF16)
            a2p = jnp.pad(rwkv_a2[j], ((DECAY_LORA, LANES - DECAY_LORA - AAA_LORA), (0, 0))).astype(BF16)
            vec = lambda t: t.reshape(1, RWKV_WIDTH)
            out_b = _rwkv(rw, rwkv_mu[j].reshape(1, RWKV_COLS), vec(rwkv_w0[j]), w2p, vec(rwkv_a0[j]), a2p,
                          rwkv_g2[j].astype(BF16), vec(rwkv_k_k[j]), vec(rwkv_k_a[j]), vec(rwkv_r_k[j]),
                          vec(rwkv_ln_w[j]), vec(rwkv_ln_b[j]), bsz, seq)
            w_out = mix_w_out[j].astype(BF16)
            x2 = _out_ffn(x2, [out_a, out_b], [w_out[0:GW], w_out[GW:]], mod_rows, nw_rows, *ffn_w, seq)
        else:
            width = lru_w_in.shape[2] // 2
            gb, xb = _norm_proj(x2, nw_pre, shift, scale, lru_w_in[j].astype(BF16), (width, width), seq)
            vec = lambda t: t.reshape(1, width)
            y = _rglru(gb, xb, lru_conv_w[j], vec(lru_conv_b[j]), lru_wa[j].astype(BF16), vec(lru_ba[j]),
                       lru_wx[j].astype(BF16), vec(lru_bx[j]), vec(lru_lambda[j]), bsz, seq)
            x2 = _out_ffn(x2, [y], [lru_w_out[j].astype(BF16)], mod_rows, nw_rows, *ffn_w, seq)
    return x2.reshape(bsz, seq, d)
```

```python
import functools

import jax
import jax.numpy as jnp
from jax import lax
from jax.experimental import pallas as pl
from jax.experimental.pallas import tpu as pltpu

F32 = jnp.float32
BF16 = jnp.bfloat16

NORM_EPS = 1e-6
GDN_HEADS = 4
GDN_HEAD_DIM = 128
GDN_WIDTH = GDN_HEADS * GDN_HEAD_DIM
CHUNK = 64
CONV_WIDTH = 4
RWKV_HEADS = 8
RWKV_HEAD_DIM = 64
RWKV_WIDTH = RWKV_HEADS * RWKV_HEAD_DIM
DECAY_LORA = 64
AAA_LORA = 64
GATE_LORA = 128
RWKV_COLS = 3 * RWKV_WIDTH + DECAY_LORA + AAA_LORA + GATE_LORA
RWKV_GN_EPS = 64e-5
LRU_BLOCKS = 4
LRU_C = 8.0
LANES = 128
MXU_DIM = 256
SUB = 8
HALO = SUB
VMEM_LIMIT = 56 * 1024 * 1024

NN = (((1,), (0,)), ((), ()))
NT = (((1,), (1,)), ((), ()))
TN = (((0,), (0,)), ((), ()))


def _mm(a, b, dims=NN):
    return lax.dot_general(a, b, dims, preferred_element_type=F32)


def _mm_bf(a, b, dims=NN):
    return _mm(a.astype(BF16), b.astype(BF16), dims)


def _split2(x):
    hi = x.astype(BF16)
    lo = (x - hi.astype(F32)).astype(BF16)
    return hi, lo


def _mm_x3(a, b, dims=NN):
    ah, al = _split2(a)
    bh, bl = _split2(b)
    return _mm(ah, bh, dims) + (_mm(ah, bl, dims) + _mm(al, bh, dims))


def _mm_const_lhs(c_bf, x):
    hi = x.astype(BF16)
    r1 = x - hi.astype(F32)
    mid = r1.astype(BF16)
    lo = (r1 - mid.astype(F32)).astype(BF16)
    return _mm(c_bf, hi) + (_mm(c_bf, mid) + _mm(c_bf, lo))


def _mm_const_rhs(x, c_bf):
    hi, lo = _split2(x)
    return _mm(hi, c_bf) + _mm(lo, c_bf)


def _sigmoid(x):
    return 1.0 / (1.0 + jnp.exp(-x))


def _silu(x):
    return x * _sigmoid(x)


def _softplus(x):
    return jnp.maximum(x, 0.0) + jnp.log1p(jnp.exp(-jnp.abs(x)))


def _gelu_tanh(x):
    return 0.5 * x * (1.0 + jnp.tanh(0.7978845608028654 * (x + 0.044715 * (x * x * x))))


def _rms(x):
    return x * lax.rsqrt(jnp.mean(x * x, axis=-1, keepdims=True) + NORM_EPS)


def _tri_masks(n):
    row = lax.broadcasted_iota(jnp.int32, (n, n), 0)
    col = lax.broadcasted_iota(jnp.int32, (n, n), 1)
    return row, col


def _pack_blocks(x):
    out = x[0:CHUNK]
    for c in range(1, x.shape[0] // CHUNK):
        out = out + x[c * CHUNK:(c + 1) * CHUNK]
    return out


def _expand_blocks(xp_bf, mask_bf):
    return jnp.concatenate([xp_bf] * (xp_bf.shape[1] // CHUNK), axis=0) * mask_bf


def _unit_lower_inverses(ms_p, mask_bf):
    n = ms_p[0].shape[1]
    prow = lax.broadcasted_iota(jnp.int32, (CHUNK, n), 0)
    pcol = lax.broadcasted_iota(jnp.int32, (CHUNK, n), 1) & (CHUNK - 1)
    eye_p = (prow == pcol).astype(F32)
    expand = lambda xp: _expand_blocks(xp.astype(BF16), mask_bf)
    mm = lambda xp, y_bf: _mm(xp.astype(BF16), y_bf)
    same = (prow >> 3) == (pcol >> 3)
    m8_p = [jnp.where(same, mp, 0.0) for mp in ms_p]
    m2_p = [mm(ap, expand(ap)) for ap in m8_p]
    m4_p = [mm(ap, expand(ap)) for ap in m2_p]
    inv_p = [mm(eye_p - ap, expand(eye_p + bp)) for ap, bp in zip(m8_p, m2_p)]
    inv_p = [mm(ap, expand(eye_p + bp)) for ap, bp in zip(inv_p, m4_p)]
    shift = 3
    while (1 << shift) < CHUNK:
        pair = (prow >> (shift + 1)) == (pcol >> (shift + 1))
        off_mask = jnp.logical_and(pair, jnp.logical_not(same))
        tmp_p = [mm(ap, expand(jnp.where(off_mask, mp, 0.0))) for ap, mp in zip(inv_p, ms_p)]
        inv_p = [ap - mm(tp, expand(ap)) for ap, tp in zip(inv_p, tmp_p)]
        same = pair
        shift += 1
    a_split = [_split2(ap) for ap in inv_p]
    m_split = [_split2(mp) for mp in ms_p]
    e_hi = [_expand_blocks(a_hi, mask_bf) for a_hi, _ in a_split]
    e_lo = [_expand_blocks(a_lo, mask_bf) for _, a_lo in a_split]
    prod = [_mm(m_hi, eh) + (_mm(m_hi, el) + _mm(m_lo, eh)) for (m_hi, m_lo), eh, el in zip(m_split, e_hi, e_lo)]
    res = [expand(eye_p - (ap + pr)) for ap, pr in zip(inv_p, prod)]
    return [ap + _mm(a_hi, r) for ap, (a_hi, _), r in zip(inv_p, a_split, res)]


def _ada_kernel(c_ref, w_ref, b_ref, o_ref):
    s = _silu(c_ref[...])
    o_ref[0] = _mm_x3(s, w_ref[0]) + b_ref[0]


def _ada_params(c, ada_w, ada_b):
    n_l, n_s, d, d3 = ada_w.shape
    n = n_l * n_s
    bsz = c.shape[0]
    tn = 1024
    return pl.pallas_call(
        _ada_kernel,
        out_shape=jax.ShapeDtypeStruct((n, bsz, d3), F32),
        grid=(n, d3 // tn),
        in_specs=[
            pl.BlockSpec((bsz, d), lambda i, j: (0, 0)),
            pl.BlockSpec((1, d, tn), lambda i, j: (i, 0, j)),
            pl.BlockSpec((1, 1, tn), lambda i, j: (i, 0, j)),
        ],
        out_specs=pl.BlockSpec((1, bsz, tn), lambda i, j: (i, 0, j)),
        compiler_params=pltpu.CompilerParams(vmem_limit_bytes=VMEM_LIMIT),
        name="ada_params",
    )(c, ada_w.reshape(n, d, d3), ada_b.reshape(n, 1, d3))


def _norm_proj_kernel(x_ref, nw_ref, shift_ref, scale_ref, w_ref, *out_refs):
    tm = x_ref.shape[0]
    halves = [slice(0, tm // 2), slice(tm // 2, tm)]
    hs = [((_rms(x_ref[rows, :]) * nw_ref[...]) * (1.0 + scale_ref[0]) + shift_ref[0]).astype(BF16)
          for rows in halves]
    for rows, h in zip(halves, hs):
        off = 0
        for o_ref in out_refs:
            n = o_ref.shape[1]
            o_ref[rows, :] = _mm(h, w_ref[:, off:off + n])
            off += n


def _norm_proj(x2, nw, shift, scale, w_bf, splits, seq, tm=512):
    n_tok, d = x2.shape
    per_b = seq // tm
    return pl.pallas_call(
        _norm_proj_kernel,
        out_shape=[jax.ShapeDtypeStruct((n_tok, n), F32) for n in splits],
        grid=(n_tok // tm,),
        in_specs=[
            pl.BlockSpec((tm, d), lambda i: (i, 0)),
            pl.BlockSpec((1, d), lambda i: (0, 0)),
            pl.BlockSpec((1, 1, d), lambda i: (i // per_b, 0, 0)),
            pl.BlockSpec((1, 1, d), lambda i: (i // per_b, 0, 0)),
            pl.BlockSpec(w_bf.shape, lambda i: (0, 0), pipeline_mode=pl.Buffered(1)),
        ],
        out_specs=[pl.BlockSpec((tm, n), lambda i: (i, 0)) for n in splits],
        compiler_params=pltpu.CompilerParams(vmem_limit_bytes=VMEM_LIMIT),
        name="norm_proj",
    )(x2, nw, shift, scale, w_bf)


def _out_ffn_kernel(*refs, n_in):
    x_ref = refs[0]
    a_refs = refs[1:1 + n_in]
    w_refs = refs[1 + n_in:1 + 2 * n_in]
    mod_ref, nw_ref, wg_ref, wu_ref, wd_ref, o_ref = refs[1 + 2 * n_in:]
    mod = mod_ref[0]
    nw = nw_ref[...]
    tm = x_ref.shape[0]
    halves = [slice(0, tm // 2), slice(tm // 2, tm)]
    ys = []
    for rows in halves:
        y = _mm(a_refs[0][rows, :].astype(BF16), w_refs[0][...])
        for a_ref, w_ref in zip(a_refs[1:], w_refs[1:]):
            y = y + _mm(a_ref[rows, :].astype(BF16), w_ref[...])
        ys.append(y)
    xs = [x_ref[rows, :] + mod[0:1] * (_rms(y) * nw[0:1]) for rows, y in zip(halves, ys)]
    hs = [((_rms(x) * nw[1:2]) * (1.0 + mod[2:3]) + mod[1:2]).astype(BF16) for x in xs]
    gs = [_mm(h, wg_ref[...]) for h in hs]
    us = [_mm(h, wu_ref[...]) for h in hs]
    acts = [(_silu(g) * u).astype(BF16) for g, u in zip(gs, us)]
    ys = [_mm(act, wd_ref[...]) for act in acts]
    for rows, x, y in zip(halves, xs, ys):
        o_ref[rows, :] = x + mod[3:4] * (_rms(y) * nw[2:3])


def _out_ffn(x2, acts, ws_bf, mod, nw, wg_bf, wu_bf, wd_bf, seq, tm=512):
    n_tok, d = x2.shape
    per_b = seq // tm
    row = lambda i: (i, 0)
    const = lambda i: (0, 0)
    resident = lambda w: pl.BlockSpec(w.shape, const, pipeline_mode=pl.Buffered(1))
    in_specs = [pl.BlockSpec((tm, d), row)]
    in_specs += [pl.BlockSpec((tm, a.shape[1]), row) for a in acts]
    in_specs += [resident(w) for w in ws_bf]
    in_specs += [pl.BlockSpec((1,) + mod.shape[1:], lambda i: (i // per_b, 0, 0)),
                 pl.BlockSpec(nw.shape, const), resident(wg_bf), resident(wu_bf), resident(wd_bf)]
    return pl.pallas_call(
        functools.partial(_out_ffn_kernel, n_in=len(acts)),
        out_shape=jax.ShapeDtypeStruct((n_tok, d), F32),
        grid=(n_tok // tm,),
        in_specs=in_specs,
        out_specs=pl.BlockSpec((tm, d), row),
        compiler_params=pltpu.CompilerParams(vmem_limit_bytes=VMEM_LIMIT),
        name="out_ffn",
    )(x2, *acts, *ws_bf, mod, nw, wg_bf, wu_bf, wd_bf)


def _gdn_kernel(qkvz_ref, ab_ref, convw_ref, alog_ref, dtb_ref, nw_ref, o_ref,
                halo_ref, qkv_s, gc_s, u_s, w_s, qd_s, kd_s, attn_s, o_s, state_ref, *, tt, nb):
    W = GDN_WIDTH
    D = GDN_HEAD_DIM
    C = CHUNK
    H = GDN_HEADS
    n_c = tt // C

    @pl.when(pl.program_id(1) == 0)
    def _():
        halo_ref[...] = jnp.zeros_like(halo_ref)
        state_ref[...] = jnp.zeros_like(state_ref)

    row, col = _tri_masks(tt)
    same_chunk = (row >> 6) == (col >> 6)
    causal = jnp.logical_and(same_chunk, row >= col)
    strict = jnp.logical_and(same_chunk, row > col)
    tril_bf = causal.astype(BF16)
    mask_bf = same_chunk.astype(BF16)
    cw = convw_ref[...]

    units = [(bi, h) for bi in range(nb) for h in range(H)]
    qs, ks, vs, betas, k_betas, decays, e_gs, e_rems = [], [], [], [], [], [], [], []
    for bi in range(nb):
        x = qkvz_ref[bi, :, 0:3 * W]
        xp = jnp.concatenate([halo_ref[bi], x], axis=0)
        y = x * cw[CONV_WIDTH - 1:CONV_WIDTH]
        for j in range(CONV_WIDTH - 1):
            o = HALO - (CONV_WIDTH - 1) + j
            y = y + xp[o:o + tt] * cw[j:j + 1]
        halo_ref[bi] = x[tt - HALO:tt]
        y = _silu(y)
        for h in range(H):
            q = y[:, h * D:(h + 1) * D]
            k = y[:, W + h * D:W + (h + 1) * D]
            q = q * lax.rsqrt(jnp.sum(q * q, axis=-1, keepdims=True) + NORM_EPS) * (D ** -0.5)
            k = k * lax.rsqrt(jnp.sum(k * k, axis=-1, keepdims=True) + NORM_EPS)
            qkv_s[bi, :, h * D:(h + 1) * D] = q
            qkv_s[bi, :, W + h * D:W + (h + 1) * D] = k
        qkv_s[bi, :, 2 * W:3 * W] = y[:, 2 * W:3 * W]

        ab = ab_ref[bi]
        g = -jnp.exp(alog_ref[...]) * _softplus(ab + dtb_ref[...])
        beta_all = _sigmoid(ab)
        gc = _mm_const_lhs(tril_bf, g)
        gc_s[bi] = gc
        gc_t = gc.T
        g_last = jnp.concatenate(
            [jnp.broadcast_to(gc[c * C + C - 1:c * C + C, :], (C, LANES)) for c in range(n_c)], axis=0)
        e_g = jnp.exp(gc)
        e_rem = jnp.exp(g_last - gc)
        for h in range(H):
            qs.append(qkv_s[bi, :, h * D:(h + 1) * D])
            ks.append(qkv_s[bi, :, W + h * D:W + (h + 1) * D])
            vs.append(qkv_s[bi, :, 2 * W + h * D:2 * W + (h + 1) * D])
            betas.append(beta_all[:, H + h:H + h + 1])
            k_betas.append(ks[-1] * betas[-1])
            decays.append(
                jnp.where(causal, jnp.exp(jnp.minimum(gc[:, h:h + 1] - gc_t[h:h + 1, :], 0.0)), 0.0))
            e_gs.append(e_g[:, h:h + 1])
            e_rems.append(e_rem[:, h:h + 1])

    ms = [jnp.where(strict, _mm_bf(kb, k, NT) * dc, 0.0) for kb, k, dc in zip(k_betas, ks, decays)]
    t_ps = _unit_lower_inverses([_pack_blocks(m) for m in ms], mask_bf)
    for i in range(len(units)):
        rhs = jnp.concatenate([vs[i] * betas[i], k_betas[i] * e_gs[i]], axis=1)
        t_hi, t_lo = _split2(t_ps[i])
        rhs_hi, rhs_lo = _split2(rhs)
        t_hi = _expand_blocks(t_hi, mask_bf)
        uw = _mm(t_hi, rhs_hi) + (_mm(t_hi, rhs_lo) + _mm(_expand_blocks(t_lo, mask_bf), rhs_hi))
        u_s[i] = uw[:, 0:D]
        w_s[i] = uw[:, D:2 * D]
        attn_s[i] = _mm_bf(qs[i], ks[i], NT) * decays[i]
        qd_s[i] = qs[i] * e_gs[i]
        kd_s[i] = ks[i] * e_rems[i]

    for c in range(n_c):
        rows = slice(c * C, (c + 1) * C)
        win = slice((c // 2) * 2 * C, (c // 2 + 1) * 2 * C)
        ss = [state_ref[i] for i in range(len(units))]
        wq = [_mm_bf(jnp.concatenate([w_s[i, rows, :], qd_s[i, rows, :]], axis=0), s) for i, s in enumerate(ss)]
        v_new = [u_s[i, rows, :] - x[0:C] for i, x in enumerate(wq)]
        kv = [_mm_bf(kd_s[i, rows, :], x, TN) for i, x in enumerate(v_new)]
        for i, (bi, h) in enumerate(units):
            gl = jnp.exp(gc_s[bi, c * C + C - 1:c * C + C, h:h + 1])
            state_ref[i] = ss[i] * gl + kv[i]
            vv = jnp.concatenate([v_new[i], v_new[i]], axis=0)
            o_s[bi, rows, h * D:(h + 1) * D] = wq[i][C:2 * C] + _mm_bf(attn_s[i, rows, win], vv)

    nw = nw_ref[...]
    for bi, h in units:
        z = qkvz_ref[bi, :, 3 * W + h * D:3 * W + (h + 1) * D]
        o_ref[bi, :, h * D:(h + 1) * D] = _rms(o_s[bi, :, h * D:(h + 1) * D]) * nw * _silu(z)


def _gdn(qkvz, ab, conv_w, a_log_row, dt_row, norm_w, bsz, seq, tt=256, nb=2):
    W = GDN_WIDTH
    tile = lambda b, j: (b, j, 0)
    const = lambda b, j: (0, 0)
    n_u = nb * GDN_HEADS
    out = pl.pallas_call(
        functools.partial(_gdn_kernel, tt=tt, nb=nb),
        out_shape=jax.ShapeDtypeStruct((bsz, seq, W), F32),
        grid=(bsz // nb, seq // tt),
        in_specs=[
            pl.BlockSpec((nb, tt, 4 * W), tile),
            pl.BlockSpec((nb, tt, LANES), tile),
            pl.BlockSpec(conv_w.shape, const),
            pl.BlockSpec((1, LANES), const),
            pl.BlockSpec((1, LANES), const),
            pl.BlockSpec((1, GDN_HEAD_DIM), const),
        ],
        out_specs=pl.BlockSpec((nb, tt, W), tile),
        scratch_shapes=[
            pltpu.VMEM((nb, HALO, 3 * W), F32),
            pltpu.VMEM((nb, tt, 3 * W), F32),
            pltpu.VMEM((nb, tt, LANES), F32),
            pltpu.VMEM((n_u, tt, GDN_HEAD_DIM), F32),
            pltpu.VMEM((n_u, tt, GDN_HEAD_DIM), F32),
            pltpu.VMEM((n_u, tt, GDN_HEAD_DIM), F32),
            pltpu.VMEM((n_u, tt, GDN_HEAD_DIM), F32),
            pltpu.VMEM((n_u, tt, tt), F32),
            pltpu.VMEM((nb, tt, W), F32),
            pltpu.VMEM((n_u, GDN_HEAD_DIM, GDN_HEAD_DIM), F32),
        ],
        compiler_params=pltpu.CompilerParams(
            dimension_semantics=("arbitrary", "arbitrary"), vmem_limit_bytes=VMEM_LIMIT),
        name="gdn",
    )(qkvz.reshape(bsz, seq, 4 * W), ab.reshape(bsz, seq, LANES), conv_w, a_log_row, dt_row, norm_w)
    return out.reshape(bsz * seq, W)


def _rwkv_kernel(rw_ref, mu_ref, w0_ref, w2_ref, a0_ref, a2_ref, g2_ref, kk_ref, ka_ref, rk_ref,
                 lnw_ref, lnb_ref, o_ref,
                 carry_ref, r_s, k_s, b_s, kk_s, kend_s, bend_s, v_s, y_s, u_s, w_s, rv_s, bonus_s, gate_s,
                 lct_s, arb_s, state_ref, *, tt, nb):
    W = RWKV_WIDTH
    C = CHUNK
    NH = RWKV_HEADS
    P2 = 2 * RWKV_HEAD_DIM
    n_c = tt // C
    n_p = NH // 2

    @pl.when(pl.program_id(1) == 0)
    def _():
        carry_ref[...] = jnp.zeros_like(carry_ref)
        state_ref[...] = jnp.zeros_like(state_ref)

    hrow, hcol = _tri_masks(MXU_DIM)
    head_ones = ((hrow >> 6) == (hcol >> 6)).astype(BF16)

    def head_sums(x):
        return jnp.concatenate(
            [_mm_const_rhs(x[:, o:o + MXU_DIM], head_ones) for o in range(0, W, MXU_DIM)], axis=1)

    row, col = _tri_masks(tt)
    tril_bf = jnp.logical_and((row >> 6) == (col >> 6), row >= col).astype(BF16)
    trow = lax.broadcasted_iota(jnp.int32, (tt, 1), 0)

    for bi in range(nb):
        cf = rw_ref[bi]
        prev = jnp.where(trow == 0, carry_ref[bi, HALO - 1:HALO, :], pltpu.roll(cf, 1, axis=0))
        carry_ref[bi] = cf[tt - HALO:tt]
        cf = cf + mu_ref[...] * (prev - cf)

        r = cf[:, 0:W]
        k = cf[:, W:2 * W]
        v = cf[:, 2 * W:3 * W]
        wd_ad = cf[:, 3 * W:3 * W + LANES]
        gd = cf[:, 3 * W + LANES:3 * W + 2 * LANES]

        w_log = -_softplus(-(w0_ref[...] + _mm_bf(jnp.tanh(wd_ad), w2_ref[...]))) - 0.5
        a = _sigmoid(a0_ref[...] + _mm_bf(wd_ad, a2_ref[...]))
        gate_s[bi] = _mm_bf(_sigmoid(gd), g2_ref[...])
        kkv = k * kk_ref[...]
        kk = kkv * lax.rsqrt(head_sums(kkv * kkv) + NORM_EPS)
        kmod = k * (1.0 + (a - 1.0) * ka_ref[...])
        bonus_s[bi] = head_sums(r * kmod * rk_ref[...]) * v
        lw = -jnp.exp(w_log)

        lc = _mm_const_lhs(tril_bf, lw)
        lct_s[bi] = lc.T
        l_last = jnp.concatenate(
            [jnp.broadcast_to(lc[c * C + C - 1:c * C + C, :], (C, W)) for c in range(n_c)], axis=0)
        e_inv = jnp.exp(-lc)
        e_rem = jnp.exp(l_last - lc)
        b = kk * a
        r_s[bi] = r * jnp.exp(lc)
        k_s[bi] = kmod * e_inv
        b_s[bi] = b * e_inv
        kk_s[bi] = kk * jnp.exp(lc - lw)
        kend_s[bi] = kmod * e_rem
        bend_s[bi] = b * e_rem
        v_s[bi] = v

    lane = lax.broadcasted_iota(jnp.int32, (1, P2), 1)
    first = lane < RWKV_HEAD_DIM
    prow = lax.broadcasted_iota(jnp.int32, (C, 2 * P2), 0)
    pcol = lax.broadcasted_iota(jnp.int32, (C, 2 * P2), 1) & (C - 1)
    strict_p = prow > pcol
    causal_p = prow >= pcol

    def per_head_rows(x):
        return jnp.concatenate([jnp.where(first, x, 0.0), jnp.where(first, 0.0, x)], axis=0)

    units = [(bi, p, c) for bi in range(nb) for p in range(n_p) for c in range(n_c)]
    rows_of = lambda c: slice(c * C, (c + 1) * C)
    lanes_of = lambda p: slice(p * P2, (p + 1) * P2)
    cut = lambda ref, u: ref[u[0], rows_of(u[2]), lanes_of(u[1])]

    prod = [_mm_bf(jnp.concatenate([cut(kk_s, u), cut(r_s, u)], axis=0),
                   jnp.concatenate([per_head_rows(cut(b_s, u)), per_head_rows(cut(k_s, u))], axis=0), NT)
            for u in units]
    kbkk = [jnp.where(strict_p, x[0:C], 0.0) for x in prod]
    rbrk = [jnp.where(causal_p, x[C:2 * C], 0.0) for x in prod]
    for i in range(len(units)):
        arb_s[i] = rbrk[i][:, 0:P2]
    avrv = [_mm_bf(jnp.concatenate([x[:, P2:2 * P2], y[:, P2:2 * P2]], axis=0), per_head_rows(cut(v_s, u)))
            for x, y, u in zip(kbkk, rbrk, units)]
    for u, x in zip(units, avrv):
        rv_s[u[0], rows_of(u[2]), lanes_of(u[1])] = x[C:2 * C]
    prow2, pcol2 = _tri_masks(2 * P2)
    t_p = _unit_lower_inverses(
        [jnp.concatenate([kbkk[i][:, 0:P2], kbkk[i + 1][:, 0:P2]], axis=1) for i in range(0, len(units), 2)],
        ((prow2 >> 6) == (pcol2 >> 6)).astype(BF16))
    first2 = jnp.concatenate([first, first], axis=1)
    for i, u in enumerate(units):
        x = jnp.concatenate([avrv[i][0:C], cut(kk_s, u)], axis=1)
        x = jnp.concatenate([jnp.where(first2, x, 0.0), jnp.where(first2, 0.0, x)], axis=0)
        uw = _mm_x3(t_p[i // 2][:, (i % 2) * P2:(i % 2 + 1) * P2], x)
        u_s[u[0], rows_of(u[2]), lanes_of(u[1])] = uw[:, 0:P2]
        w_s[u[0], rows_of(u[2]), lanes_of(u[1])] = uw[:, P2:2 * P2]

    brow, bcol = _tri_masks(P2)
    blockdiag = (brow >> 6) == (bcol >> 6)
    pairs = [(bi, p) for bi in range(nb) for p in range(n_p)]
    for c in range(n_c):
        us = [(bi, p, c) for bi, p in pairs]
        ss = [state_ref[bi * n_p + p] for bi, p in pairs]
        rw = [_mm_bf(jnp.concatenate([cut(r_s, u), cut(w_s, u)], axis=0), s) for u, s in zip(us, ss)]
        pm = [-(x[C:2 * C] + cut(u_s, u)) for u, x in zip(us, rw)]
        upd = [_mm_bf(jnp.concatenate([cut(bend_s, u), cut(kend_s, u)], axis=0),
                      jnp.concatenate([x, cut(v_s, u)], axis=0), TN) for u, x in zip(us, pm)]
        for i, (bi, p) in enumerate(pairs):
            g_col = jnp.exp(lct_s[bi, p * P2:(p + 1) * P2, c * C + C - 1:c * C + C])
            state_ref[bi * n_p + p] = ss[i] * g_col + jnp.where(blockdiag, upd[i], 0.0)
            y_s[bi, rows_of(c), lanes_of(p)] = rw[i][0:C] + cut(rv_s, us[i]) + _mm_bf(
                arb_s[(bi * n_p + p) * n_c + c], per_head_rows(pm[i]))

    inv_n = 1.0 / RWKV_HEAD_DIM
    for bi in range(nb):
        y = y_s[bi]
        mean = head_sums(y) * inv_n
        d = y - mean
        var = head_sums(d * d) * inv_n
        yn = d * lax.rsqrt(var + RWKV_GN_EPS) * lnw_ref[...] + lnb_ref[...]
        o_ref[bi] = (yn + bonus_s[bi]) * gate_s[bi]


def _rwkv(rw, mu, w0, w2p, a0, a2p, g2, k_k, k_a, r_k, ln_w, ln_b, bsz, seq, tt=256, nb=2):
    cols = rw.shape[1]
    W = RWKV_WIDTH
    tile = lambda b, j: (b, j, 0)
    const = lambda b, j: (0, 0)
    vec = pl.BlockSpec((1, W), const)
    out = pl.pallas_call(
        functools.partial(_rwkv_kernel, tt=tt, nb=nb),
        out_shape=jax.ShapeDtypeStruct((bsz, seq, W), F32),
        grid=(bsz // nb, seq // tt),
        in_specs=[
            pl.BlockSpec((nb, tt, cols), tile),
            pl.BlockSpec((1, cols), const),
            vec,
            pl.BlockSpec(w2p.shape, const),
            vec,
            pl.BlockSpec(a2p.shape, const),
            pl.BlockSpec(g2.shape, const),
            vec, vec, vec, vec, vec,
        ],
        out_specs=pl.BlockSpec((nb, tt, W), tile),
        scratch_shapes=[pltpu.VMEM((nb, HALO, cols), F32)]
        + [pltpu.VMEM((nb, tt, W), F32) for _ in range(13)]
        + [pltpu.VMEM((nb, W, tt), F32),
           pltpu.VMEM((nb * (RWKV_HEADS // 2) * (tt // CHUNK), CHUNK, 2 * RWKV_HEAD_DIM), F32),
           pltpu.VMEM((nb * RWKV_HEADS // 2, 2 * RWKV_HEAD_DIM, 2 * RWKV_HEAD_DIM), F32)],
        compiler_params=pltpu.CompilerParams(
            dimension_semantics=("arbitrary", "arbitrary"), vmem_limit_bytes=VMEM_LIMIT),
        name="rwkv7",
    )(rw.reshape(bsz, seq, cols), mu, w0, w2p, a0, a2p, g2, k_k, k_a, r_k, ln_w, ln_b)
    return out.reshape(bsz * seq, W)


def _rglru_kernel(gb_ref, xb_ref, convw_ref, convb_ref, wa_ref, ba_ref, wx_ref, bx_ref, lam_ref, o_ref,
                  halo_ref, h_ref, *, tt):
    @pl.when(pl.program_id(1) == 0)
    def _():
        halo_ref[...] = jnp.zeros_like(halo_ref)
        h_ref[...] = jnp.zeros_like(h_ref)

    x = xb_ref[...]
    xp = jnp.concatenate([halo_ref[...], x], axis=0)
    cw = convw_ref[...]
    xc = x * cw[CONV_WIDTH - 1:CONV_WIDTH] + convb_ref[...]
    for j in range(CONV_WIDTH - 1):
        o = HALO - (CONV_WIDTH - 1) + j
        xc = xc + xp[o:o + tt] * cw[j:j + 1]
    halo_ref[...] = x[tt - HALO:tt]

    blk = x.shape[1] // LRU_BLOCKS
    xc_bf = xc.astype(BF16)
    ra = jnp.concatenate(
        [_mm(xc_bf[:, n * blk:(n + 1) * blk], wa_ref[n]) for n in range(LRU_BLOCKS)], axis=1)
    ix = jnp.concatenate(
        [_mm(xc_bf[:, n * blk:(n + 1) * blk], wx_ref[n]) for n in range(LRU_BLOCKS)], axis=1)
    r = _sigmoid(ra + ba_ref[...])
    i = _sigmoid(ix + bx_ref[...])
    log_a = -LRU_C * r * _softplus(-lam_ref[...])
    a = jnp.exp(log_a)
    u = xc * i * jnp.sqrt(1.0 - jnp.exp(2.0 * log_a))

    width = x.shape[1]
    a = a.reshape(tt // SUB, SUB, width)
    u = u.reshape(tt // SUB, SUB, width)
    sub = lax.broadcasted_iota(jnp.int32, (1, SUB, 1), 1)
    d = 1
    while d < SUB:
        keep = sub >= d
        a_sh = jnp.where(keep, pltpu.roll(a, d, axis=1), 1.0)
        u_sh = jnp.where(keep, pltpu.roll(u, d, axis=1), 0.0)
        u = a * u_sh + u
        a = a * a_sh
        d *= 2
    gate = _gelu_tanh(gb_ref[...])
    carry = h_ref[...]
    for g in range(tt // SUB):
        rows = slice(g * SUB, (g + 1) * SUB)
        h = u[g] + a[g] * carry
        o_ref[rows, :] = h * gate[rows]
        carry = jnp.broadcast_to(h[SUB - 1:SUB, :], h.shape)
    h_ref[...] = carry


def _rglru(gb, xb, conv_w, conv_b, wa_bf, ba, wx_bf, bx, lam, bsz, seq, tt=256):
    n_tok, width = xb.shape
    per_b = seq // tt
    row = lambda b, j: (b * per_b + j, 0)
    const = lambda b, j: (0, 0)
    const3 = lambda b, j: (0, 0, 0)
    vec = pl.BlockSpec((1, width), const)
    return pl.pallas_call(
        functools.partial(_rglru_kernel, tt=tt),
        out_shape=jax.ShapeDtypeStruct((n_tok, width), F32),
        grid=(bsz, per_b),
        in_specs=[
            pl.BlockSpec((tt, width), row),
            pl.BlockSpec((tt, width), row),
            pl.BlockSpec(conv_w.shape, const),
            vec,
            pl.BlockSpec(wa_bf.shape, const3),
            vec,
            pl.BlockSpec(wx_bf.shape, const3),
            vec,
            vec,
        ],
        out_specs=pl.BlockSpec((tt, width), row),
        scratch_shapes=[pltpu.VMEM((HALO, width), F32), pltpu.VMEM((HALO, width), F32)],
        compiler_params=pltpu.CompilerParams(
            dimension_semantics=("arbitrary", "arbitrary"), vmem_limit_bytes=VMEM_LIMIT),
        name="rglru",
    )(gb, xb, conv_w, conv_b, wa_bf, ba, wx_bf, bx, lam)


def _pad_lanes(v):
    return jnp.pad(v, (0, LANES - v.shape[0])).reshape(1, LANES)


def kernel(x, c, norm_pre, norm_post, ada_w, ada_b, ffn_w_gate, ffn_w_up, ffn_w_down, mix_w_in, mix_w_out, gdn_conv_w, gdn_a_log, gdn_dt_bias, gdn_norm_w, rwkv_mu, rwkv_w0, rwkv_w2, rwkv_a0, rwkv_a2, rwkv_g2, rwkv_k_k, rwkv_k_a, rwkv_r_k, rwkv_ln_w, rwkv_ln_b, lru_w_in, lru_conv_w, lru_conv_b, lru_wa, lru_ba, lru_wx, lru_bx, lru_lambda, lru_w_out):
    bsz, seq, d = x.shape
    depth = norm_pre.shape[0]
    x2 = x.reshape(bsz * seq, d)
    mods = _ada_params(c, ada_w, ada_b)

    def mod(layer, sub):
        m = mods[layer * 2 + sub]
        return (m[:, None, 0:d], m[:, None, d:2 * d], m[:, None, 2 * d:3 * d])

    GW = GDN_WIDTH
    for layer in range(depth):
        j = layer // 2
        shift, scale, gate = mod(layer, 0)
        shift2, scale2, gate2 = mod(layer, 1)
        mod_rows = jnp.concatenate([gate, shift2, scale2, gate2], axis=1)
        nw_rows = jnp.stack([norm_post[layer, 0], norm_pre[layer, 1], norm_post[layer, 1]])
        nw_pre = norm_pre[layer, 0].reshape(1, d)
        ffn_w = (ffn_w_gate[layer].astype(BF16), ffn_w_up[layer].astype(BF16), ffn_w_down[layer].astype(BF16))
        if layer % 2 == 0:
            w_in = mix_w_in[j].astype(BF16)
            n_gdn = 4 * GW + 2 * GDN_HEADS
            w_cat = jnp.concatenate(
                [w_in[:, 0:4 * GW], w_in[:, n_gdn:],
                 jnp.pad(w_in[:, 4 * GW:n_gdn], ((0, 0), (0, LANES - 2 * GDN_HEADS)))], axis=1)
            qkvz, rw, ab = _norm_proj(x2, nw_pre, shift, scale, w_cat, (4 * GW, RWKV_COLS, LANES), seq)
            out_a = _gdn(qkvz, ab, gdn_conv_w[j], _pad_lanes(gdn_a_log[j]), _pad_lanes(gdn_dt_bias[j]),
                         gdn_norm_w[j].reshape(1, GDN_HEAD_DIM), bsz, seq)
            w2p = jnp.pad(rwkv_w2[j], ((0, LANES - DECAY_LORA), (0, 0))).astype(BF16)
            a2p = jnp.pad(rwkv_a2[j], ((DECAY_LORA, LANES - DECAY_LORA - AAA_LORA), (0, 0))).astype(BF16)
            vec = lambda t: t.reshape(1, RWKV_WIDTH)
            out_b = _rwkv(rw, rwkv_mu[j].reshape(1, RWKV_COLS), vec(rwkv_w0[j]), w2p, vec(rwkv_a0[j]), a2p,
                          rwkv_g2[j].astype(BF16), vec(rwkv_k_k[j]), vec(rwkv_k_a[j]), vec(rwkv_r_k[j]),
                          vec(rwkv_ln_w[j]), vec(rwkv_ln_b[j]), bsz, seq)
            w_out = mix_w_out[j].astype(BF16)
            x2 = _out_ffn(x2, [out_a, out_b], [w_out[0:GW], w_out[GW:]], mod_rows, nw_rows, *ffn_w, seq)
        else:
            width = lru_w_in.shape[2] // 2
            gb, xb = _norm_proj(x2, nw_pre, shift, scale, lru_w_in[j].astype(BF16), (width, width), seq)
            vec = lambda t: t.reshape(1, width)
            y = _rglru(gb, xb, lru_conv_w[j], vec(lru_conv_b[j]), lru_wa[j].astype(BF16), vec(lru_ba[j]),
                       lru_wx[j].astype(BF16), vec(lru_bx[j]), vec(lru_lambda[j]), bsz, seq)
            x2 = _out_ffn(x2, [y], [lru_w_out[j].astype(BF16)], mod_rows, nw_rows, *ffn_w, seq)
    return x2.reshape(bsz, seq, d)
```

```python
import functools
import math

import jax
import jax.numpy as jnp
from jax import lax
from jax.experimental import pallas as pl
from jax.experimental.pallas import tpu as pltpu

F32 = jnp.float32
BF16 = jnp.bfloat16

NORM_EPS = 1e-6
LOG2_E = 1.4426950408889634
GDN_HEADS = 4
GDN_HEAD_DIM = 128
GDN_WIDTH = GDN_HEADS * GDN_HEAD_DIM
CHUNK = 64
CONV_WIDTH = 4
RWKV_HEADS = 8
RWKV_HEAD_DIM = 64
RWKV_WIDTH = RWKV_HEADS * RWKV_HEAD_DIM
DECAY_LORA = 64
AAA_LORA = 64
GATE_LORA = 128
RWKV_COLS = 3 * RWKV_WIDTH + DECAY_LORA + AAA_LORA + GATE_LORA
RWKV_GN_EPS = 64e-5
LRU_BLOCKS = 4
LRU_C = 8.0
LANES = 128
MXU_DIM = 256
SUB = 8
HALO = SUB
VMEM_LIMIT = 56 * 1024 * 1024

NN = (((1,), (0,)), ((), ()))
NT = (((1,), (1,)), ((), ()))
TN = (((0,), (0,)), ((), ()))


def _mm(a, b, dims=NN):
    return lax.dot_general(a, b, dims, preferred_element_type=F32)


def _mm_bf(a, b, dims=NN):
    return _mm(a.astype(BF16), b.astype(BF16), dims)


def _split2(x):
    hi = x.astype(BF16)
    lo = (x - hi.astype(F32)).astype(BF16)
    return hi, lo


def _mm_x3(a, b, dims=NN):
    ah, al = _split2(a)
    bh, bl = _split2(b)
    return _mm(ah, bh, dims) + (_mm(ah, bl, dims) + _mm(al, bh, dims))


def _mm_const_lhs(c_bf, x):
    hi = x.astype(BF16)
    r1 = x - hi.astype(F32)
    mid = r1.astype(BF16)
    lo = (r1 - mid.astype(F32)).astype(BF16)
    return _mm(c_bf, hi) + (_mm(c_bf, mid) + _mm(c_bf, lo))


def _mm_const_rhs(x, c_bf):
    hi, lo = _split2(x)
    return _mm(hi, c_bf) + _mm(lo, c_bf)


def _sigmoid(x):
    return 1.0 / (1.0 + jnp.exp(-x))


def _silu(x):
    return x * _sigmoid(x)


def _softplus(x):
    return jnp.maximum(x, 0.0) + jnp.log1p(jnp.exp(-jnp.abs(x)))


def _gelu_tanh(x):
    return 0.5 * x * (1.0 + jnp.tanh(0.7978845608028654 * (x + 0.044715 * (x * x * x))))


def _rms(x):
    return x * lax.rsqrt(jnp.mean(x * x, axis=-1, keepdims=True) + NORM_EPS)


def _tri_masks(n):
    row = lax.broadcasted_iota(jnp.int32, (n, n), 0)
    col = lax.broadcasted_iota(jnp.int32, (n, n), 1)
    return row, col


def _pack_blocks(x):
    out = x[0:CHUNK]
    for c in range(1, x.shape[0] // CHUNK):
        out = out + x[c * CHUNK:(c + 1) * CHUNK]
    return out


def _expand_blocks(xp_bf, mask_bf):
    return jnp.concatenate([xp_bf] * (xp_bf.shape[1] // CHUNK), axis=0) * mask_bf


def _unit_lower_inverses(ms_p, mask_bf):
    n = ms_p[0].shape[1]
    prow = lax.broadcasted_iota(jnp.int32, (CHUNK, n), 0)
    pcol = lax.broadcasted_iota(jnp.int32, (CHUNK, n), 1) & (CHUNK - 1)
    eye_p = (prow == pcol).astype(F32)
    expand = lambda xp: _expand_blocks(xp.astype(BF16), mask_bf)
    mm = lambda xp, y_bf: _mm(xp.astype(BF16), y_bf)
    same = (prow >> 3) == (pcol >> 3)
    m8_p = [jnp.where(same, mp, 0.0) for mp in ms_p]
    m2_p = [mm(ap, expand(ap)) for ap in m8_p]
    yield
    m4_p = [mm(ap, expand(ap)) for ap in m2_p]
    yield
    inv_p = [mm(eye_p - ap, expand(eye_p + bp)) for ap, bp in zip(m8_p, m2_p)]
    yield
    inv_p = [mm(ap, expand(eye_p + bp)) for ap, bp in zip(inv_p, m4_p)]
    yield
    shift = 3
    while (1 << shift) < CHUNK:
        pair = (prow >> (shift + 1)) == (pcol >> (shift + 1))
        off_mask = jnp.logical_and(pair, jnp.logical_not(same))
        tmp_p = [mm(ap, expand(jnp.where(off_mask, mp, 0.0))) for ap, mp in zip(inv_p, ms_p)]
        yield
        inv_p = [ap - mm(tp, expand(ap)) for ap, tp in zip(inv_p, tmp_p)]
        yield
        same = pair
        shift += 1
    a_split = [_split2(ap) for ap in inv_p]
    m_split = [_split2(mp) for mp in ms_p]
    e_hi = [_expand_blocks(a_hi, mask_bf) for a_hi, _ in a_split]
    e_lo = [_expand_blocks(a_lo, mask_bf) for _, a_lo in a_split]
    prod = [_mm(m_hi, eh) + (_mm(m_hi, el) + _mm(m_lo, eh)) for (m_hi, m_lo), eh, el in zip(m_split, e_hi, e_lo)]
    yield
    res = [expand(eye_p - (ap + pr)) for ap, pr in zip(inv_p, prod)]
    return [ap + _mm(a_hi, r) for ap, (a_hi, _), r in zip(inv_p, a_split, res)]


def _run(gen):
    try:
        while True:
            next(gen)
    except StopIteration as stop:
        return stop.value


def _interleave(*gens):
    live = list(gens)
    while live:
        for g in list(live):
            try:
                next(g)
            except StopIteration:
                live.remove(g)


def _ada_kernel(c_ref, w_ref, b_ref, o_ref):
    s = _silu(c_ref[...])
    o_ref[0] = _mm_x3(s, w_ref[0]) + b_ref[0]


def _ada_params(c, ada_w, ada_b):
    n_l, n_s, d, d3 = ada_w.shape
    n = n_l * n_s
    bsz = c.shape[0]
    tn = 1024
    return pl.pallas_call(
        _ada_kernel,
        out_shape=jax.ShapeDtypeStruct((n, bsz, d3), F32),
        grid=(n, d3 // tn),
        in_specs=[
            pl.BlockSpec((bsz, d), lambda i, j: (0, 0)),
            pl.BlockSpec((1, d, tn), lambda i, j: (i, 0, j)),
            pl.BlockSpec((1, 1, tn), lambda i, j: (i, 0, j)),
        ],
        out_specs=pl.BlockSpec((1, bsz, tn), lambda i, j: (i, 0, j)),
        compiler_params=pltpu.CompilerParams(vmem_limit_bytes=VMEM_LIMIT),
        name="ada_params",
    )(c, ada_w.reshape(n, d, d3), ada_b.reshape(n, 1, d3))


def _causal_conv(x, hist, cw):
    n = x.shape[0]
    xp = jnp.concatenate([hist, x], axis=0)
    y = x * cw[CONV_WIDTH - 1:CONV_WIDTH]
    for j in range(CONV_WIDTH - 1):
        o = HALO - (CONV_WIDTH - 1) + j
        y = y + xp[o:o + n] * cw[j:j + 1]
    return y


def _norm_proj_kernel(x_ref, nw_ref, shift_ref, scale_ref, w_ref, *refs, post, per_b):
    n_par = sum({"plain": 0, "conv_silu": 1, "conv_bias": 2, "shift_mix": 1}[p[0]] for p in post)
    par_refs = list(refs[:n_par])
    out_refs = refs[n_par:n_par + len(post)]
    hist_refs = list(refs[n_par + len(post):])

    @pl.when(pl.program_id(0) % per_b == 0)
    def _():
        for h_ref in hist_refs:
            h_ref[...] = jnp.zeros_like(h_ref)

    tm = x_ref.shape[0]
    halves = [slice(0, tm // 2), slice(tm // 2, tm)]
    hs = [((_rms(x_ref[rows, :]) * nw_ref[...]) * (1.0 + scale_ref[0]) + shift_ref[0]).astype(BF16)
          for rows in halves]
    off = 0
    for o_ref, p in zip(out_refs, post):
        n = o_ref.shape[1]
        ys = [_mm(h, w_ref[:, off:off + n]) for h in hs]
        off += n
        if p[0] == "plain":
            outs = ys
        else:
            h_ref = hist_refs.pop(0)
            hist = h_ref[...]
            h_ref[...] = ys[-1][tm // 2 - HALO:tm // 2]
            hists = [hist, ys[0][tm // 2 - HALO:tm // 2]]
            if p[0] == "conv_silu":
                cw = par_refs.pop(0)[...]
                outs = [jnp.concatenate([_silu(_causal_conv(y[:, 0:p[1]], hi[:, 0:p[1]], cw)), y[:, p[1]:n]], axis=1)
                        for y, hi in zip(ys, hists)]
            elif p[0] == "conv_bias":
                cw = par_refs.pop(0)[...]
                cb = par_refs.pop(0)[...]
                outs = [_causal_conv(y, hi, cw) + cb for y, hi in zip(ys, hists)]
            else:
                mu = par_refs.pop(0)[...]
                trow = lax.broadcasted_iota(jnp.int32, (tm // 2, 1), 0)
                outs = [y + mu * (jnp.where(trow == 0, hi[HALO - 1:HALO, :], pltpu.roll(y, 1, axis=0)) - y)
                        for y, hi in zip(ys, hists)]
        for rows, out in zip(halves, outs):
            o_ref[rows, :] = out


def _norm_proj(x2, nw, shift, scale, w_bf, splits, post, params, seq, tm=512):
    n_tok, d = x2.shape
    per_b = seq // tm
    const = lambda i: (0, 0)
    return pl.pallas_call(
        functools.partial(_norm_proj_kernel, post=post, per_b=per_b),
        out_shape=[jax.ShapeDtypeStruct((n_tok, n), F32) for n in splits],
        grid=(n_tok // tm,),
        in_specs=[
            pl.BlockSpec((tm, d), lambda i: (i, 0)),
            pl.BlockSpec((1, d), const),
            pl.BlockSpec((1, 1, d), lambda i: (i // per_b, 0, 0)),
            pl.BlockSpec((1, 1, d), lambda i: (i // per_b, 0, 0)),
            pl.BlockSpec(w_bf.shape, const, pipeline_mode=pl.Buffered(1)),
        ] + [pl.BlockSpec(p.shape, const) for p in params],
        out_specs=[pl.BlockSpec((tm, n), lambda i: (i, 0)) for n in splits],
        scratch_shapes=[pltpu.VMEM((HALO, n), F32) for n, p in zip(splits, post) if p[0] != "plain"],
        compiler_params=pltpu.CompilerParams(
            dimension_semantics=("arbitrary",), vmem_limit_bytes=VMEM_LIMIT),
        name="norm_proj",
    )(x2, nw, shift, scale, w_bf, *params)


def _out_ffn_kernel(*refs, n_in):
    x_ref = refs[0]
    a_refs = refs[1:1 + n_in]
    w_refs = refs[1 + n_in:1 + 2 * n_in]
    mod_ref, nw_ref, wg_ref, wu_ref, wd_ref, o_ref = refs[1 + 2 * n_in:]
    mod = mod_ref[0]
    nw = nw_ref[...]
    tm = x_ref.shape[0]
    halves = [slice(0, tm // 2), slice(tm // 2, tm)]
    ys = []
    for rows in halves:
        y = _mm(a_refs[0][rows, :].astype(BF16), w_refs[0][...])
        for a_ref, w_ref in zip(a_refs[1:], w_refs[1:]):
            y = y + _mm(a_ref[rows, :].astype(BF16), w_ref[...])
        ys.append(y)
    xs = [x_ref[rows, :] + mod[0:1] * (_rms(y) * nw[0:1]) for rows, y in zip(halves, ys)]
    hs = [((_rms(x) * nw[1:2]) * (1.0 + mod[2:3]) + mod[1:2]).astype(BF16) for x in xs]
    gs = [_mm(h, wg_ref[...]) for h in hs]
    us = [_mm(h, wu_ref[...]) for h in hs]
    acts = [(_silu(g) * u).astype(BF16) for g, u in zip(gs, us)]
    ys = [_mm(act, wd_ref[...]) for act in acts]
    for rows, x, y in zip(halves, xs, ys):
        o_ref[rows, :] = x + mod[3:4] * (_rms(y) * nw[2:3])


def _out_ffn(x2, acts, ws_bf, mod, nw, wg_bf, wu_bf, wd_bf, seq, tm=512):
    n_tok, d = x2.shape
    per_b = seq // tm
    row = lambda i: (i, 0)
    const = lambda i: (0, 0)
    resident = lambda w: pl.BlockSpec(w.shape, const, pipeline_mode=pl.Buffered(1))
    in_specs = [pl.BlockSpec((tm, d), row)]
    in_specs += [pl.BlockSpec((tm, a.shape[1]), row) for a in acts]
    in_specs += [resident(w) for w in ws_bf]
    in_specs += [pl.BlockSpec((1,) + mod.shape[1:], lambda i: (i // per_b, 0, 0)),
                 pl.BlockSpec(nw.shape, const), resident(wg_bf), resident(wu_bf), resident(wd_bf)]
    return pl.pallas_call(
        functools.partial(_out_ffn_kernel, n_in=len(acts)),
        out_shape=jax.ShapeDtypeStruct((n_tok, d), F32),
        grid=(n_tok // tm,),
        in_specs=in_specs,
        out_specs=pl.BlockSpec((tm, d), row),
        compiler_params=pltpu.CompilerParams(vmem_limit_bytes=VMEM_LIMIT),
        name="out_ffn",
    )(x2, *acts, *ws_bf, mod, nw, wg_bf, wu_bf, wd_bf)


def _gdn_kernel(qkvz_ref, ab_ref, alog_ref, dtb_ref, nw_ref, o_ref,
                qkv_s, gc_s, u_s, w_s, qd_s, kd_s, attn_s, o_s, state_ref, *, tt, nb):
    W = GDN_WIDTH
    D = GDN_HEAD_DIM
    C = CHUNK
    H = GDN_HEADS
    n_c = tt // C

    @pl.when(pl.program_id(1) == 0)
    def _():
        state_ref[...] = jnp.zeros_like(state_ref)

    row, col = _tri_masks(tt)
    same_chunk = (row >> 6) == (col >> 6)
    causal = jnp.logical_and(same_chunk, row >= col)
    strict = jnp.logical_and(same_chunk, row > col)
    tril_bf = causal.astype(BF16)
    mask_bf = same_chunk.astype(BF16)

    units = [(bi, h) for bi in range(nb) for h in range(H)]
    qs, ks, vs, betas, k_betas, decays, e_gs, e_rems = [], [], [], [], [], [], [], []
    for bi in range(nb):
        y = qkvz_ref[bi, :, 0:3 * W]
        for h in range(H):
            q = y[:, h * D:(h + 1) * D]
            k = y[:, W + h * D:W + (h + 1) * D]
            q = q * lax.rsqrt(jnp.sum(q * q, axis=-1, keepdims=True) + NORM_EPS) * (D ** -0.5)
            k = k * lax.rsqrt(jnp.sum(k * k, axis=-1, keepdims=True) + NORM_EPS)
            qkv_s[bi, :, h * D:(h + 1) * D] = q
            qkv_s[bi, :, W + h * D:W + (h + 1) * D] = k
        qkv_s[bi, :, 2 * W:3 * W] = y[:, 2 * W:3 * W]

        ab = ab_ref[bi]
        g = -jnp.exp(alog_ref[...]) * _softplus(ab + dtb_ref[...])
        beta_all = _sigmoid(ab)
        gc = _mm_const_lhs(tril_bf, g)
        gc_s[bi] = gc
        gc_t = gc.T
        g_last = jnp.concatenate(
            [jnp.broadcast_to(gc[c * C + C - 1:c * C + C, :], (C, LANES)) for c in range(n_c)], axis=0)
        e_g = jnp.exp(gc)
        e_rem = jnp.exp(g_last - gc)
        for h in range(H):
            qs.append(qkv_s[bi, :, h * D:(h + 1) * D])
            ks.append(qkv_s[bi, :, W + h * D:W + (h + 1) * D])
            vs.append(qkv_s[bi, :, 2 * W + h * D:2 * W + (h + 1) * D])
            betas.append(beta_all[:, H + h:H + h + 1])
            k_betas.append(ks[-1] * betas[-1])
            decays.append(
                jnp.where(causal, jnp.exp(jnp.minimum(gc[:, h:h + 1] - gc_t[h:h + 1, :], 0.0)), 0.0))
            e_gs.append(e_g[:, h:h + 1])
            e_rems.append(e_rem[:, h:h + 1])

    ms = [jnp.where(strict, _mm_bf(kb, k, NT) * dc, 0.0) for kb, k, dc in zip(k_betas, ks, decays)]
    t_ps = _run(_unit_lower_inverses([_pack_blocks(m) for m in ms], mask_bf))
    for i in range(len(units)):
        rhs = jnp.concatenate([vs[i] * betas[i], k_betas[i] * e_gs[i]], axis=1)
        t_hi, t_lo = _split2(t_ps[i])
        rhs_hi, rhs_lo = _split2(rhs)
        t_hi = _expand_blocks(t_hi, mask_bf)
        uw = _mm(t_hi, rhs_hi) + (_mm(t_hi, rhs_lo) + _mm(_expand_blocks(t_lo, mask_bf), rhs_hi))
        u_s[i] = uw[:, 0:D]
        w_s[i] = uw[:, D:2 * D]
        attn_s[i] = _mm_bf(qs[i], ks[i], NT) * decays[i]
        qd_s[i] = qs[i] * e_gs[i]
        kd_s[i] = ks[i] * e_rems[i]

    for c in range(n_c):
        rows = slice(c * C, (c + 1) * C)
        win = slice((c // 2) * 2 * C, (c // 2 + 1) * 2 * C)
        ss = [state_ref[i] for i in range(len(units))]
        wq = [_mm_bf(jnp.concatenate([w_s[i, rows, :], qd_s[i, rows, :]], axis=0), s) for i, s in enumerate(ss)]
        v_new = [u_s[i, rows, :] - x[0:C] for i, x in enumerate(wq)]
        kv = [_mm_bf(kd_s[i, rows, :], x, TN) for i, x in enumerate(v_new)]
        for i, (bi, h) in enumerate(units):
            gl = jnp.exp(gc_s[bi, c * C + C - 1:c * C + C, h:h + 1])
            state_ref[i] = ss[i] * gl + kv[i]
            vv = jnp.concatenate([v_new[i], v_new[i]], axis=0)
            o_s[bi, rows, h * D:(h + 1) * D] = wq[i][C:2 * C] + _mm_bf(attn_s[i, rows, win], vv)

    nw = nw_ref[...]
    for bi, h in units:
        z = qkvz_ref[bi, :, 3 * W + h * D:3 * W + (h + 1) * D]
        o_ref[bi, :, h * D:(h + 1) * D] = _rms(o_s[bi, :, h * D:(h + 1) * D]) * nw * _silu(z)


def _gdn(qkvz, ab, a_log_row, dt_row, norm_w, bsz, seq, tt=256, nb=2):
    W = GDN_WIDTH
    tile = lambda b, j: (b, j, 0)
    const = lambda b, j: (0, 0)
    n_u = nb * GDN_HEADS
    out = pl.pallas_call(
        functools.partial(_gdn_kernel, tt=tt, nb=nb),
        out_shape=jax.ShapeDtypeStruct((bsz, seq, W), F32),
        grid=(bsz // nb, seq // tt),
        in_specs=[
            pl.BlockSpec((nb, tt, 4 * W), tile),
            pl.BlockSpec((nb, tt, LANES), tile),
            pl.BlockSpec((1, LANES), const),
            pl.BlockSpec((1, LANES), const),
            pl.BlockSpec((1, GDN_HEAD_DIM), const),
        ],
        out_specs=pl.BlockSpec((nb, tt, W), tile),
        scratch_shapes=[
            pltpu.VMEM((nb, tt, 3 * W), F32),
            pltpu.VMEM((nb, tt, LANES), F32),
            pltpu.VMEM((n_u, tt, GDN_HEAD_DIM), F32),
            pltpu.VMEM((n_u, tt, GDN_HEAD_DIM), F32),
            pltpu.VMEM((n_u, tt, GDN_HEAD_DIM), F32),
            pltpu.VMEM((n_u, tt, GDN_HEAD_DIM), F32),
            pltpu.VMEM((n_u, tt, tt), F32),
            pltpu.VMEM((nb, tt, W), F32),
            pltpu.VMEM((n_u, GDN_HEAD_DIM, GDN_HEAD_DIM), F32),
        ],
        compiler_params=pltpu.CompilerParams(
            dimension_semantics=("arbitrary", "arbitrary"), vmem_limit_bytes=VMEM_LIMIT),
        name="gdn",
    )(qkvz.reshape(bsz, seq, 4 * W), ab.reshape(bsz, seq, LANES), a_log_row, dt_row, norm_w)
    return out.reshape(bsz * seq, W)


def _rwkv_kernel(rw_ref, w0_ref, w2_ref, a0_ref, a2_ref, g2_ref, kk_ref, ka_ref, rk_ref,
                 lnw_ref, lnb_ref, o_ref,
                 r_s, k_s, b_s, kk_s, kend_s, bend_s, v_s, y_s, u_s, w_s, rv_s, bonus_s, gate_s,
                 lct_s, arb_s, state_ref, *, tt, nb):
    W = RWKV_WIDTH
    C = CHUNK
    NH = RWKV_HEADS
    P2 = 2 * RWKV_HEAD_DIM
    n_c = tt // C
    n_p = NH // 2

    @pl.when(pl.program_id(1) == 0)
    def _():
        state_ref[...] = jnp.zeros_like(state_ref)

    hrow, hcol = _tri_masks(MXU_DIM)
    head_ones = ((hrow >> 6) == (hcol >> 6)).astype(BF16)

    def head_sums(x):
        return jnp.concatenate(
            [_mm_const_rhs(x[:, o:o + MXU_DIM], head_ones) for o in range(0, W, MXU_DIM)], axis=1)

    row, col = _tri_masks(tt)
    tril_bf = jnp.logical_and((row >> 6) == (col >> 6), row >= col).astype(BF16)

    def prepass(bi):
        cf = rw_ref[bi]

        r = cf[:, 0:W]
        k = cf[:, W:2 * W]
        v = cf[:, 2 * W:3 * W]
        wd_ad = cf[:, 3 * W:3 * W + LANES]
        gd = cf[:, 3 * W + LANES:3 * W + 2 * LANES]

        w_in = w0_ref[...] + _mm_bf(jnp.tanh(wd_ad), w2_ref[...])
        a = _sigmoid(a0_ref[...] + _mm_bf(wd_ad, a2_ref[...]))
        yield
        gate_s[bi] = _mm_bf(_sigmoid(gd), g2_ref[...])
        kkv = k * kk_ref[...]
        yield
        kk = kkv * lax.rsqrt(head_sums(kkv * kkv) + NORM_EPS)
        kmod = k * (1.0 + (a - 1.0) * ka_ref[...])
        yield
        bonus_s[bi] = head_sums(r * kmod * rk_ref[...]) * v
        lw = (-math.exp(-0.5) * LOG2_E) * _sigmoid(w_in)
        yield
        lc = _mm_const_lhs(tril_bf, lw)
        lct_s[bi] = lc.T
        l_last = jnp.concatenate(
            [jnp.broadcast_to(lc[c * C + C - 1:c * C + C, :], (C, W)) for c in range(n_c)], axis=0)
        yield
        e_inv = jnp.exp2(-lc)
        e_rem = jnp.exp2(l_last - lc)
        b = kk * a
        r_s[bi] = (r * jnp.exp2(lc)).astype(BF16)
        yield
        k_s[bi] = (kmod * e_inv).astype(BF16)
        b_s[bi] = (b * e_inv).astype(BF16)
        yield
        kk_s[bi] = kk * jnp.exp2(lc - lw)
        kend_s[bi] = (kmod * e_rem).astype(BF16)
        yield
        bend_s[bi] = (b * e_rem).astype(BF16)
        v_s[bi] = v.astype(BF16)

    lane = lax.broadcasted_iota(jnp.int32, (1, P2), 1)
    first = lane < RWKV_HEAD_DIM
    rows_of = lambda c: slice(c * C, (c + 1) * C)
    lanes_of = lambda p: slice(p * P2, (p + 1) * P2)
    cut = lambda ref, u: ref[u[0], rows_of(u[2]), lanes_of(u[1])]
    slot = lambda u: (u[0] * n_p + u[1]) * n_c + u[2]

    def per_head_rows(x):
        zero = jnp.zeros_like(x)
        return jnp.concatenate([jnp.where(first, x, zero), jnp.where(first, zero, x)], axis=0)

    def chunk_form(seqs):
        prow = lax.broadcasted_iota(jnp.int32, (C, 2 * P2), 0)
        pcol = lax.broadcasted_iota(jnp.int32, (C, 2 * P2), 1) & (C - 1)
        strict_p = prow > pcol
        causal_p = prow >= pcol
        units = [(bi, p, c) for bi in seqs for p in range(n_p) for c in range(n_c)]

        prod = [_mm(jnp.concatenate([cut(kk_s, u).astype(BF16), cut(r_s, u)], axis=0),
                    jnp.concatenate([per_head_rows(cut(b_s, u)), per_head_rows(cut(k_s, u))], axis=0), NT)
                for u in units]
        kbkk = [jnp.where(strict_p, x[0:C], 0.0) for x in prod]
        rbrk = [jnp.where(causal_p, x[C:2 * C], 0.0) for x in prod]
        for u, x in zip(units, rbrk):
            arb_s[slot(u)] = x[:, 0:P2]
        yield
        avrv = [_mm(jnp.concatenate([x[:, P2:2 * P2], y[:, P2:2 * P2]], axis=0).astype(BF16),
                    per_head_rows(cut(v_s, u))) for x, y, u in zip(kbkk, rbrk, units)]
        for u, x in zip(units, avrv):
            rv_s[u[0], rows_of(u[2]), lanes_of(u[1])] = x[C:2 * C]
        yield
        prow2, pcol2 = _tri_masks(2 * P2)
        t_p = yield from _unit_lower_inverses(
            [jnp.concatenate([kbkk[i][:, 0:P2], kbkk[i + 1][:, 0:P2]], axis=1) for i in range(0, len(units), 2)],
            ((prow2 >> 6) == (pcol2 >> 6)).astype(BF16))
        first2 = jnp.concatenate([first, first], axis=1)

        def per_head_rows2(x):
            zero = jnp.zeros_like(x)
            return jnp.concatenate([jnp.where(first2, x, zero), jnp.where(first2, zero, x)], axis=0)

        for i, u in enumerate(units):
            t_hi, t_lo = _split2(t_p[i // 2][:, (i % 2) * P2:(i % 2 + 1) * P2])
            x_hi, x_lo = _split2(jnp.concatenate([avrv[i][0:C], cut(kk_s, u)], axis=1))
            x_hi = per_head_rows2(x_hi)
            uw = _mm(t_hi, x_hi) + (_mm(t_hi, per_head_rows2(x_lo)) + _mm(t_lo, x_hi))
            u_s[u[0], rows_of(u[2]), lanes_of(u[1])] = uw[:, 0:P2]
            w_s[u[0], rows_of(u[2]), lanes_of(u[1])] = uw[:, P2:2 * P2]

    def recurrence(seqs):
        brow, bcol = _tri_masks(P2)
        blockdiag = (brow >> 6) == (bcol >> 6)
        pairs = [(bi, p) for bi in seqs for p in range(n_p)]
        for c in range(n_c):
            us = [(bi, p, c) for bi, p in pairs]
            ss = [state_ref[bi * n_p + p] for bi, p in pairs]
            rw = [_mm(jnp.concatenate([cut(r_s, u), cut(w_s, u).astype(BF16)], axis=0), s.astype(BF16))
                  for u, s in zip(us, ss)]
            pm = [-(x[C:2 * C] + cut(u_s, u)) for u, x in zip(us, rw)]
            upd = [_mm(jnp.concatenate([cut(bend_s, u), cut(kend_s, u)], axis=0),
                       jnp.concatenate([x.astype(BF16), cut(v_s, u)], axis=0), TN) for u, x in zip(us, pm)]
            for i, (bi, p) in enumerate(pairs):
                g_col = jnp.exp2(lct_s[bi, p * P2:(p + 1) * P2, c * C + C - 1:c * C + C])
                state_ref[bi * n_p + p] = ss[i] * g_col + jnp.where(blockdiag, upd[i], 0.0)
                y_s[bi, rows_of(c), lanes_of(p)] = rw[i][0:C] + cut(rv_s, us[i]) + _mm_bf(
                    arb_s[slot(us[i])], per_head_rows(pm[i]))

    _run(prepass(0))
    for bi in range(nb):
        _interleave(chunk_form([bi]), *([prepass(bi + 1)] if bi + 1 < nb else []))
    recurrence(list(range(nb)))

    inv_n = 1.0 / RWKV_HEAD_DIM
    for bi in range(nb):
        y = y_s[bi]
        mean = head_sums(y) * inv_n
        d = y - mean
        var = head_sums(d * d) * inv_n
        yn = d * lax.rsqrt(var + RWKV_GN_EPS) * lnw_ref[...] + lnb_ref[...]
        o_ref[bi] = (yn + bonus_s[bi]) * gate_s[bi]


def _rwkv(rw, w0, w2p, a0, a2p, g2, k_k, k_a, r_k, ln_w, ln_b, bsz, seq, tt=256, nb=2):
    cols = rw.shape[1]
    W = RWKV_WIDTH
    tile = lambda b, j: (b, j, 0)
    const = lambda b, j: (0, 0)
    vec = pl.BlockSpec((1, W), const)
    out = pl.pallas_call(
        functools.partial(_rwkv_kernel, tt=tt, nb=nb),
        out_shape=jax.ShapeDtypeStruct((bsz, seq, W), F32),
        grid=(bsz // nb, seq // tt),
        in_specs=[
            pl.BlockSpec((nb, tt, cols), tile),
            vec,
            pl.BlockSpec(w2p.shape, const),
            vec,
            pl.BlockSpec(a2p.shape, const),
            pl.BlockSpec(g2.shape, const),
            vec, vec, vec, vec, vec,
        ],
        out_specs=pl.BlockSpec((nb, tt, W), tile),
        scratch_shapes=[pltpu.VMEM((nb, tt, W), dt) for dt in (BF16, BF16, BF16, F32, BF16, BF16, BF16) + (F32,) * 6]
        + [pltpu.VMEM((nb, W, tt), F32),
           pltpu.VMEM((nb * (RWKV_HEADS // 2) * (tt // CHUNK), CHUNK, 2 * RWKV_HEAD_DIM), F32),
           pltpu.VMEM((nb * RWKV_HEADS // 2, 2 * RWKV_HEAD_DIM, 2 * RWKV_HEAD_DIM), F32)],
        compiler_params=pltpu.CompilerParams(
            dimension_semantics=("arbitrary", "arbitrary"), vmem_limit_bytes=VMEM_LIMIT),
        name="rwkv7",
    )(rw.reshape(bsz, seq, cols), w0, w2p, a0, a2p, g2, k_k, k_a, r_k, ln_w, ln_b)
    return out.reshape(bsz * seq, W)


def _rglru_kernel(gb_ref, xb_ref, wa_ref, ba_ref, wx_ref, bx_ref, lam_ref, o_ref, h_ref, *, tt):
    @pl.when(pl.program_id(1) == 0)
    def _():
        h_ref[...] = jnp.zeros_like(h_ref)

    xc = xb_ref[...]

    blk = xc.shape[1] // LRU_BLOCKS
    xc_bf = xc.astype(BF16)
    ra = jnp.concatenate(
        [_mm(xc_bf[:, n * blk:(n + 1) * blk], wa_ref[n]) for n in range(LRU_BLOCKS)], axis=1)
    ix = jnp.concatenate(
        [_mm(xc_bf[:, n * blk:(n + 1) * blk], wx_ref[n]) for n in range(LRU_BLOCKS)], axis=1)
    r = _sigmoid(ra + ba_ref[...])
    i = _sigmoid(ix + bx_ref[...])
    log_a = -LRU_C * r * _softplus(-lam_ref[...])
    a = jnp.exp(log_a)
    u = xc * i * jnp.sqrt(1.0 - jnp.exp(2.0 * log_a))

    width = xc.shape[1]
    a = a.reshape(tt // SUB, SUB, width)
    u = u.reshape(tt // SUB, SUB, width)
    sub = lax.broadcasted_iota(jnp.int32, (1, SUB, 1), 1)
    d = 1
    while d < SUB:
        keep = sub >= d
        a_sh = jnp.where(keep, pltpu.roll(a, d, axis=1), 1.0)
        u_sh = jnp.where(keep, pltpu.roll(u, d, axis=1), 0.0)
        u = a * u_sh + u
        a = a * a_sh
        d *= 2
    gate = _gelu_tanh(gb_ref[...])
    carry = h_ref[...]
    for g in range(tt // SUB):
        rows = slice(g * SUB, (g + 1) * SUB)
        h = u[g] + a[g] * carry
        o_ref[rows, :] = h * gate[rows]
        carry = jnp.broadcast_to(h[SUB - 1:SUB, :], h.shape)
    h_ref[...] = carry


def _rglru(gb, xb, wa_bf, ba, wx_bf, bx, lam, bsz, seq, tt=256):
    n_tok, width = xb.shape
    per_b = seq // tt
    row = lambda b, j: (b * per_b + j, 0)
    const = lambda b, j: (0, 0)
    const3 = lambda b, j: (0, 0, 0)
    vec = pl.BlockSpec((1, width), const)
    return pl.pallas_call(
        functools.partial(_rglru_kernel, tt=tt),
        out_shape=jax.ShapeDtypeStruct((n_tok, width), F32),
        grid=(bsz, per_b),
        in_specs=[
            pl.BlockSpec((tt, width), row),
            pl.BlockSpec((tt, width), row),
            pl.BlockSpec(wa_bf.shape, const3),
            vec,
            pl.BlockSpec(wx_bf.shape, const3),
            vec,
            vec,
        ],
        out_specs=pl.BlockSpec((tt, width), row),
        scratch_shapes=[pltpu.VMEM((HALO, width), F32)],
        compiler_params=pltpu.CompilerParams(
            dimension_semantics=("arbitrary", "arbitrary"), vmem_limit_bytes=VMEM_LIMIT),
        name="rglru",
    )(gb, xb, wa_bf, ba, wx_bf, bx, lam)


def _pad_lanes(v):
    return jnp.pad(v, (0, LANES - v.shape[0])).reshape(1, LANES)


def kernel(x, c, norm_pre, norm_post, ada_w, ada_b, ffn_w_gate, ffn_w_up, ffn_w_down, mix_w_in, mix_w_out, gdn_conv_w, gdn_a_log, gdn_dt_bias, gdn_norm_w, rwkv_mu, rwkv_w0, rwkv_w2, rwkv_a0, rwkv_a2, rwkv_g2, rwkv_k_k, rwkv_k_a, rwkv_r_k, rwkv_ln_w, rwkv_ln_b, lru_w_in, lru_conv_w, lru_conv_b, lru_wa, lru_ba, lru_wx, lru_bx, lru_lambda, lru_w_out):
    bsz, seq, d = x.shape
    depth = norm_pre.shape[0]
    x2 = x.reshape(bsz * seq, d)
    mods = _ada_params(c, ada_w, ada_b)

    def mod(layer, sub):
        m = mods[layer * 2 + sub]
        return (m[:, None, 0:d], m[:, None, d:2 * d], m[:, None, 2 * d:3 * d])

    GW = GDN_WIDTH
    for layer in range(depth):
        j = layer // 2
        shift, scale, gate = mod(layer, 0)
        shift2, scale2, gate2 = mod(layer, 1)
        mod_rows = jnp.concatenate([gate, shift2, scale2, gate2], axis=1)
        nw_rows = jnp.stack([norm_post[layer, 0], norm_pre[layer, 1], norm_post[layer, 1]])
        nw_pre = norm_pre[layer, 0].reshape(1, d)
        ffn_w = (ffn_w_gate[layer].astype(BF16), ffn_w_up[layer].astype(BF16), ffn_w_down[layer].astype(BF16))
        if layer % 2 == 0:
            w_in = mix_w_in[j].astype(BF16)
            n_gdn = 4 * GW + 2 * GDN_HEADS
            w_cat = jnp.concatenate(
                [w_in[:, 0:4 * GW], w_in[:, n_gdn:],
                 jnp.pad(w_in[:, 4 * GW:n_gdn], ((0, 0), (0, LANES - 2 * GDN_HEADS)))], axis=1)
            qkvz, rw, ab = _norm_proj(
                x2, nw_pre, shift, scale, w_cat, (4 * GW, RWKV_COLS, LANES),
                (("conv_silu", 3 * GW), ("shift_mix",), ("plain",)),
                (gdn_conv_w[j], rwkv_mu[j].reshape(1, RWKV_COLS)), seq)
            out_a = _gdn(qkvz, ab, _pad_lanes(gdn_a_log[j]), _pad_lanes(gdn_dt_bias[j]),
                         gdn_norm_w[j].reshape(1, GDN_HEAD_DIM), bsz, seq)
            w2p = jnp.pad(rwkv_w2[j], ((0, LANES - DECAY_LORA), (0, 0))).astype(BF16)
            a2p = jnp.pad(rwkv_a2[j], ((DECAY_LORA, LANES - DECAY_LORA - AAA_LORA), (0, 0))).astype(BF16)
            vec = lambda t: t.reshape(1, RWKV_WIDTH)
            out_b = _rwkv(rw, vec(rwkv_w0[j]), w2p, vec(rwkv_a0[j]), a2p,
                          rwkv_g2[j].astype(BF16), vec(rwkv_k_k[j]), vec(rwkv_k_a[j]), vec(rwkv_r_k[j]),
                          vec(rwkv_ln_w[j]), vec(rwkv_ln_b[j]), bsz, seq)
            w_out = mix_w_out[j].astype(BF16)
            x2 = _out_ffn(x2, [out_a, out_b], [w_out[0:GW], w_out[GW:]], mod_rows, nw_rows, *ffn_w, seq)
        else:
            width = lru_w_in.shape[2] // 2
            vec = lambda t: t.reshape(1, width)
            gb, xb = _norm_proj(x2, nw_pre, shift, scale, lru_w_in[j].astype(BF16), (width, width),
                                (("plain",), ("conv_bias",)), (lru_conv_w[j], vec(lru_conv_b[j])), seq)
            y = _rglru(gb, xb, lru_wa[j].astype(BF16), vec(lru_ba[j]),
                       lru_wx[j].astype(BF16), vec(lru_bx[j]), vec(lru_lambda[j]), bsz, seq)
            x2 = _out_ffn(x2, [y], [lru_w_out[j].astype(BF16)], mod_rows, nw_rows, *ffn_w, seq)
    return x2.reshape(bsz, seq, d)
```

```python
import functools
import math

import jax
import jax.numpy as jnp
from jax import lax
from jax.experimental import pallas as pl
from jax.experimental.pallas import tpu as pltpu

F32 = jnp.float32
BF16 = jnp.bfloat16

NORM_EPS = 1e-6
LOG2_E = 1.4426950408889634
GDN_HEADS = 4
GDN_HEAD_DIM = 128
GDN_WIDTH = GDN_HEADS * GDN_HEAD_DIM
CHUNK = 64
CONV_WIDTH = 4
RWKV_HEADS = 8
RWKV_HEAD_DIM = 64
RWKV_WIDTH = RWKV_HEADS * RWKV_HEAD_DIM
DECAY_LORA = 64
AAA_LORA = 64
GATE_LORA = 128
RWKV_COLS = 3 * RWKV_WIDTH + DECAY_LORA + AAA_LORA + GATE_LORA
RWKV_GN_EPS = 64e-5
LRU_BLOCKS = 4
LRU_C = 8.0
LANES = 128
MXU_DIM = 256
FFN_PART_ROWS = 256
SUB = 8
HALO = SUB
VMEM_LIMIT = 56 * 1024 * 1024

NN = (((1,), (0,)), ((), ()))
NT = (((1,), (1,)), ((), ()))
TN = (((0,), (0,)), ((), ()))


def _mm(a, b, dims=NN):
    return lax.dot_general(a, b, dims, preferred_element_type=F32)


def _mm_bf(a, b, dims=NN):
    return _mm(a.astype(BF16), b.astype(BF16), dims)


def _split2(x):
    hi = x.astype(BF16)
    lo = (x - hi.astype(F32)).astype(BF16)
    return hi, lo


def _mm_x3(a, b, dims=NN):
    ah, al = _split2(a)
    bh, bl = _split2(b)
    return _mm(ah, bh, dims) + (_mm(ah, bl, dims) + _mm(al, bh, dims))


def _mm_const_lhs(c_bf, x):
    hi = x.astype(BF16)
    r1 = x - hi.astype(F32)
    mid = r1.astype(BF16)
    lo = (r1 - mid.astype(F32)).astype(BF16)
    return _mm(c_bf, hi) + (_mm(c_bf, mid) + _mm(c_bf, lo))


def _mm_const_rhs(x, c_bf):
    hi, lo = _split2(x)
    return _mm(hi, c_bf) + _mm(lo, c_bf)


def _sigmoid(x):
    return 1.0 / (1.0 + jnp.exp(-x))


def _silu(x):
    return x * _sigmoid(x)


def _softplus(x):
    return jnp.maximum(x, 0.0) + jnp.log1p(jnp.exp(-jnp.abs(x)))


def _gelu_tanh(x):
    return 0.5 * x * (1.0 + jnp.tanh(0.7978845608028654 * (x + 0.044715 * (x * x * x))))


def _rms(x):
    return x * lax.rsqrt(jnp.mean(x * x, axis=-1, keepdims=True) + NORM_EPS)


def _tri_masks(n):
    row = lax.broadcasted_iota(jnp.int32, (n, n), 0)
    col = lax.broadcasted_iota(jnp.int32, (n, n), 1)
    return row, col


def _pack_blocks(x):
    out = x[0:CHUNK]
    for c in range(1, x.shape[0] // CHUNK):
        out = out + x[c * CHUNK:(c + 1) * CHUNK]
    return out


def _expand_blocks(xp_bf, mask_bf):
    return jnp.concatenate([xp_bf] * (xp_bf.shape[1] // CHUNK), axis=0) * mask_bf


def _unit_lower_inverses(ms_p, mask_bf):
    n = ms_p[0].shape[1]
    prow = lax.broadcasted_iota(jnp.int32, (CHUNK, n), 0)
    pcol = lax.broadcasted_iota(jnp.int32, (CHUNK, n), 1) & (CHUNK - 1)
    eye_p = (prow == pcol).astype(F32)
    expand = lambda xp: _expand_blocks(xp.astype(BF16), mask_bf)
    mm = lambda xp, y_bf: _mm(xp.astype(BF16), y_bf)
    same = (prow >> 3) == (pcol >> 3)
    m8_p = [jnp.where(same, mp, 0.0) for mp in ms_p]
    m2_p = [mm(ap, expand(ap)) for ap in m8_p]
    yield
    m4_p = [mm(ap, expand(ap)) for ap in m2_p]
    yield
    inv_p = [mm(eye_p - ap, expand(eye_p + bp)) for ap, bp in zip(m8_p, m2_p)]
    yield
    inv_p = [mm(ap, expand(eye_p + bp)) for ap, bp in zip(inv_p, m4_p)]
    yield
    shift = 3
    while (1 << shift) < CHUNK:
        pair = (prow >> (shift + 1)) == (pcol >> (shift + 1))
        off_mask = jnp.logical_and(pair, jnp.logical_not(same))
        tmp_p = [mm(ap, expand(jnp.where(off_mask, mp, 0.0))) for ap, mp in zip(inv_p, ms_p)]
        yield
        inv_p = [ap - mm(tp, expand(ap)) for ap, tp in zip(inv_p, tmp_p)]
        yield
        same = pair
        shift += 1
    a_split = [_split2(ap) for ap in inv_p]
    m_split = [_split2(mp) for mp in ms_p]
    e_hi = [_expand_blocks(a_hi, mask_bf) for a_hi, _ in a_split]
    e_lo = [_expand_blocks(a_lo, mask_bf) for _, a_lo in a_split]
    prod = [_mm(m_hi, eh) + (_mm(m_hi, el) + _mm(m_lo, eh)) for (m_hi, m_lo), eh, el in zip(m_split, e_hi, e_lo)]
    yield
    res = [expand(eye_p - (ap + pr)) for ap, pr in zip(inv_p, prod)]
    return [ap + _mm(a_hi, r) for ap, (a_hi, _), r in zip(inv_p, a_split, res)]


def _run(gen):
    try:
        while True:
            next(gen)
    except StopIteration as stop:
        return stop.value


def _interleave(*gens):
    live = list(gens)
    while live:
        for g in list(live):
            try:
                next(g)
            except StopIteration:
                live.remove(g)


def _ada_kernel(c_ref, w_ref, b_ref, o_ref):
    s = _silu(c_ref[...])
    o_ref[0] = _mm_x3(s, w_ref[0]) + b_ref[0]


def _ada_params(c, ada_w, ada_b):
    n_l, n_s, d, d3 = ada_w.shape
    n = n_l * n_s
    bsz = c.shape[0]
    tn = 1024
    return pl.pallas_call(
        _ada_kernel,
        out_shape=jax.ShapeDtypeStruct((n, bsz, d3), F32),
        grid=(n, d3 // tn),
        in_specs=[
            pl.BlockSpec((bsz, d), lambda i, j: (0, 0)),
            pl.BlockSpec((1, d, tn), lambda i, j: (i, 0, j)),
            pl.BlockSpec((1, 1, tn), lambda i, j: (i, 0, j)),
        ],
        out_specs=pl.BlockSpec((1, bsz, tn), lambda i, j: (i, 0, j)),
        compiler_params=pltpu.CompilerParams(vmem_limit_bytes=VMEM_LIMIT),
        name="ada_params",
    )(c, ada_w.reshape(n, d, d3), ada_b.reshape(n, 1, d3))


def _causal_conv(x, hist, cw):
    n = x.shape[0]
    xp = jnp.concatenate([hist, x], axis=0)
    y = x * cw[CONV_WIDTH - 1:CONV_WIDTH]
    for j in range(CONV_WIDTH - 1):
        o = HALO - (CONV_WIDTH - 1) + j
        y = y + xp[o:o + n] * cw[j:j + 1]
    return y


def _norm_proj_kernel(x_ref, nw_ref, shift_ref, scale_ref, w_ref, *refs, post, per_b):
    n_par = sum({"plain": 0, "conv_silu": 1, "conv_bias": 2, "shift_mix": 1}[p[0]] for p in post)
    par_refs = list(refs[:n_par])
    out_refs = refs[n_par:n_par + len(post)]
    hist_refs = list(refs[n_par + len(post):])

    @pl.when(pl.program_id(0) % per_b == 0)
    def _():
        for h_ref in hist_refs:
            h_ref[...] = jnp.zeros_like(h_ref)

    tm = x_ref.shape[0]
    halves = [slice(0, tm // 2), slice(tm // 2, tm)]
    hs = [((_rms(x_ref[rows, :]) * nw_ref[...]) * (1.0 + scale_ref[0]) + shift_ref[0]).astype(BF16)
          for rows in halves]
    off = 0
    for o_ref, p in zip(out_refs, post):
        n = o_ref.shape[1]
        ys = [_mm(h, w_ref[:, off:off + n]) for h in hs]
        off += n
        if p[0] == "plain":
            outs = ys
        else:
            h_ref = hist_refs.pop(0)
            hist = h_ref[...]
            h_ref[...] = ys[-1][tm // 2 - HALO:tm // 2]
            hists = [hist, ys[0][tm // 2 - HALO:tm // 2]]
            if p[0] == "conv_silu":
                cw = par_refs.pop(0)[...]
                outs = [jnp.concatenate([_silu(_causal_conv(y[:, 0:p[1]], hi[:, 0:p[1]], cw)), y[:, p[1]:n]], axis=1)
                        for y, hi in zip(ys, hists)]
            elif p[0] == "conv_bias":
                cw = par_refs.pop(0)[...]
                cb = par_refs.pop(0)[...]
                outs = [_causal_conv(y, hi, cw) + cb for y, hi in zip(ys, hists)]
            else:
                mu = par_refs.pop(0)[...]
                trow = lax.broadcasted_iota(jnp.int32, (tm // 2, 1), 0)
                outs = [y + mu * (jnp.where(trow == 0, hi[HALO - 1:HALO, :], pltpu.roll(y, 1, axis=0)) - y)
                        for y, hi in zip(ys, hists)]
        for rows, out in zip(halves, outs):
            o_ref[rows, :] = out


def _norm_proj(x2, nw, shift, scale, w_bf, splits, post, params, seq, tm=512):
    n_tok, d = x2.shape
    per_b = seq // tm
    const = lambda i: (0, 0)
    return pl.pallas_call(
        functools.partial(_norm_proj_kernel, post=post, per_b=per_b),
        out_shape=[jax.ShapeDtypeStruct((n_tok, n), F32) for n in splits],
        grid=(n_tok // tm,),
        in_specs=[
            pl.BlockSpec((tm, d), lambda i: (i, 0)),
            pl.BlockSpec((1, d), const),
            pl.BlockSpec((1, 1, d), lambda i: (i // per_b, 0, 0)),
            pl.BlockSpec((1, 1, d), lambda i: (i // per_b, 0, 0)),
            pl.BlockSpec(w_bf.shape, const, pipeline_mode=pl.Buffered(1)),
        ] + [pl.BlockSpec(p.shape, const) for p in params],
        out_specs=[pl.BlockSpec((tm, n), lambda i: (i, 0)) for n in splits],
        scratch_shapes=[pltpu.VMEM((HALO, n), F32) for n, p in zip(splits, post) if p[0] != "plain"],
        compiler_params=pltpu.CompilerParams(
            dimension_semantics=("arbitrary",), vmem_limit_bytes=VMEM_LIMIT),
        name="norm_proj",
    )(x2, nw, shift, scale, w_bf, *params)


def _out_ffn_kernel(*refs, n_in):
    x_ref = refs[0]
    a_refs = refs[1:1 + n_in]
    w_refs = refs[1 + n_in:1 + 2 * n_in]
    mod_ref, nw_ref, wg_ref, wu_ref, wd_ref, o_ref = refs[1 + 2 * n_in:]
    mod = mod_ref[0]
    nw = nw_ref[...]
    tm = x_ref.shape[0]
    halves = [slice(r, r + FFN_PART_ROWS) for r in range(0, tm, FFN_PART_ROWS)]
    ys = []
    for rows in halves:
        y = _mm(a_refs[0][rows, :].astype(BF16), w_refs[0][...])
        for a_ref, w_ref in zip(a_refs[1:], w_refs[1:]):
            y = y + _mm(a_ref[rows, :].astype(BF16), w_ref[...])
        ys.append(y)
    xs = [x_ref[rows, :] + mod[0:1] * (_rms(y) * nw[0:1]) for rows, y in zip(halves, ys)]
    hs = [((_rms(x) * nw[1:2]) * (1.0 + mod[2:3]) + mod[1:2]).astype(BF16) for x in xs]
    gs = [_mm(h, wg_ref[...]) for h in hs]
    us = [_mm(h, wu_ref[...]) for h in hs]
    acts = [(_silu(g) * u).astype(BF16) for g, u in zip(gs, us)]
    ys = [_mm(act, wd_ref[...]) for act in acts]
    for rows, x, y in zip(halves, xs, ys):
        o_ref[rows, :] = x + mod[3:4] * (_rms(y) * nw[2:3])


def _out_ffn(x2, acts, ws_bf, mod, nw, wg_bf, wu_bf, wd_bf, seq, tm=1024):
    n_tok, d = x2.shape
    per_b = seq // tm
    row = lambda i: (i, 0)
    const = lambda i: (0, 0)
    resident = lambda w: pl.BlockSpec(w.shape, const, pipeline_mode=pl.Buffered(1))
    in_specs = [pl.BlockSpec((tm, d), row)]
    in_specs += [pl.BlockSpec((tm, a.shape[1]), row) for a in acts]
    in_specs += [resident(w) for w in ws_bf]
    in_specs += [pl.BlockSpec((1,) + mod.shape[1:], lambda i: (i // per_b, 0, 0)),
                 pl.BlockSpec(nw.shape, const), resident(wg_bf), resident(wu_bf), resident(wd_bf)]
    return pl.pallas_call(
        functools.partial(_out_ffn_kernel, n_in=len(acts)),
        out_shape=jax.ShapeDtypeStruct((n_tok, d), F32),
        grid=(n_tok // tm,),
        in_specs=in_specs,
        out_specs=pl.BlockSpec((tm, d), row),
        compiler_params=pltpu.CompilerParams(vmem_limit_bytes=VMEM_LIMIT),
        name="out_ffn",
    )(x2, *acts, *ws_bf, mod, nw, wg_bf, wu_bf, wd_bf)


def _gdn_kernel(qkvz_ref, ab_ref, alog_ref, dtb_ref, nw_ref, o_ref,
                qkv_s, gc_s, u_s, w_s, qd_s, kd_s, attn_s, o_s, state_ref, *, tt, nb):
    W = GDN_WIDTH
    D = GDN_HEAD_DIM
    C = CHUNK
    H = GDN_HEADS
    n_c = tt // C

    @pl.when(pl.program_id(1) == 0)
    def _():
        state_ref[...] = jnp.zeros_like(state_ref)

    row, col = _tri_masks(tt)
    same_chunk = (row >> 6) == (col >> 6)
    causal = jnp.logical_and(same_chunk, row >= col)
    strict = jnp.logical_and(same_chunk, row > col)
    tril_bf = causal.astype(BF16)
    mask_bf = same_chunk.astype(BF16)

    units = [(bi, h) for bi in range(nb) for h in range(H)]
    qs, ks, vs, betas, k_betas, decays, e_gs, e_rems = [], [], [], [], [], [], [], []
    for bi in range(nb):
        y = qkvz_ref[bi, :, 0:3 * W]
        for h in range(H):
            q = y[:, h * D:(h + 1) * D]
            k = y[:, W + h * D:W + (h + 1) * D]
            q = q * lax.rsqrt(jnp.sum(q * q, axis=-1, keepdims=True) + NORM_EPS) * (D ** -0.5)
            k = k * lax.rsqrt(jnp.sum(k * k, axis=-1, keepdims=True) + NORM_EPS)
            qkv_s[bi, :, h * D:(h + 1) * D] = q
            qkv_s[bi, :, W + h * D:W + (h + 1) * D] = k
        qkv_s[bi, :, 2 * W:3 * W] = y[:, 2 * W:3 * W]

        ab = ab_ref[bi]
        g = -jnp.exp(alog_ref[...]) * _softplus(ab + dtb_ref[...])
        beta_all = _sigmoid(ab)
        gc = _mm_const_lhs(tril_bf, g)
        gc_s[bi] = gc
        gc_t = gc.T
        g_last = jnp.concatenate(
            [jnp.broadcast_to(gc[c * C + C - 1:c * C + C, :], (C, LANES)) for c in range(n_c)], axis=0)
        e_g = jnp.exp(gc)
        e_rem = jnp.exp(g_last - gc)
        for h in range(H):
            qs.append(qkv_s[bi, :, h * D:(h + 1) * D])
            ks.append(qkv_s[bi, :, W + h * D:W + (h + 1) * D])
            vs.append(qkv_s[bi, :, 2 * W + h * D:2 * W + (h + 1) * D])
            betas.append(beta_all[:, H + h:H + h + 1])
            k_betas.append(ks[-1] * betas[-1])
            decays.append(
                jnp.where(causal, jnp.exp(jnp.minimum(gc[:, h:h + 1] - gc_t[h:h + 1, :], 0.0)), 0.0))
            e_gs.append(e_g[:, h:h + 1])
            e_rems.append(e_rem[:, h:h + 1])

    ms = [jnp.where(strict, _mm_bf(kb, k, NT) * dc, 0.0) for kb, k, dc in zip(k_betas, ks, decays)]
    t_ps = _run(_unit_lower_inverses([_pack_blocks(m) for m in ms], mask_bf))
    for i in range(len(units)):
        rhs = jnp.concatenate([vs[i] * betas[i], k_betas[i] * e_gs[i]], axis=1)
        t_hi, t_lo = _split2(t_ps[i])
        rhs_hi, rhs_lo = _split2(rhs)
        t_hi = _expand_blocks(t_hi, mask_bf)
        uw = _mm(t_hi, rhs_hi) + (_mm(t_hi, rhs_lo) + _mm(_expand_blocks(t_lo, mask_bf), rhs_hi))
        u_s[i] = uw[:, 0:D]
        w_s[i] = uw[:, D:2 * D]
        attn_s[i] = _mm_bf(qs[i], ks[i], NT) * decays[i]
        qd_s[i] = qs[i] * e_gs[i]
        kd_s[i] = ks[i] * e_rems[i]

    for c in range(n_c):
        rows = slice(c * C, (c + 1) * C)
        win = slice((c // 2) * 2 * C, (c // 2 + 1) * 2 * C)
        ss = [state_ref[i] for i in range(len(units))]
        wq = [_mm_bf(jnp.concatenate([w_s[i, rows, :], qd_s[i, rows, :]], axis=0), s) for i, s in enumerate(ss)]
        v_new = [u_s[i, rows, :] - x[0:C] for i, x in enumerate(wq)]
        kv = [_mm_bf(kd_s[i, rows, :], x, TN) for i, x in enumerate(v_new)]
        for i, (bi, h) in enumerate(units):
            gl = jnp.exp(gc_s[bi, c * C + C - 1:c * C + C, h:h + 1])
            state_ref[i] = ss[i] * gl + kv[i]
            vv = jnp.concatenate([v_new[i], v_new[i]], axis=0)
            o_s[bi, rows, h * D:(h + 1) * D] = wq[i][C:2 * C] + _mm_bf(attn_s[i, rows, win], vv)

    nw = nw_ref[...]
    for bi, h in units:
        z = qkvz_ref[bi, :, 3 * W + h * D:3 * W + (h + 1) * D]
        o_ref[bi, :, h * D:(h + 1) * D] = (_rms(o_s[bi, :, h * D:(h + 1) * D]) * nw * _silu(z)).astype(o_ref.dtype)


def _gdn(qkvz, ab, a_log_row, dt_row, norm_w, bsz, seq, tt=256, nb=2):
    W = GDN_WIDTH
    tile = lambda b, j: (b, j, 0)
    const = lambda b, j: (0, 0)
    n_u = nb * GDN_HEADS
    out = pl.pallas_call(
        functools.partial(_gdn_kernel, tt=tt, nb=nb),
        out_shape=jax.ShapeDtypeStruct((bsz, seq, W), BF16),
        grid=(bsz // nb, seq // tt),
        in_specs=[
            pl.BlockSpec((nb, tt, 4 * W), tile),
            pl.BlockSpec((nb, tt, LANES), tile),
            pl.BlockSpec((1, LANES), const),
            pl.BlockSpec((1, LANES), const),
            pl.BlockSpec((1, GDN_HEAD_DIM), const),
        ],
        out_specs=pl.BlockSpec((nb, tt, W), tile),
        scratch_shapes=[
            pltpu.VMEM((nb, tt, 3 * W), F32),
            pltpu.VMEM((nb, tt, LANES), F32),
            pltpu.VMEM((n_u, tt, GDN_HEAD_DIM), F32),
            pltpu.VMEM((n_u, tt, GDN_HEAD_DIM), F32),
            pltpu.VMEM((n_u, tt, GDN_HEAD_DIM), F32),
            pltpu.VMEM((n_u, tt, GDN_HEAD_DIM), F32),
            pltpu.VMEM((n_u, tt, tt), F32),
            pltpu.VMEM((nb, tt, W), F32),
            pltpu.VMEM((n_u, GDN_HEAD_DIM, GDN_HEAD_DIM), F32),
        ],
        compiler_params=pltpu.CompilerParams(
            dimension_semantics=("arbitrary", "arbitrary"), vmem_limit_bytes=VMEM_LIMIT),
        name="gdn",
    )(qkvz.reshape(bsz, seq, 4 * W), ab.reshape(bsz, seq, LANES), a_log_row, dt_row, norm_w)
    return out.reshape(bsz * seq, W)


def _rwkv_kernel(rw_ref, w0_ref, w2_ref, a0_ref, a2_ref, g2_ref, kk_ref, ka_ref, rk_ref,
                 lnw_ref, lnb_ref, o_ref,
                 r_s, k_s, b_s, kk_s, kend_s, bend_s, v_s, y_s, u_s, w_s, rv_s, bonus_s, gate_s,
                 lct_s, arb_s, state_ref, *, tt, nb):
    W = RWKV_WIDTH
    C = CHUNK
    NH = RWKV_HEADS
    P2 = 2 * RWKV_HEAD_DIM
    n_c = tt // C
    n_p = NH // 2

    @pl.when(pl.program_id(1) == 0)
    def _():
        state_ref[...] = jnp.zeros_like(state_ref)

    hrow, hcol = _tri_masks(MXU_DIM)
    head_ones = ((hrow >> 6) == (hcol >> 6)).astype(BF16)

    def head_sums(x):
        return jnp.concatenate(
            [_mm_const_rhs(x[:, o:o + MXU_DIM], head_ones) for o in range(0, W, MXU_DIM)], axis=1)

    row, col = _tri_masks(tt)
    tril_bf = jnp.logical_and((row >> 6) == (col >> 6), row >= col).astype(BF16)

    def prepass(bi):
        cf = rw_ref[bi]

        r = cf[:, 0:W]
        k = cf[:, W:2 * W]
        v = cf[:, 2 * W:3 * W]
        wd_ad = cf[:, 3 * W:3 * W + LANES]
        gd = cf[:, 3 * W + LANES:3 * W + 2 * LANES]

        w_in = w0_ref[...] + _mm_bf(jnp.tanh(wd_ad), w2_ref[...])
        a = _sigmoid(a0_ref[...] + _mm_bf(wd_ad, a2_ref[...]))
        yield
        gate_s[bi] = _mm_bf(_sigmoid(gd), g2_ref[...])
        kkv = k * kk_ref[...]
        yield
        kk = kkv * lax.rsqrt(head_sums(kkv * kkv) + NORM_EPS)
        kmod = k * (1.0 + (a - 1.0) * ka_ref[...])
        yield
        bonus_s[bi] = head_sums(r * kmod * rk_ref[...]) * v
        lw = (-math.exp(-0.5) * LOG2_E) * _sigmoid(w_in)
        yield
        lc = _mm_const_lhs(tril_bf, lw)
        lct_s[bi] = lc.T
        l_last = jnp.concatenate(
            [jnp.broadcast_to(lc[c * C + C - 1:c * C + C, :], (C, W)) for c in range(n_c)], axis=0)
        yield
        e_inv = jnp.exp2(-lc)
        e_rem = jnp.exp2(l_last - lc)
        b = kk * a
        r_s[bi] = (r * jnp.exp2(lc)).astype(BF16)
        yield
        k_s[bi] = (kmod * e_inv).astype(BF16)
        b_s[bi] = (b * e_inv).astype(BF16)
        yield
        kk_s[bi] = kk * jnp.exp2(lc - lw)
        kend_s[bi] = (kmod * e_rem).astype(BF16)
        yield
        bend_s[bi] = (b * e_rem).astype(BF16)
        v_s[bi] = v.astype(BF16)

    lane = lax.broadcasted_iota(jnp.int32, (1, P2), 1)
    first = lane < RWKV_HEAD_DIM
    rows_of = lambda c: slice(c * C, (c + 1) * C)
    lanes_of = lambda p: slice(p * P2, (p + 1) * P2)
    cut = lambda ref, u: ref[u[0], rows_of(u[2]), lanes_of(u[1])]
    slot = lambda u: (u[0] * n_p + u[1]) * n_c + u[2]

    def per_head_rows(x):
        zero = jnp.zeros_like(x)
        return jnp.concatenate([jnp.where(first, x, zero), jnp.where(first, zero, x)], axis=0)

    def chunk_form(seqs):
        prow = lax.broadcasted_iota(jnp.int32, (C, 2 * P2), 0)
        pcol = lax.broadcasted_iota(jnp.int32, (C, 2 * P2), 1) & (C - 1)
        strict_p = prow > pcol
        causal_p = prow >= pcol
        units = [(bi, p, c) for bi in seqs for p in range(n_p) for c in range(n_c)]

        prod = [_mm(jnp.concatenate([cut(kk_s, u).astype(BF16), cut(r_s, u)], axis=0),
                    jnp.concatenate([per_head_rows(cut(b_s, u)), per_head_rows(cut(k_s, u))], axis=0), NT)
                for u in units]
        kbkk = [jnp.where(strict_p, x[0:C], 0.0) for x in prod]
        rbrk = [jnp.where(causal_p, x[C:2 * C], 0.0) for x in prod]
        for u, x in zip(units, rbrk):
            arb_s[slot(u)] = x[:, 0:P2]
        yield
        avrv = [_mm(jnp.concatenate([x[:, P2:2 * P2], y[:, P2:2 * P2]], axis=0).astype(BF16),
                    per_head_rows(cut(v_s, u))) for x, y, u in zip(kbkk, rbrk, units)]
        for u, x in zip(units, avrv):
            rv_s[u[0], rows_of(u[2]), lanes_of(u[1])] = x[C:2 * C]
        yield
        prow2, pcol2 = _tri_masks(2 * P2)
        t_p = yield from _unit_lower_inverses(
            [jnp.concatenate([kbkk[i][:, 0:P2], kbkk[i + 1][:, 0:P2]], axis=1) for i in range(0, len(units), 2)],
            ((prow2 >> 6) == (pcol2 >> 6)).astype(BF16))
        first2 = jnp.concatenate([first, first], axis=1)

        def per_head_rows2(x):
            zero = jnp.zeros_like(x)
            return jnp.concatenate([jnp.where(first2, x, zero), jnp.where(first2, zero, x)], axis=0)

        for i, u in enumerate(units):
            t_hi, t_lo = _split2(t_p[i // 2][:, (i % 2) * P2:(i % 2 + 1) * P2])
            x_hi, x_lo = _split2(jnp.concatenate([avrv[i][0:C], cut(kk_s, u)], axis=1))
            x_hi = per_head_rows2(x_hi)
            uw = _mm(t_hi, x_hi) + (_mm(t_hi, per_head_rows2(x_lo)) + _mm(t_lo, x_hi))
            u_s[u[0], rows_of(u[2]), lanes_of(u[1])] = uw[:, 0:P2]
            w_s[u[0], rows_of(u[2]), lanes_of(u[1])] = uw[:, P2:2 * P2]

    def recurrence(seqs):
        brow, bcol = _tri_masks(P2)
        blockdiag = (brow >> 6) == (bcol >> 6)
        pairs = [(bi, p) for bi in seqs for p in range(n_p)]
        for c in range(n_c):
            us = [(bi, p, c) for bi, p in pairs]
            ss = [state_ref[bi * n_p + p] for bi, p in pairs]
            rw = [_mm(jnp.concatenate([cut(r_s, u), cut(w_s, u).astype(BF16)], axis=0), s.astype(BF16))
                  for u, s in zip(us, ss)]
            pm = [-(x[C:2 * C] + cut(u_s, u)) for u, x in zip(us, rw)]
            upd = [_mm(jnp.concatenate([cut(bend_s, u), cut(kend_s, u)], axis=0),
                       jnp.concatenate([x.astype(BF16), cut(v_s, u)], axis=0), TN) for u, x in zip(us, pm)]
            for i, (bi, p) in enumerate(pairs):
                g_col = jnp.exp2(lct_s[bi, p * P2:(p + 1) * P2, c * C + C - 1:c * C + C])
                state_ref[bi * n_p + p] = ss[i] * g_col + jnp.where(blockdiag, upd[i], 0.0)
                y_s[bi, rows_of(c), lanes_of(p)] = rw[i][0:C] + cut(rv_s, us[i]) + _mm_bf(
                    arb_s[slot(us[i])], per_head_rows(pm[i]))

    _run(prepass(0))
    for bi in range(nb):
        _interleave(chunk_form([bi]), *([prepass(bi + 1)] if bi + 1 < nb else []))
    recurrence(list(range(nb)))

    inv_n = 1.0 / RWKV_HEAD_DIM
    for bi in range(nb):
        y = y_s[bi]
        mean = head_sums(y) * inv_n
        d = y - mean
        var = head_sums(d * d) * inv_n
        yn = d * lax.rsqrt(var + RWKV_GN_EPS) * lnw_ref[...] + lnb_ref[...]
        o_ref[bi] = ((yn + bonus_s[bi]) * gate_s[bi]).astype(o_ref.dtype)


def _rwkv(rw, w0, w2p, a0, a2p, g2, k_k, k_a, r_k, ln_w, ln_b, bsz, seq, tt=256, nb=2):
    cols = rw.shape[1]
    W = RWKV_WIDTH
    tile = lambda b, j: (b, j, 0)
    const = lambda b, j: (0, 0)
    vec = pl.BlockSpec((1, W), const)
    out = pl.pallas_call(
        functools.partial(_rwkv_kernel, tt=tt, nb=nb),
        out_shape=jax.ShapeDtypeStruct((bsz, seq, W), BF16),
        grid=(bsz // nb, seq // tt),
        in_specs=[
            pl.BlockSpec((nb, tt, cols), tile),
            vec,
            pl.BlockSpec(w2p.shape, const),
            vec,
            pl.BlockSpec(a2p.shape, const),
            pl.BlockSpec(g2.shape, const),
            vec, vec, vec, vec, vec,
        ],
        out_specs=pl.BlockSpec((nb, tt, W), tile),
        scratch_shapes=[pltpu.VMEM((nb, tt, W), dt) for dt in (BF16, BF16, BF16, F32, BF16, BF16, BF16) + (F32,) * 6]
        + [pltpu.VMEM((nb, W, tt), F32),
           pltpu.VMEM((nb * (RWKV_HEADS // 2) * (tt // CHUNK), CHUNK, 2 * RWKV_HEAD_DIM), F32),
           pltpu.VMEM((nb * RWKV_HEADS // 2, 2 * RWKV_HEAD_DIM, 2 * RWKV_HEAD_DIM), F32)],
        compiler_params=pltpu.CompilerParams(
            dimension_semantics=("arbitrary", "arbitrary"), vmem_limit_bytes=VMEM_LIMIT),
        name="rwkv7",
    )(rw.reshape(bsz, seq, cols), w0, w2p, a0, a2p, g2, k_k, k_a, r_k, ln_w, ln_b)
    return out.reshape(bsz * seq, W)


def _rglru_kernel(gb_ref, xb_ref, wa_ref, ba_ref, wx_ref, bx_ref, lam_ref, o_ref, h_ref, *, tt):
    @pl.when(pl.program_id(1) == 0)
    def _():
        h_ref[...] = jnp.zeros_like(h_ref)

    xc = xb_ref[...]

    blk = xc.shape[1] // LRU_BLOCKS
    xc_bf = xc.astype(BF16)
    ra = jnp.concatenate(
        [_mm(xc_bf[:, n * blk:(n + 1) * blk], wa_ref[n]) for n in range(LRU_BLOCKS)], axis=1)
    ix = jnp.concatenate(
        [_mm(xc_bf[:, n * blk:(n + 1) * blk], wx_ref[n]) for n in range(LRU_BLOCKS)], axis=1)
    r = _sigmoid(ra + ba_ref[...])
    i = _sigmoid(ix + bx_ref[...])
    log_a = -LRU_C * r * _softplus(-lam_ref[...])
    a = jnp.exp(log_a)
    u = xc * i * jnp.sqrt(1.0 - jnp.exp(2.0 * log_a))

    width = xc.shape[1]
    a = a.reshape(tt // SUB, SUB, width)
    u = u.reshape(tt // SUB, SUB, width)
    sub = lax.broadcasted_iota(jnp.int32, (1, SUB, 1), 1)
    d = 1
    while d < SUB:
        keep = sub >= d
        a_sh = jnp.where(keep, pltpu.roll(a, d, axis=1), 1.0)
        u_sh = jnp.where(keep, pltpu.roll(u, d, axis=1), 0.0)
        u = a * u_sh + u
        a = a * a_sh
        d *= 2
    gate = _gelu_tanh(gb_ref[...])
    carry = h_ref[...]
    pack = 2
    for g0 in range(0, tt // SUB, pack):
        outs = []
        for g in range(g0, g0 + pack):
            h = u[g] + a[g] * carry
            outs.append(h * gate[g * SUB:(g + 1) * SUB])
            carry = jnp.broadcast_to(h[SUB - 1:SUB, :], h.shape)
        o_ref[g0 * SUB:(g0 + pack) * SUB, :] = jnp.concatenate(outs, axis=0).astype(o_ref.dtype)
    h_ref[...] = carry


def _rglru(gb, xb, wa_bf, ba, wx_bf, bx, lam, bsz, seq, tt=256):
    n_tok, width = xb.shape
    per_b = seq // tt
    row = lambda b, j: (b * per_b + j, 0)
    const = lambda b, j: (0, 0)
    const3 = lambda b, j: (0, 0, 0)
    vec = pl.BlockSpec((1, width), const)
    return pl.pallas_call(
        functools.partial(_rglru_kernel, tt=tt),
        out_shape=jax.ShapeDtypeStruct((n_tok, width), BF16),
        grid=(bsz, per_b),
        in_specs=[
            pl.BlockSpec((tt, width), row),
            pl.BlockSpec((tt, width), row),
            pl.BlockSpec(wa_bf.shape, const3),
            vec,
            pl.BlockSpec(wx_bf.shape, const3),
            vec,
            vec,
        ],
        out_specs=pl.BlockSpec((tt, width), row),
        scratch_shapes=[pltpu.VMEM((HALO, width), F32)],
        compiler_params=pltpu.CompilerParams(
            dimension_semantics=("arbitrary", "arbitrary"), vmem_limit_bytes=VMEM_LIMIT),
        name="rglru",
    )(gb, xb, wa_bf, ba, wx_bf, bx, lam)


def _pad_lanes(v):
    return jnp.pad(v, (0, LANES - v.shape[0])).reshape(1, LANES)


def kernel(x, c, norm_pre, norm_post, ada_w, ada_b, ffn_w_gate, ffn_w_up, ffn_w_down, mix_w_in, mix_w_out, gdn_conv_w, gdn_a_log, gdn_dt_bias, gdn_norm_w, rwkv_mu, rwkv_w0, rwkv_w2, rwkv_a0, rwkv_a2, rwkv_g2, rwkv_k_k, rwkv_k_a, rwkv_r_k, rwkv_ln_w, rwkv_ln_b, lru_w_in, lru_conv_w, lru_conv_b, lru_wa, lru_ba, lru_wx, lru_bx, lru_lambda, lru_w_out):
    bsz, seq, d = x.shape
    depth = norm_pre.shape[0]
    x2 = x.reshape(bsz * seq, d)
    mods = _ada_params(c, ada_w, ada_b)

    def mod(layer, sub):
        m = mods[layer * 2 + sub]
        return (m[:, None, 0:d], m[:, None, d:2 * d], m[:, None, 2 * d:3 * d])

    GW = GDN_WIDTH
    for layer in range(depth):
        j = layer // 2
        shift, scale, gate = mod(layer, 0)
        shift2, scale2, gate2 = mod(layer, 1)
        mod_rows = jnp.concatenate([gate, shift2, scale2, gate2], axis=1)
        nw_rows = jnp.stack([norm_post[layer, 0], norm_pre[layer, 1], norm_post[layer, 1]])
        nw_pre = norm_pre[layer, 0].reshape(1, d)
        ffn_w = (ffn_w_gate[layer].astype(BF16), ffn_w_up[layer].astype(BF16), ffn_w_down[layer].astype(BF16))
        if layer % 2 == 0:
            w_in = mix_w_in[j].astype(BF16)
            n_gdn = 4 * GW + 2 * GDN_HEADS
            w_cat = jnp.concatenate(
                [w_in[:, 0:4 * GW], w_in[:, n_gdn:],
                 jnp.pad(w_in[:, 4 * GW:n_gdn], ((0, 0), (0, LANES - 2 * GDN_HEADS)))], axis=1)
            qkvz, rw, ab = _norm_proj(
                x2, nw_pre, shift, scale, w_cat, (4 * GW, RWKV_COLS, LANES),
                (("conv_silu", 3 * GW), ("shift_mix",), ("plain",)),
                (gdn_conv_w[j], rwkv_mu[j].reshape(1, RWKV_COLS)), seq)
            out_a = _gdn(qkvz, ab, _pad_lanes(gdn_a_log[j]), _pad_lanes(gdn_dt_bias[j]),
                         gdn_norm_w[j].reshape(1, GDN_HEAD_DIM), bsz, seq)
            w2p = jnp.pad(rwkv_w2[j], ((0, LANES - DECAY_LORA), (0, 0))).astype(BF16)
            a2p = jnp.pad(rwkv_a2[j], ((DECAY_LORA, LANES - DECAY_LORA - AAA_LORA), (0, 0))).astype(BF16)
            vec = lambda t: t.reshape(1, RWKV_WIDTH)
            out_b = _rwkv(rw, vec(rwkv_w0[j]), w2p, vec(rwkv_a0[j]), a2p,
                          rwkv_g2[j].astype(BF16), vec(rwkv_k_k[j]), vec(rwkv_k_a[j]), vec(rwkv_r_k[j]),
                          vec(rwkv_ln_w[j]), vec(rwkv_ln_b[j]), bsz, seq)
            w_out = mix_w_out[j].astype(BF16)
            x2 = _out_ffn(x2, [out_a, out_b], [w_out[0:GW], w_out[GW:]], mod_rows, nw_rows, *ffn_w, seq)
        else:
            width = lru_w_in.shape[2] // 2
            vec = lambda t: t.reshape(1, width)
            gb, xb = _norm_proj(x2, nw_pre, shift, scale, lru_w_in[j].astype(BF16), (width, width),
                                (("plain",), ("conv_bias",)), (lru_conv_w[j], vec(lru_conv_b[j])), seq)
            y = _rglru(gb, xb, lru_wa[j].astype(BF16), vec(lru_ba[j]),
                       lru_wx[j].astype(BF16), vec(lru_bx[j]), vec(lru_lambda[j]), bsz, seq)
            x2 = _out_ffn(x2, [y], [lru_w_out[j].astype(BF16)], mod_rows, nw_rows, *ffn_w, seq)
    return x2.reshape(bsz, seq, d)
```

```python
import functools
import math

import jax
import jax.numpy as jnp
from jax import lax
from jax.experimental import pallas as pl
from jax.experimental.pallas import tpu as pltpu

F32 = jnp.float32
BF16 = jnp.bfloat16

NORM_EPS = 1e-6
LOG2_E = 1.4426950408889634
GDN_HEADS = 4
GDN_HEAD_DIM = 128
GDN_WIDTH = GDN_HEADS * GDN_HEAD_DIM
CHUNK = 64
CONV_WIDTH = 4
RWKV_HEADS = 8
RWKV_HEAD_DIM = 64
RWKV_WIDTH = RWKV_HEADS * RWKV_HEAD_DIM
DECAY_LORA = 64
AAA_LORA = 64
GATE_LORA = 128
RWKV_COLS = 3 * RWKV_WIDTH + DECAY_LORA + AAA_LORA + GATE_LORA
RWKV_GN_EPS = 64e-5
LRU_BLOCKS = 4
LRU_C = 8.0
LANES = 128
MXU_DIM = 256
FFN_PART_ROWS = 256
SUB = 8
HALO = SUB
VMEM_LIMIT = 56 * 1024 * 1024

NN = (((1,), (0,)), ((), ()))
NT = (((1,), (1,)), ((), ()))
TN = (((0,), (0,)), ((), ()))


def _mm(a, b, dims=NN):
    return lax.dot_general(a, b, dims, preferred_element_type=F32)


def _mm_bf(a, b, dims=NN):
    return _mm(a.astype(BF16), b.astype(BF16), dims)


def _split2(x):
    hi = x.astype(BF16)
    lo = (x - hi.astype(F32)).astype(BF16)
    return hi, lo


def _mm_x3(a, b, dims=NN):
    ah, al = _split2(a)
    bh, bl = _split2(b)
    return _mm(ah, bh, dims) + (_mm(ah, bl, dims) + _mm(al, bh, dims))


def _mm_const_lhs(c_bf, x):
    hi = x.astype(BF16)
    r1 = x - hi.astype(F32)
    mid = r1.astype(BF16)
    lo = (r1 - mid.astype(F32)).astype(BF16)
    return _mm(c_bf, hi) + (_mm(c_bf, mid) + _mm(c_bf, lo))


def _mm_const_rhs(x, c_bf):
    hi, lo = _split2(x)
    return _mm(hi, c_bf) + _mm(lo, c_bf)


def _sigmoid(x):
    return 0.5 * jnp.tanh(0.5 * x) + 0.5


def _silu(x):
    h = 0.5 * x
    return h * jnp.tanh(h) + h


def _softplus(x):
    return jnp.maximum(x, 0.0) + jnp.log1p(jnp.exp(-jnp.abs(x)))


def _gelu_tanh(x):
    return 0.5 * x * (1.0 + jnp.tanh(0.7978845608028654 * (x + 0.044715 * (x * x * x))))


def _rms(x):
    return x * lax.rsqrt(jnp.mean(x * x, axis=-1, keepdims=True) + NORM_EPS)


def _tri_masks(n):
    row = lax.broadcasted_iota(jnp.int32, (n, n), 0)
    col = lax.broadcasted_iota(jnp.int32, (n, n), 1)
    return row, col


def _pack_blocks(x):
    out = x[0:CHUNK]
    for c in range(1, x.shape[0] // CHUNK):
        out = out + x[c * CHUNK:(c + 1) * CHUNK]
    return out


def _expand_blocks(xp_bf, mask_bf):
    return jnp.concatenate([xp_bf] * (xp_bf.shape[1] // CHUNK), axis=0) * mask_bf


def _unit_lower_inverses(ms_p, mask_bf):
    n = ms_p[0].shape[1]
    prow = lax.broadcasted_iota(jnp.int32, (CHUNK, n), 0)
    pcol = lax.broadcasted_iota(jnp.int32, (CHUNK, n), 1) & (CHUNK - 1)
    eye_p = (prow == pcol).astype(F32)
    expand = lambda xp: _expand_blocks(xp.astype(BF16), mask_bf)
    mm = lambda xp, y_bf: _mm(xp.astype(BF16), y_bf)
    same = (prow >> 3) == (pcol >> 3)
    m8_p = [jnp.where(same, mp, 0.0) for mp in ms_p]
    m2_p = [mm(ap, expand(ap)) for ap in m8_p]
    yield
    m4_p = [mm(ap, expand(ap)) for ap in m2_p]
    yield
    inv_p = [mm(eye_p - ap, expand(eye_p + bp)) for ap, bp in zip(m8_p, m2_p)]
    yield
    inv_p = [mm(ap, expand(eye_p + bp)) for ap, bp in zip(inv_p, m4_p)]
    yield
    shift = 3
    while (1 << shift) < CHUNK:
        pair = (prow >> (shift + 1)) == (pcol >> (shift + 1))
        off_mask = jnp.logical_and(pair, jnp.logical_not(same))
        tmp_p = [mm(ap, expand(jnp.where(off_mask, mp, 0.0))) for ap, mp in zip(inv_p, ms_p)]
        yield
        inv_p = [ap - mm(tp, expand(ap)) for ap, tp in zip(inv_p, tmp_p)]
        yield
        same = pair
        shift += 1
    a_split = [_split2(ap) for ap in inv_p]
    m_split = [_split2(mp) for mp in ms_p]
    e_hi = [_expand_blocks(a_hi, mask_bf) for a_hi, _ in a_split]
    e_lo = [_expand_blocks(a_lo, mask_bf) for _, a_lo in a_split]
    prod = [_mm(m_hi, eh) + (_mm(m_hi, el) + _mm(m_lo, eh)) for (m_hi, m_lo), eh, el in zip(m_split, e_hi, e_lo)]
    yield
    res = [expand(eye_p - (ap + pr)) for ap, pr in zip(inv_p, prod)]
    return [ap + _mm(a_hi, r) for ap, (a_hi, _), r in zip(inv_p, a_split, res)]


def _run(gen):
    try:
        while True:
            next(gen)
    except StopIteration as stop:
        return stop.value


def _interleave(*gens):
    live = list(gens)
    while live:
        for g in list(live):
            try:
                next(g)
            except StopIteration:
                live.remove(g)


def _ada_kernel(c_ref, w_ref, b_ref, o_ref):
    s = _silu(c_ref[...])
    o_ref[0] = _mm_x3(s, w_ref[0]) + b_ref[0]


def _ada_params(c, ada_w, ada_b):
    n_l, n_s, d, d3 = ada_w.shape
    n = n_l * n_s
    bsz = c.shape[0]
    tn = 1024
    return pl.pallas_call(
        _ada_kernel,
        out_shape=jax.ShapeDtypeStruct((n, bsz, d3), F32),
        grid=(n, d3 // tn),
        in_specs=[
            pl.BlockSpec((bsz, d), lambda i, j: (0, 0)),
            pl.BlockSpec((1, d, tn), lambda i, j: (i, 0, j)),
            pl.BlockSpec((1, 1, tn), lambda i, j: (i, 0, j)),
        ],
        out_specs=pl.BlockSpec((1, bsz, tn), lambda i, j: (i, 0, j)),
        compiler_params=pltpu.CompilerParams(vmem_limit_bytes=VMEM_LIMIT),
        name="ada_params",
    )(c, ada_w.reshape(n, d, d3), ada_b.reshape(n, 1, d3))


def _causal_conv(x, hist, cw):
    n = x.shape[0]
    xp = jnp.concatenate([hist, x], axis=0)
    y = x * cw[CONV_WIDTH - 1:CONV_WIDTH]
    for j in range(CONV_WIDTH - 1):
        o = HALO - (CONV_WIDTH - 1) + j
        y = y + xp[o:o + n] * cw[j:j + 1]
    return y


def _norm_proj_kernel(x_ref, nw_ref, shift_ref, scale_ref, w_ref, *refs, post, per_b):
    n_par = sum({"plain": 0, "conv_silu": 1, "conv_bias": 2, "shift_mix": 1}[p[0]] for p in post)
    par_refs = list(refs[:n_par])
    out_refs = refs[n_par:n_par + len(post)]
    hist_refs = list(refs[n_par + len(post):])

    @pl.when(pl.program_id(0) % per_b == 0)
    def _():
        for h_ref in hist_refs:
            h_ref[...] = jnp.zeros_like(h_ref)

    tm = x_ref.shape[0]
    halves = [slice(0, tm // 2), slice(tm // 2, tm)]
    hs = [((_rms(x_ref[rows, :]) * nw_ref[...]) * (1.0 + scale_ref[0]) + shift_ref[0]).astype(BF16)
          for rows in halves]
    off = 0
    for o_ref, p in zip(out_refs, post):
        n = o_ref.shape[1]
        ys = [_mm(h, w_ref[:, off:off + n]) for h in hs]
        off += n
        if p[0] == "plain":
            outs = ys
        else:
            h_ref = hist_refs.pop(0)
            hist = h_ref[...]
            h_ref[...] = ys[-1][tm // 2 - HALO:tm // 2]
            hists = [hist, ys[0][tm // 2 - HALO:tm // 2]]
            if p[0] == "conv_silu":
                cw = par_refs.pop(0)[...]
                outs = [jnp.concatenate([_silu(_causal_conv(y[:, 0:p[1]], hi[:, 0:p[1]], cw)), y[:, p[1]:n]], axis=1)
                        for y, hi in zip(ys, hists)]
            elif p[0] == "conv_bias":
                cw = par_refs.pop(0)[...]
                cb = par_refs.pop(0)[...]
                outs = [_causal_conv(y, hi, cw) + cb for y, hi in zip(ys, hists)]
            else:
                mu = par_refs.pop(0)[...]
                trow = lax.broadcasted_iota(jnp.int32, (tm // 2, 1), 0)
                outs = [y + mu * (jnp.where(trow == 0, hi[HALO - 1:HALO, :], pltpu.roll(y, 1, axis=0)) - y)
                        for y, hi in zip(ys, hists)]
        for rows, out in zip(halves, outs):
            o_ref[rows, :] = out


def _norm_proj(x2, nw, shift, scale, w_bf, splits, post, params, seq, tm=512):
    n_tok, d = x2.shape
    per_b = seq // tm
    const = lambda i: (0, 0)
    return pl.pallas_call(
        functools.partial(_norm_proj_kernel, post=post, per_b=per_b),
        out_shape=[jax.ShapeDtypeStruct((n_tok, n), F32) for n in splits],
        grid=(n_tok // tm,),
        in_specs=[
            pl.BlockSpec((tm, d), lambda i: (i, 0)),
            pl.BlockSpec((1, d), const),
            pl.BlockSpec((1, 1, d), lambda i: (i // per_b, 0, 0)),
            pl.BlockSpec((1, 1, d), lambda i: (i // per_b, 0, 0)),
            pl.BlockSpec(w_bf.shape, const, pipeline_mode=pl.Buffered(1)),
        ] + [pl.BlockSpec(p.shape, const) for p in params],
        out_specs=[pl.BlockSpec((tm, n), lambda i: (i, 0)) for n in splits],
        scratch_shapes=[pltpu.VMEM((HALO, n), F32) for n, p in zip(splits, post) if p[0] != "plain"],
        compiler_params=pltpu.CompilerParams(
            dimension_semantics=("arbitrary",), vmem_limit_bytes=VMEM_LIMIT),
        name="norm_proj",
    )(x2, nw, shift, scale, w_bf, *params)


def _out_ffn_kernel(*refs, n_in):
    x_ref = refs[0]
    a_refs = refs[1:1 + n_in]
    w_refs = refs[1 + n_in:1 + 2 * n_in]
    mod_ref, nw_ref, wg_ref, wu_ref, wd_ref, o_ref = refs[1 + 2 * n_in:]
    mod = mod_ref[0]
    nw = nw_ref[...]
    tm = x_ref.shape[0]
    halves = [slice(r, r + FFN_PART_ROWS) for r in range(0, tm, FFN_PART_ROWS)]
    ys = []
    for rows in halves:
        y = _mm(a_refs[0][rows, :].astype(BF16), w_refs[0][...])
        for a_ref, w_ref in zip(a_refs[1:], w_refs[1:]):
            y = y + _mm(a_ref[rows, :].astype(BF16), w_ref[...])
        ys.append(y)
    xs = [x_ref[rows, :] + mod[0:1] * (_rms(y) * nw[0:1]) for rows, y in zip(halves, ys)]
    hs = [((_rms(x) * nw[1:2]) * (1.0 + mod[2:3]) + mod[1:2]).astype(BF16) for x in xs]
    gs = [_mm(h, wg_ref[...]) for h in hs]
    us = [_mm(h, wu_ref[...]) for h in hs]
    acts = [(_silu(g) * u).astype(BF16) for g, u in zip(gs, us)]
    ys = [_mm(act, wd_ref[...]) for act in acts]
    for rows, x, y in zip(halves, xs, ys):
        o_ref[rows, :] = x + mod[3:4] * (_rms(y) * nw[2:3])


def _out_ffn(x2, acts, ws_bf, mod, nw, wg_bf, wu_bf, wd_bf, seq, tm=1024):
    n_tok, d = x2.shape
    per_b = seq // tm
    row = lambda i: (i, 0)
    const = lambda i: (0, 0)
    resident = lambda w: pl.BlockSpec(w.shape, const, pipeline_mode=pl.Buffered(1))
    in_specs = [pl.BlockSpec((tm, d), row)]
    in_specs += [pl.BlockSpec((tm, a.shape[1]), row) for a in acts]
    in_specs += [resident(w) for w in ws_bf]
    in_specs += [pl.BlockSpec((1,) + mod.shape[1:], lambda i: (i // per_b, 0, 0)),
                 pl.BlockSpec(nw.shape, const), resident(wg_bf), resident(wu_bf), resident(wd_bf)]
    return pl.pallas_call(
        functools.partial(_out_ffn_kernel, n_in=len(acts)),
        out_shape=jax.ShapeDtypeStruct((n_tok, d), F32),
        grid=(n_tok // tm,),
        in_specs=in_specs,
        out_specs=pl.BlockSpec((tm, d), row),
        compiler_params=pltpu.CompilerParams(vmem_limit_bytes=VMEM_LIMIT),
        name="out_ffn",
    )(x2, *acts, *ws_bf, mod, nw, wg_bf, wu_bf, wd_bf)


def _gdn_kernel(qkvz_ref, ab_ref, alog_ref, dtb_ref, nw_ref, o_ref,
                qkv_s, gc_s, u_s, w_s, qd_s, kd_s, attn_s, o_s, state_ref, *, tt, nb):
    W = GDN_WIDTH
    D = GDN_HEAD_DIM
    C = CHUNK
    H = GDN_HEADS
    n_c = tt // C

    @pl.when(pl.program_id(1) == 0)
    def _():
        state_ref[...] = jnp.zeros_like(state_ref)

    row, col = _tri_masks(tt)
    same_chunk = (row >> 6) == (col >> 6)
    causal = jnp.logical_and(same_chunk, row >= col)
    strict = jnp.logical_and(same_chunk, row > col)
    tril_bf = causal.astype(BF16)
    mask_bf = same_chunk.astype(BF16)
    ones_bf = jnp.ones((D, D), BF16)

    units = [(bi, h) for bi in range(nb) for h in range(H)]
    qs, ks, vs, betas, k_betas, decays, e_gs, e_rems = [], [], [], [], [], [], [], []
    for bi in range(nb):
        y = qkvz_ref[bi, :, 0:3 * W]
        for h in range(H):
            q = y[:, h * D:(h + 1) * D]
            k = y[:, W + h * D:W + (h + 1) * D]
            q = q * lax.rsqrt(_mm_const_rhs(q * q, ones_bf) + NORM_EPS) * (D ** -0.5)
            k = k * lax.rsqrt(_mm_const_rhs(k * k, ones_bf) + NORM_EPS)
            qkv_s[bi, :, h * D:(h + 1) * D] = q
            qkv_s[bi, :, W + h * D:W + (h + 1) * D] = k
        qkv_s[bi, :, 2 * W:3 * W] = y[:, 2 * W:3 * W]

        ab = ab_ref[bi]
        g = (-LOG2_E * jnp.exp(alog_ref[...])) * _softplus(ab + dtb_ref[...])
        beta_all = _sigmoid(ab)
        gc = _mm_const_lhs(tril_bf, g)
        gc_s[bi] = gc
        gc_t = gc.T
        g_last = jnp.concatenate(
            [jnp.broadcast_to(gc[c * C + C - 1:c * C + C, :], (C, LANES)) for c in range(n_c)], axis=0)
        e_g = jnp.exp2(gc)
        e_rem = jnp.exp2(g_last - gc)
        for h in range(H):
            qs.append(qkv_s[bi, :, h * D:(h + 1) * D])
            ks.append(qkv_s[bi, :, W + h * D:W + (h + 1) * D])
            vs.append(qkv_s[bi, :, 2 * W + h * D:2 * W + (h + 1) * D])
            betas.append(beta_all[:, H + h:H + h + 1])
            k_betas.append(ks[-1] * betas[-1])
            decays.append(
                jnp.where(causal, jnp.exp2(jnp.minimum(gc[:, h:h + 1] - gc_t[h:h + 1, :], 0.0)), 0.0))
            e_gs.append(e_g[:, h:h + 1])
            e_rems.append(e_rem[:, h:h + 1])

    ms = [jnp.where(strict, _mm_bf(kb, k, NT) * dc, 0.0) for kb, k, dc in zip(k_betas, ks, decays)]
    t_ps = _run(_unit_lower_inverses([_pack_blocks(m) for m in ms], mask_bf))
    for i in range(len(units)):
        rhs = jnp.concatenate([vs[i] * betas[i], k_betas[i] * e_gs[i]], axis=1)
        t_hi, t_lo = _split2(t_ps[i])
        rhs_hi, rhs_lo = _split2(rhs)
        t_hi = _expand_blocks(t_hi, mask_bf)
        uw = _mm(t_hi, rhs_hi) + (_mm(t_hi, rhs_lo) + _mm(_expand_blocks(t_lo, mask_bf), rhs_hi))
        u_s[i] = uw[:, 0:D]
        w_s[i] = uw[:, D:2 * D]
        attn_s[i] = _mm_bf(qs[i], ks[i], NT) * decays[i]
        qd_s[i] = qs[i] * e_gs[i]
        kd_s[i] = ks[i] * e_rems[i]

    for c in range(n_c):
        rows = slice(c * C, (c + 1) * C)
        win = slice((c // 2) * 2 * C, (c // 2 + 1) * 2 * C)
        ss = [state_ref[i] for i in range(len(units))]
        wq = [_mm_bf(jnp.concatenate([w_s[i, rows, :], qd_s[i, rows, :]], axis=0), s) for i, s in enumerate(ss)]
        v_new = [u_s[i, rows, :] - x[0:C] for i, x in enumerate(wq)]
        kv = [_mm_bf(kd_s[i, rows, :], x, TN) for i, x in enumerate(v_new)]
        for i, (bi, h) in enumerate(units):
            gl = jnp.exp2(gc_s[bi, c * C + C - 1:c * C + C, h:h + 1])
            state_ref[i] = ss[i] * gl + kv[i]
            vv = jnp.concatenate([v_new[i], v_new[i]], axis=0)
            o_s[bi, rows, h * D:(h + 1) * D] = wq[i][C:2 * C] + _mm_bf(attn_s[i, rows, win], vv)

    nw = nw_ref[...]
    for bi, h in units:
        z = qkvz_ref[bi, :, 3 * W + h * D:3 * W + (h + 1) * D]
        o_ref[bi, :, h * D:(h + 1) * D] = (_rms(o_s[bi, :, h * D:(h + 1) * D]) * nw * _silu(z)).astype(o_ref.dtype)


def _gdn(qkvz, ab, a_log_row, dt_row, norm_w, bsz, seq, tt=256, nb=2):
    W = GDN_WIDTH
    tile = lambda b, j: (b, j, 0)
    const = lambda b, j: (0, 0)
    n_u = nb * GDN_HEADS
    out = pl.pallas_call(
        functools.partial(_gdn_kernel, tt=tt, nb=nb),
        out_shape=jax.ShapeDtypeStruct((bsz, seq, W), BF16),
        grid=(bsz // nb, seq // tt),
        in_specs=[
            pl.BlockSpec((nb, tt, 4 * W), tile),
            pl.BlockSpec((nb, tt, LANES), tile),
            pl.BlockSpec((1, LANES), const),
            pl.BlockSpec((1, LANES), const),
            pl.BlockSpec((1, GDN_HEAD_DIM), const),
        ],
        out_specs=pl.BlockSpec((nb, tt, W), tile),
        scratch_shapes=[
            pltpu.VMEM((nb, tt, 3 * W), F32),
            pltpu.VMEM((nb, tt, LANES), F32),
            pltpu.VMEM((n_u, tt, GDN_HEAD_DIM), F32),
            pltpu.VMEM((n_u, tt, GDN_HEAD_DIM), F32),
            pltpu.VMEM((n_u, tt, GDN_HEAD_DIM), F32),
            pltpu.VMEM((n_u, tt, GDN_HEAD_DIM), F32),
            pltpu.VMEM((n_u, tt, tt), F32),
            pltpu.VMEM((nb, tt, W), F32),
            pltpu.VMEM((n_u, GDN_HEAD_DIM, GDN_HEAD_DIM), F32),
        ],
        compiler_params=pltpu.CompilerParams(
            dimension_semantics=("arbitrary", "arbitrary"), vmem_limit_bytes=VMEM_LIMIT),
        name="gdn",
    )(qkvz.reshape(bsz, seq, 4 * W), ab.reshape(bsz, seq, LANES), a_log_row, dt_row, norm_w)
    return out.reshape(bsz * seq, W)


def _rwkv_kernel(rw_ref, w0_ref, w2_ref, a0_ref, a2_ref, g2_ref, kk_ref, ka_ref, rk_ref,
                 lnw_ref, lnb_ref, o_ref,
                 r_s, k_s, b_s, kk_s, kend_s, bend_s, v_s, y_s, u_s, w_s, rv_s, bonus_s, gate_s,
                 lct_s, arb_s, state_ref, *, tt, nb):
    W = RWKV_WIDTH
    C = CHUNK
    NH = RWKV_HEADS
    P2 = 2 * RWKV_HEAD_DIM
    n_c = tt // C
    n_p = NH // 2

    @pl.when(pl.program_id(1) == 0)
    def _():
        state_ref[...] = jnp.zeros_like(state_ref)

    hrow, hcol = _tri_masks(MXU_DIM)
    head_ones = ((hrow >> 6) == (hcol >> 6)).astype(BF16)

    def head_sums(x):
        return jnp.concatenate(
            [_mm_const_rhs(x[:, o:o + MXU_DIM], head_ones) for o in range(0, W, MXU_DIM)], axis=1)

    row, col = _tri_masks(tt)
    tril_bf = jnp.logical_and((row >> 6) == (col >> 6), row >= col).astype(BF16)

    def prepass(bi):
        cf = rw_ref[bi]

        r = cf[:, 0:W]
        k = cf[:, W:2 * W]
        v = cf[:, 2 * W:3 * W]
        wd_ad = cf[:, 3 * W:3 * W + LANES]
        gd = cf[:, 3 * W + LANES:3 * W + 2 * LANES]

        w_in = w0_ref[...] + _mm_bf(jnp.tanh(wd_ad), w2_ref[...])
        a = _sigmoid(a0_ref[...] + _mm_bf(wd_ad, a2_ref[...]))
        yield
        gate_s[bi] = _mm_bf(_sigmoid(gd), g2_ref[...])
        kkv = k * kk_ref[...]
        yield
        kk = kkv * lax.rsqrt(head_sums(kkv * kkv) + NORM_EPS)
        kmod = k * (1.0 + (a - 1.0) * ka_ref[...])
        yield
        bonus_s[bi] = head_sums(r * kmod * rk_ref[...]) * v
        lw = (-math.exp(-0.5) * LOG2_E) * _sigmoid(w_in)
        yield
        lc = _mm_const_lhs(tril_bf, lw)
        lct_s[bi] = lc.T
        l_last = jnp.concatenate(
            [jnp.broadcast_to(lc[c * C + C - 1:c * C + C, :], (C, W)) for c in range(n_c)], axis=0)
        yield
        e_inv = jnp.exp2(-lc)
        e_rem = jnp.exp2(l_last - lc)
        b = kk * a
        r_s[bi] = (r * jnp.exp2(lc)).astype(BF16)
        yield
        k_s[bi] = (kmod * e_inv).astype(BF16)
        b_s[bi] = (b * e_inv).astype(BF16)
        yield
        kk_s[bi] = kk * jnp.exp2(lc - lw)
        kend_s[bi] = (kmod * e_rem).astype(BF16)
        yield
        bend_s[bi] = (b * e_rem).astype(BF16)
        v_s[bi] = v.astype(BF16)

    lane = lax.broadcasted_iota(jnp.int32, (1, P2), 1)
    first = lane < RWKV_HEAD_DIM
    rows_of = lambda c: slice(c * C, (c + 1) * C)
    lanes_of = lambda p: slice(p * P2, (p + 1) * P2)
    cut = lambda ref, u: ref[u[0], rows_of(u[2]), lanes_of(u[1])]
    slot = lambda u: (u[0] * n_p + u[1]) * n_c + u[2]

    def per_head_rows(x):
        zero = jnp.zeros_like(x)
        return jnp.concatenate([jnp.where(first, x, zero), jnp.where(first, zero, x)], axis=0)

    def chunk_form(seqs):
        prow = lax.broadcasted_iota(jnp.int32, (C, 2 * P2), 0)
        pcol = lax.broadcasted_iota(jnp.int32, (C, 2 * P2), 1) & (C - 1)
        strict_p = prow > pcol
        causal_p = prow >= pcol
        units = [(bi, p, c) for bi in seqs for p in range(n_p) for c in range(n_c)]

        prod = [_mm(jnp.concatenate([cut(kk_s, u).astype(BF16), cut(r_s, u)], axis=0),
                    jnp.concatenate([per_head_rows(cut(b_s, u)), per_head_rows(cut(k_s, u))], axis=0), NT)
                for u in units]
        kbkk = [jnp.where(strict_p, x[0:C], 0.0) for x in prod]
        rbrk = [jnp.where(causal_p, x[C:2 * C], 0.0) for x in prod]
        for u, x in zip(units, rbrk):
            arb_s[slot(u)] = x[:, 0:P2]
        yield
        avrv = [_mm(jnp.concatenate([x[:, P2:2 * P2], y[:, P2:2 * P2]], axis=0).astype(BF16),
                    per_head_rows(cut(v_s, u))) for x, y, u in zip(kbkk, rbrk, units)]
        for u, x in zip(units, avrv):
            rv_s[u[0], rows_of(u[2]), lanes_of(u[1])] = x[C:2 * C]
        yield
        prow2, pcol2 = _tri_masks(2 * P2)
        t_p = yield from _unit_lower_inverses(
            [jnp.concatenate([kbkk[i][:, 0:P2], kbkk[i + 1][:, 0:P2]], axis=1) for i in range(0, len(units), 2)],
            ((prow2 >> 6) == (pcol2 >> 6)).astype(BF16))
        first2 = jnp.concatenate([first, first], axis=1)

        def per_head_rows2(x):
            zero = jnp.zeros_like(x)
            return jnp.concatenate([jnp.where(first2, x, zero), jnp.where(first2, zero, x)], axis=0)

        for i, u in enumerate(units):
            t_hi, t_lo = _split2(t_p[i // 2][:, (i % 2) * P2:(i % 2 + 1) * P2])
            x_hi, x_lo = _split2(jnp.concatenate([avrv[i][0:C], cut(kk_s, u)], axis=1))
            x_hi = per_head_rows2(x_hi)
            uw = _mm(t_hi, x_hi) + (_mm(t_hi, per_head_rows2(x_lo)) + _mm(t_lo, x_hi))
            u_s[u[0], rows_of(u[2]), lanes_of(u[1])] = uw[:, 0:P2]
            w_s[u[0], rows_of(u[2]), lanes_of(u[1])] = uw[:, P2:2 * P2]

    def recurrence(seqs):
        brow, bcol = _tri_masks(P2)
        blockdiag = (brow >> 6) == (bcol >> 6)
        pairs = [(bi, p) for bi in seqs for p in range(n_p)]
        for c in range(n_c):
            us = [(bi, p, c) for bi, p in pairs]
            ss = [state_ref[bi * n_p + p] for bi, p in pairs]
            rw = [_mm(jnp.concatenate([cut(r_s, u), cut(w_s, u).astype(BF16)], axis=0), s.astype(BF16))
                  for u, s in zip(us, ss)]
            pm = [-(x[C:2 * C] + cut(u_s, u)) for u, x in zip(us, rw)]
            upd = [_mm(jnp.concatenate([cut(bend_s, u), cut(kend_s, u)], axis=0),
                       jnp.concatenate([x.astype(BF16), cut(v_s, u)], axis=0), TN) for u, x in zip(us, pm)]
            for i, (bi, p) in enumerate(pairs):
                g_col = jnp.exp2(lct_s[bi, p * P2:(p + 1) * P2, c * C + C - 1:c * C + C])
                state_ref[bi * n_p + p] = ss[i] * g_col + jnp.where(blockdiag, upd[i], 0.0)
                y_s[bi, rows_of(c), lanes_of(p)] = rw[i][0:C] + cut(rv_s, us[i]) + _mm_bf(
                    arb_s[slot(us[i])], per_head_rows(pm[i]))

    _run(prepass(0))
    for bi in range(nb):
        _interleave(chunk_form([bi]), *([prepass(bi + 1)] if bi + 1 < nb else []))
    recurrence(list(range(nb)))

    inv_n = 1.0 / RWKV_HEAD_DIM
    for bi in range(nb):
        y = y_s[bi]
        mean = head_sums(y) * inv_n
        d = y - mean
        var = head_sums(d * d) * inv_n
        yn = d * lax.rsqrt(var + RWKV_GN_EPS) * lnw_ref[...] + lnb_ref[...]
        o_ref[bi] = ((yn + bonus_s[bi]) * gate_s[bi]).astype(o_ref.dtype)


def _rwkv(rw, w0, w2p, a0, a2p, g2, k_k, k_a, r_k, ln_w, ln_b, bsz, seq, tt=256, nb=2):
    cols = rw.shape[1]
    W = RWKV_WIDTH
    tile = lambda b, j: (b, j, 0)
    const = lambda b, j: (0, 0)
    vec = pl.BlockSpec((1, W), const)
    out = pl.pallas_call(
        functools.partial(_rwkv_kernel, tt=tt, nb=nb),
        out_shape=jax.ShapeDtypeStruct((bsz, seq, W), BF16),
        grid=(bsz // nb, seq // tt),
        in_specs=[
            pl.BlockSpec((nb, tt, cols), tile),
            vec,
            pl.BlockSpec(w2p.shape, const),
            vec,
            pl.BlockSpec(a2p.shape, const),
            pl.BlockSpec(g2.shape, const),
            vec, vec, vec, vec, vec,
        ],
        out_specs=pl.BlockSpec((nb, tt, W), tile),
        scratch_shapes=[pltpu.VMEM((nb, tt, W), dt) for dt in (BF16, BF16, BF16, F32, BF16, BF16, BF16) + (F32,) * 6]
        + [pltpu.VMEM((nb, W, tt), F32),
           pltpu.VMEM((nb * (RWKV_HEADS // 2) * (tt // CHUNK), CHUNK, 2 * RWKV_HEAD_DIM), F32),
           pltpu.VMEM((nb * RWKV_HEADS // 2, 2 * RWKV_HEAD_DIM, 2 * RWKV_HEAD_DIM), F32)],
        compiler_params=pltpu.CompilerParams(
            dimension_semantics=("arbitrary", "arbitrary"), vmem_limit_bytes=VMEM_LIMIT),
        name="rwkv7",
    )(rw.reshape(bsz, seq, cols), w0, w2p, a0, a2p, g2, k_k, k_a, r_k, ln_w, ln_b)
    return out.reshape(bsz * seq, W)


def _rglru_kernel(gb_ref, xb_ref, wa_ref, ba_ref, wx_ref, bx_ref, lam_ref, o_ref, h_ref, *, tt):
    @pl.when(pl.program_id(1) == 0)
    def _():
        h_ref[...] = jnp.zeros_like(h_ref)

    xc = xb_ref[...]

    blk = xc.shape[1] // LRU_BLOCKS
    xc_bf = xc.astype(BF16)
    ra = jnp.concatenate(
        [_mm(xc_bf[:, n * blk:(n + 1) * blk], wa_ref[n]) for n in range(LRU_BLOCKS)], axis=1)
    ix = jnp.concatenate(
        [_mm(xc_bf[:, n * blk:(n + 1) * blk], wx_ref[n]) for n in range(LRU_BLOCKS)], axis=1)
    r = _sigmoid(ra + ba_ref[...])
    i = _sigmoid(ix + bx_ref[...])
    a = jnp.exp2(r * ((-LRU_C * LOG2_E) * _softplus(-lam_ref[...])))
    u = xc * i * jnp.sqrt(1.0 - a * a)

    width = xc.shape[1]
    a = a.reshape(tt // SUB, SUB, width)
    u = u.reshape(tt // SUB, SUB, width)
    sub = lax.broadcasted_iota(jnp.int32, (1, SUB, 1), 1)
    d = 1
    while d < SUB:
        keep = sub >= d
        a_sh = jnp.where(keep, pltpu.roll(a, d, axis=1), 1.0)
        u_sh = jnp.where(keep, pltpu.roll(u, d, axis=1), 0.0)
        u = a * u_sh + u
        a = a * a_sh
        d *= 2
    gate = _gelu_tanh(gb_ref[...])
    carry = h_ref[...]
    pack = 2
    for g0 in range(0, tt // SUB, pack):
        outs = []
        for g in range(g0, g0 + pack):
            h = u[g] + a[g] * carry
            outs.append(h * gate[g * SUB:(g + 1) * SUB])
            carry = jnp.broadcast_to(h[SUB - 1:SUB, :], h.shape)
        o_ref[g0 * SUB:(g0 + pack) * SUB, :] = jnp.concatenate(outs, axis=0).astype(o_ref.dtype)
    h_ref[...] = carry


def _rglru(gb, xb, wa_bf, ba, wx_bf, bx, lam, bsz, seq, tt=256):
    n_tok, width = xb.shape
    per_b = seq // tt
    row = lambda b, j: (b * per_b + j, 0)
    const = lambda b, j: (0, 0)
    const3 = lambda b, j: (0, 0, 0)
    vec = pl.BlockSpec((1, width), const)
    return pl.pallas_call(
        functools.partial(_rglru_kernel, tt=tt),
        out_shape=jax.ShapeDtypeStruct((n_tok, width), BF16),
        grid=(bsz, per_b),
        in_specs=[
            pl.BlockSpec((tt, width), row),
            pl.BlockSpec((tt, width), row),
            pl.BlockSpec(wa_bf.shape, const3),
            vec,
            pl.BlockSpec(wx_bf.shape, const3),
            vec,
            vec,
        ],
        out_specs=pl.BlockSpec((tt, width), row),
        scratch_shapes=[pltpu.VMEM((HALO, width), F32)],
        compiler_params=pltpu.CompilerParams(
            dimension_semantics=("arbitrary", "arbitrary"), vmem_limit_bytes=VMEM_LIMIT),
        name="rglru",
    )(gb, xb, wa_bf, ba, wx_bf, bx, lam)


def _pad_lanes(v):
    return jnp.pad(v, (0, LANES - v.shape[0])).reshape(1, LANES)


def kernel(x, c, norm_pre, norm_post, ada_w, ada_b, ffn_w_gate, ffn_w_up, ffn_w_down, mix_w_in, mix_w_out, gdn_conv_w, gdn_a_log, gdn_dt_bias, gdn_norm_w, rwkv_mu, rwkv_w0, rwkv_w2, rwkv_a0, rwkv_a2, rwkv_g2, rwkv_k_k, rwkv_k_a, rwkv_r_k, rwkv_ln_w, rwkv_ln_b, lru_w_in, lru_conv_w, lru_conv_b, lru_wa, lru_ba, lru_wx, lru_bx, lru_lambda, lru_w_out):
    bsz, seq, d = x.shape
    depth = norm_pre.shape[0]
    x2 = x.reshape(bsz * seq, d)
    mods = _ada_params(c, ada_w, ada_b)

    def mod(layer, sub):
        m = mods[layer * 2 + sub]
        return (m[:, None, 0:d], m[:, None, d:2 * d], m[:, None, 2 * d:3 * d])

    GW = GDN_WIDTH
    for layer in range(depth):
        j = layer // 2
        shift, scale, gate = mod(layer, 0)
        shift2, scale2, gate2 = mod(layer, 1)
        mod_rows = jnp.concatenate([gate, shift2, scale2, gate2], axis=1)
        nw_rows = jnp.stack([norm_post[layer, 0], norm_pre[layer, 1], norm_post[layer, 1]])
        nw_pre = norm_pre[layer, 0].reshape(1, d)
        ffn_w = (ffn_w_gate[layer].astype(BF16), ffn_w_up[layer].astype(BF16), ffn_w_down[layer].astype(BF16))
        if layer % 2 == 0:
            w_in = mix_w_in[j].astype(BF16)
            n_gdn = 4 * GW + 2 * GDN_HEADS
            w_cat = jnp.concatenate(
                [w_in[:, 0:4 * GW], w_in[:, n_gdn:],
                 jnp.pad(w_in[:, 4 * GW:n_gdn], ((0, 0), (0, LANES - 2 * GDN_HEADS)))], axis=1)
            qkvz, rw, ab = _norm_proj(
                x2, nw_pre, shift, scale, w_cat, (4 * GW, RWKV_COLS, LANES),
                (("conv_silu", 3 * GW), ("shift_mix",), ("plain",)),
                (gdn_conv_w[j], rwkv_mu[j].reshape(1, RWKV_COLS)), seq)
            out_a = _gdn(qkvz, ab, _pad_lanes(gdn_a_log[j]), _pad_lanes(gdn_dt_bias[j]),
                         gdn_norm_w[j].reshape(1, GDN_HEAD_DIM), bsz, seq)
            w2p = jnp.pad(rwkv_w2[j], ((0, LANES - DECAY_LORA), (0, 0))).astype(BF16)
            a2p = jnp.pad(rwkv_a2[j], ((DECAY_LORA, LANES - DECAY_LORA - AAA_LORA), (0, 0))).astype(BF16)
            vec = lambda t: t.reshape(1, RWKV_WIDTH)
            out_b = _rwkv(rw, vec(rwkv_w0[j]), w2p, vec(rwkv_a0[j]), a2p,
                          rwkv_g2[j].astype(BF16), vec(rwkv_k_k[j]), vec(rwkv_k_a[j]), vec(rwkv_r_k[j]),
                          vec(rwkv_ln_w[j]), vec(rwkv_ln_b[j]), bsz, seq)
            w_out = mix_w_out[j].astype(BF16)
            x2 = _out_ffn(x2, [out_a, out_b], [w_out[0:GW], w_out[GW:]], mod_rows, nw_rows, *ffn_w, seq)
        else:
            width = lru_w_in.shape[2] // 2
            vec = lambda t: t.reshape(1, width)
            gb, xb = _norm_proj(x2, nw_pre, shift, scale, lru_w_in[j].astype(BF16), (width, width),
                                (("plain",), ("conv_bias",)), (lru_conv_w[j], vec(lru_conv_b[j])), seq)
            y = _rglru(gb, xb, lru_wa[j].astype(BF16), vec(lru_ba[j]),
                       lru_wx[j].astype(BF16), vec(lru_bx[j]), vec(lru_lambda[j]), bsz, seq)
            x2 = _out_ffn(x2, [y], [lru_w_out[j].astype(BF16)], mod_rows, nw_rows, *ffn_w, seq)
    return x2.reshape(bsz, seq, d)
```

```python
import functools
import math

import jax
import jax.numpy as jnp
from jax import lax
from jax.experimental import pallas as pl
from jax.experimental.pallas import tpu as pltpu

F32 = jnp.float32
BF16 = jnp.bfloat16

NORM_EPS = 1e-6
LOG2_E = 1.4426950408889634
GDN_HEADS = 4
GDN_HEAD_DIM = 128
GDN_WIDTH = GDN_HEADS * GDN_HEAD_DIM
CHUNK_LOG2 = 6
CHUNK = 1 << CHUNK_LOG2
BASE_LOG2 = 3
CONV_WIDTH = 4
RWKV_HEADS = 8
RWKV_HEAD_LOG2 = 6
RWKV_HEAD_DIM = 1 << RWKV_HEAD_LOG2
RWKV_WIDTH = RWKV_HEADS * RWKV_HEAD_DIM
DECAY_LORA = 64
AAA_LORA = 64
GATE_LORA = 128
RWKV_COLS = 3 * RWKV_WIDTH + DECAY_LORA + AAA_LORA + GATE_LORA
RWKV_GN_EPS = 64e-5
LRU_BLOCKS = 4
LRU_C = 8.0
LANES = 128
MXU_DIM = 256
FFN_PART_ROWS = 256
SUB = 8
HALO = SUB
VMEM_LIMIT = 56 * 1024 * 1024

NN = (((1,), (0,)), ((), ()))
NT = (((1,), (1,)), ((), ()))
TN = (((0,), (0,)), ((), ()))


def _mm(a, b, dims=NN):
    return lax.dot_general(a, b, dims, preferred_element_type=F32)


def _mm_bf(a, b, dims=NN):
    return _mm(a.astype(BF16), b.astype(BF16), dims)


def _split2(x):
    hi = x.astype(BF16)
    lo = (x - hi.astype(F32)).astype(BF16)
    return hi, lo


def _mm_x3(a, b, dims=NN):
    ah, al = _split2(a)
    bh, bl = _split2(b)
    return _mm(ah, bh, dims) + (_mm(ah, bl, dims) + _mm(al, bh, dims))


def _mm_const_lhs(c_bf, x):
    hi = x.astype(BF16)
    r1 = x - hi.astype(F32)
    mid = r1.astype(BF16)
    lo = (r1 - mid.astype(F32)).astype(BF16)
    return _mm(c_bf, hi) + (_mm(c_bf, mid) + _mm(c_bf, lo))


def _mm_const_rhs(x, c_bf):
    hi, lo = _split2(x)
    return _mm(hi, c_bf) + _mm(lo, c_bf)


def _sigmoid(x):
    return 0.5 * jnp.tanh(0.5 * x) + 0.5


def _silu(x):
    h = 0.5 * x
    return h * jnp.tanh(h) + h


def _softplus(x):
    return jnp.maximum(x, 0.0) + jnp.log1p(jnp.exp(-jnp.abs(x)))


def _gelu_tanh(x):
    c = 0.7978845608028654
    h = 0.5 * x
    return h * jnp.tanh(x * (c + (c * 0.044715) * (x * x))) + h


def _rms(x):
    return x * lax.rsqrt(jnp.mean(x * x, axis=-1, keepdims=True) + NORM_EPS)


def _tri_masks(n):
    row = lax.broadcasted_iota(jnp.int32, (n, n), 0)
    col = lax.broadcasted_iota(jnp.int32, (n, n), 1)
    return row, col


def _pack_blocks(x):
    out = x[0:CHUNK]
    for c in range(1, x.shape[0] // CHUNK):
        out = out + x[c * CHUNK:(c + 1) * CHUNK]
    return out


def _expand_blocks(xp_bf, mask_bf):
    return jnp.concatenate([xp_bf] * (xp_bf.shape[1] // CHUNK), axis=0) * mask_bf


def _unit_lower_inverses(ms_p, mask_bf):
    n = ms_p[0].shape[1]
    prow = lax.broadcasted_iota(jnp.int32, (CHUNK, n), 0)
    pcol = lax.broadcasted_iota(jnp.int32, (CHUNK, n), 1) & (CHUNK - 1)
    eye_p = (prow == pcol).astype(F32)
    expand = lambda xp: _expand_blocks(xp.astype(BF16), mask_bf)
    mm = lambda xp, y_bf: _mm(xp.astype(BF16), y_bf)
    same = (prow >> BASE_LOG2) == (pcol >> BASE_LOG2)
    m8_p = [jnp.where(same, mp, 0.0) for mp in ms_p]
    m2_p = [mm(ap, expand(ap)) for ap in m8_p]
    yield
    m4_p = [mm(ap, expand(ap)) for ap in m2_p]
    yield
    inv_p = [mm(eye_p - ap, expand(eye_p + bp)) for ap, bp in zip(m8_p, m2_p)]
    yield
    inv_p = [mm(ap, expand(eye_p + bp)) for ap, bp in zip(inv_p, m4_p)]
    yield
    shift = BASE_LOG2
    while shift < CHUNK_LOG2:
        pair = (prow >> (shift + 1)) == (pcol >> (shift + 1))
        off_mask = jnp.logical_and(pair, jnp.logical_not(same))
        tmp_p = [mm(ap, expand(jnp.where(off_mask, mp, 0.0))) for ap, mp in zip(inv_p, ms_p)]
        yield
        inv_p = [ap - mm(tp, expand(ap)) for ap, tp in zip(inv_p, tmp_p)]
        yield
        same = pair
        shift += 1
    a_split = [_split2(ap) for ap in inv_p]
    m_split = [_split2(mp) for mp in ms_p]
    e_hi = [_expand_blocks(a_hi, mask_bf) for a_hi, _ in a_split]
    e_lo = [_expand_blocks(a_lo, mask_bf) for _, a_lo in a_split]
    prod = [_mm(m_hi, eh) + (_mm(m_hi, el) + _mm(m_lo, eh)) for (m_hi, m_lo), eh, el in zip(m_split, e_hi, e_lo)]
    yield
    res = [expand(eye_p - (ap + pr)) for ap, pr in zip(inv_p, prod)]
    return [ap + _mm(a_hi, r) for ap, (a_hi, _), r in zip(inv_p, a_split, res)]


def _run(gen):
    try:
        while True:
            next(gen)
    except StopIteration as stop:
        return stop.value


def _interleave(*gens):
    live = list(gens)
    while live:
        for g in list(live):
            try:
                next(g)
            except StopIteration:
                live.remove(g)


def _ada_kernel(c_ref, w_ref, b_ref, o_ref):
    s = _silu(c_ref[...])
    o_ref[0] = _mm_x3(s, w_ref[0]) + b_ref[0]


def _ada_params(c, ada_w, ada_b):
    n_l, n_s, d, d3 = ada_w.shape
    n = n_l * n_s
    bsz = c.shape[0]
    tn = 1024
    return pl.pallas_call(
        _ada_kernel,
        out_shape=jax.ShapeDtypeStruct((n, bsz, d3), F32),
        grid=(n, d3 // tn),
        in_specs=[
            pl.BlockSpec((bsz, d), lambda i, j: (0, 0)),
            pl.BlockSpec((1, d, tn), lambda i, j: (i, 0, j)),
            pl.BlockSpec((1, 1, tn), lambda i, j: (i, 0, j)),
        ],
        out_specs=pl.BlockSpec((1, bsz, tn), lambda i, j: (i, 0, j)),
        compiler_params=pltpu.CompilerParams(vmem_limit_bytes=VMEM_LIMIT),
        name="ada_params",
    )(c, ada_w.reshape(n, d, d3), ada_b.reshape(n, 1, d3))


def _causal_conv(x, hist, cw):
    n = x.shape[0]
    xp = jnp.concatenate([hist, x], axis=0)
    y = x * cw[CONV_WIDTH - 1:CONV_WIDTH]
    for j in range(CONV_WIDTH - 1):
        o = HALO - (CONV_WIDTH - 1) + j
        y = y + xp[o:o + n] * cw[j:j + 1]
    return y


def _norm_proj_kernel(x_ref, nw_ref, shift_ref, scale_ref, w_ref, *refs, post, per_b):
    n_par = sum({"plain": 0, "conv_silu": 1, "conv_bias": 2, "shift_mix": 1}[p[0]] for p in post)
    par_refs = list(refs[:n_par])
    out_refs = refs[n_par:n_par + len(post)]
    hist_refs = list(refs[n_par + len(post):])

    @pl.when(pl.program_id(0) % per_b == 0)
    def _():
        for h_ref in hist_refs:
            h_ref[...] = jnp.zeros_like(h_ref)

    tm = x_ref.shape[0]
    halves = [slice(0, tm // 2), slice(tm // 2, tm)]
    hs = [((_rms(x_ref[rows, :]) * nw_ref[...]) * (1.0 + scale_ref[0]) + shift_ref[0]).astype(BF16)
          for rows in halves]
    off = 0
    for o_ref, p in zip(out_refs, post):
        n = o_ref.shape[1]
        ys = [_mm(h, w_ref[:, off:off + n]) for h in hs]
        off += n
        if p[0] == "plain":
            outs = ys
        else:
            h_ref = hist_refs.pop(0)
            hist = h_ref[...]
            h_ref[...] = ys[-1][tm // 2 - HALO:tm // 2]
            hists = [hist, ys[0][tm // 2 - HALO:tm // 2]]
            if p[0] == "conv_silu":
                cw = par_refs.pop(0)[...]
                outs = [jnp.concatenate([_silu(_causal_conv(y[:, 0:p[1]], hi[:, 0:p[1]], cw)), y[:, p[1]:n]], axis=1)
                        for y, hi in zip(ys, hists)]
            elif p[0] == "conv_bias":
                cw = par_refs.pop(0)[...]
                cb = par_refs.pop(0)[...]
                outs = [_causal_conv(y, hi, cw) + cb for y, hi in zip(ys, hists)]
            else:
                mu = par_refs.pop(0)[...]
                trow = lax.broadcasted_iota(jnp.int32, (tm // 2, 1), 0)
                outs = [y + mu * (jnp.where(trow == 0, hi[HALO - 1:HALO, :], pltpu.roll(y, 1, axis=0)) - y)
                        for y, hi in zip(ys, hists)]
        for rows, out in zip(halves, outs):
            o_ref[rows, :] = out


def _norm_proj(x2, nw, shift, scale, w_bf, splits, post, params, seq, tm=512):
    n_tok, d = x2.shape
    per_b = seq // tm
    const = lambda i: (0, 0)
    return pl.pallas_call(
        functools.partial(_norm_proj_kernel, post=post, per_b=per_b),
        out_shape=[jax.ShapeDtypeStruct((n_tok, n), F32) for n in splits],
        grid=(n_tok // tm,),
        in_specs=[
            pl.BlockSpec((tm, d), lambda i: (i, 0)),
            pl.BlockSpec((1, d), const),
            pl.BlockSpec((1, 1, d), lambda i: (i // per_b, 0, 0)),
            pl.BlockSpec((1, 1, d), lambda i: (i // per_b, 0, 0)),
            pl.BlockSpec(w_bf.shape, const, pipeline_mode=pl.Buffered(1)),
        ] + [pl.BlockSpec(p.shape, const) for p in params],
        out_specs=[pl.BlockSpec((tm, n), lambda i: (i, 0)) for n in splits],
        scratch_shapes=[pltpu.VMEM((HALO, n), F32) for n, p in zip(splits, post) if p[0] != "plain"],
        compiler_params=pltpu.CompilerParams(
            dimension_semantics=("arbitrary",), vmem_limit_bytes=VMEM_LIMIT),
        name="norm_proj",
    )(x2, nw, shift, scale, w_bf, *params)


def _out_ffn_kernel(*refs, n_in):
    x_ref = refs[0]
    a_refs = refs[1:1 + n_in]
    w_refs = refs[1 + n_in:1 + 2 * n_in]
    mod_ref, nw_ref, wg_ref, wu_ref, wd_ref, o_ref = refs[1 + 2 * n_in:]
    mod = mod_ref[0]
    nw = nw_ref[...]
    tm = x_ref.shape[0]
    halves = [slice(r, r + FFN_PART_ROWS) for r in range(0, tm, FFN_PART_ROWS)]
    ys = []
    for rows in halves:
        y = _mm(a_refs[0][rows, :].astype(BF16), w_refs[0][...])
        for a_ref, w_ref in zip(a_refs[1:], w_refs[1:]):
            y = y + _mm(a_ref[rows, :].astype(BF16), w_ref[...])
        ys.append(y)
    xs = [x_ref[rows, :] + mod[0:1] * (_rms(y) * nw[0:1]) for rows, y in zip(halves, ys)]
    hs = [((_rms(x) * nw[1:2]) * (1.0 + mod[2:3]) + mod[1:2]).astype(BF16) for x in xs]
    gs = [_mm(h, wg_ref[...]) for h in hs]
    us = [_mm(h, wu_ref[...]) for h in hs]
    acts = [(_silu(g) * u).astype(BF16) for g, u in zip(gs, us)]
    ys = [_mm(act, wd_ref[...]) for act in acts]
    for rows, x, y in zip(halves, xs, ys):
        o_ref[rows, :] = x + mod[3:4] * (_rms(y) * nw[2:3])


def _out_ffn(x2, acts, ws_bf, mod, nw, wg_bf, wu_bf, wd_bf, seq, tm=1024):
    n_tok, d = x2.shape
    per_b = seq // tm
    row = lambda i: (i, 0)
    const = lambda i: (0, 0)
    resident = lambda w: pl.BlockSpec(w.shape, const, pipeline_mode=pl.Buffered(1))
    in_specs = [pl.BlockSpec((tm, d), row)]
    in_specs += [pl.BlockSpec((tm, a.shape[1]), row) for a in acts]
    in_specs += [resident(w) for w in ws_bf]
    in_specs += [pl.BlockSpec((1,) + mod.shape[1:], lambda i: (i // per_b, 0, 0)),
                 pl.BlockSpec(nw.shape, const), resident(wg_bf), resident(wu_bf), resident(wd_bf)]
    return pl.pallas_call(
        functools.partial(_out_ffn_kernel, n_in=len(acts)),
        out_shape=jax.ShapeDtypeStruct((n_tok, d), F32),
        grid=(n_tok // tm,),
        in_specs=in_specs,
        out_specs=pl.BlockSpec((tm, d), row),
        compiler_params=pltpu.CompilerParams(vmem_limit_bytes=VMEM_LIMIT),
        name="out_ffn",
    )(x2, *acts, *ws_bf, mod, nw, wg_bf, wu_bf, wd_bf)


def _gdn_kernel(qkvz_ref, ab_ref, alog_ref, dtb_ref, nw_ref, o_ref,
                qkv_s, gc_s, u_s, w_s, qd_s, kd_s, attn_s, o_s, state_ref, *, tt, nb):
    W = GDN_WIDTH
    D = GDN_HEAD_DIM
    C = CHUNK
    H = GDN_HEADS
    n_c = tt // C

    @pl.when(pl.program_id(1) == 0)
    def _():
        state_ref[...] = jnp.zeros_like(state_ref)

    row, col = _tri_masks(tt)
    same_chunk = (row >> CHUNK_LOG2) == (col >> CHUNK_LOG2)
    causal = jnp.logical_and(same_chunk, row >= col)
    strict = jnp.logical_and(same_chunk, row > col)
    tril_bf = causal.astype(BF16)
    mask_bf = same_chunk.astype(BF16)
    ones_bf = jnp.ones((D, D), BF16)

    units = [(bi, h) for bi in range(nb) for h in range(H)]
    qs, ks, vs, betas, k_betas, decays, e_gs, e_rems = [], [], [], [], [], [], [], []
    for bi in range(nb):
        y = qkvz_ref[bi, :, 0:3 * W]
        for h in range(H):
            q = y[:, h * D:(h + 1) * D]
            k = y[:, W + h * D:W + (h + 1) * D]
            q = q * lax.rsqrt(_mm_const_rhs(q * q, ones_bf) + NORM_EPS) * (D ** -0.5)
            k = k * lax.rsqrt(_mm_const_rhs(k * k, ones_bf) + NORM_EPS)
            qkv_s[bi, :, h * D:(h + 1) * D] = q
            qkv_s[bi, :, W + h * D:W + (h + 1) * D] = k
        qkv_s[bi, :, 2 * W:3 * W] = y[:, 2 * W:3 * W]

        ab = ab_ref[bi]
        g = (-LOG2_E * jnp.exp(alog_ref[...])) * _softplus(ab + dtb_ref[...])
        beta_all = _sigmoid(ab)
        gc = _mm_const_lhs(tril_bf, g)
        gc_s[bi] = gc
        gc_t = gc.T
        g_last = jnp.concatenate(
            [jnp.broadcast_to(gc[c * C + C - 1:c * C + C, :], (C, LANES)) for c in range(n_c)], axis=0)
        e_g = jnp.exp2(gc)
        e_rem = jnp.exp2(g_last - gc)
        for h in range(H):
            qs.append(qkv_s[bi, :, h * D:(h + 1) * D])
            ks.append(qkv_s[bi, :, W + h * D:W + (h + 1) * D])
            vs.append(qkv_s[bi, :, 2 * W + h * D:2 * W + (h + 1) * D])
            betas.append(beta_all[:, H + h:H + h + 1])
            k_betas.append(ks[-1] * betas[-1])
            decays.append(
                jnp.where(causal, jnp.exp2(jnp.minimum(gc[:, h:h + 1] - gc_t[h:h + 1, :], 0.0)), 0.0))
            e_gs.append(e_g[:, h:h + 1])
            e_rems.append(e_rem[:, h:h + 1])

    ms = [jnp.where(strict, _mm_bf(kb, k, NT) * dc, 0.0) for kb, k, dc in zip(k_betas, ks, decays)]
    t_ps = _run(_unit_lower_inverses([_pack_blocks(m) for m in ms], mask_bf))
    for i in range(len(units)):
        rhs = jnp.concatenate([vs[i] * betas[i], k_betas[i] * e_gs[i]], axis=1)
        t_hi, t_lo = _split2(t_ps[i])
        rhs_hi, rhs_lo = _split2(rhs)
        t_hi = _expand_blocks(t_hi, mask_bf)
        uw = _mm(t_hi, rhs_hi) + (_mm(t_hi, rhs_lo) + _mm(_expand_blocks(t_lo, mask_bf), rhs_hi))
        u_s[i] = uw[:, 0:D]
        w_s[i] = uw[:, D:2 * D]
        attn_s[i] = _mm_bf(qs[i], ks[i], NT) * decays[i]
        qd_s[i] = qs[i] * e_gs[i]
        kd_s[i] = ks[i] * e_rems[i]

    for c in range(n_c):
        rows = slice(c * C, (c + 1) * C)
        win = slice((c // 2) * 2 * C, (c // 2 + 1) * 2 * C)
        ss = [state_ref[i] for i in range(len(units))]
        wq = [_mm_bf(jnp.concatenate([w_s[i, rows, :], qd_s[i, rows, :]], axis=0), s) for i, s in enumerate(ss)]
        v_new = [u_s[i, rows, :] - x[0:C] for i, x in enumerate(wq)]
        kv = [_mm_bf(kd_s[i, rows, :], x, TN) for i, x in enumerate(v_new)]
        for i, (bi, h) in enumerate(units):
            gl = jnp.exp2(gc_s[bi, c * C + C - 1:c * C + C, h:h + 1])
            state_ref[i] = ss[i] * gl + kv[i]
            vv = jnp.concatenate([v_new[i], v_new[i]], axis=0)
            o_s[bi, rows, h * D:(h + 1) * D] = wq[i][C:2 * C] + _mm_bf(attn_s[i, rows, win], vv)

    nw = nw_ref[...]
    for bi, h in units:
        z = qkvz_ref[bi, :, 3 * W + h * D:3 * W + (h + 1) * D]
        o_ref[bi, :, h * D:(h + 1) * D] = (_rms(o_s[bi, :, h * D:(h + 1) * D]) * nw * _silu(z)).astype(o_ref.dtype)


def _gdn(qkvz, ab, a_log_row, dt_row, norm_w, bsz, seq, tt=256, nb=2):
    W = GDN_WIDTH
    tile = lambda b, j: (b, j, 0)
    const = lambda b, j: (0, 0)
    n_u = nb * GDN_HEADS
    out = pl.pallas_call(
        functools.partial(_gdn_kernel, tt=tt, nb=nb),
        out_shape=jax.ShapeDtypeStruct((bsz, seq, W), BF16),
        grid=(bsz // nb, seq // tt),
        in_specs=[
            pl.BlockSpec((nb, tt, 4 * W), tile),
            pl.BlockSpec((nb, tt, LANES), tile),
            pl.BlockSpec((1, LANES), const),
            pl.BlockSpec((1, LANES), const),
            pl.BlockSpec((1, GDN_HEAD_DIM), const),
        ],
        out_specs=pl.BlockSpec((nb, tt, W), tile),
        scratch_shapes=[
            pltpu.VMEM((nb, tt, 3 * W), F32),
            pltpu.VMEM((nb, tt, LANES), F32),
            pltpu.VMEM((n_u, tt, GDN_HEAD_DIM), F32),
            pltpu.VMEM((n_u, tt, GDN_HEAD_DIM), F32),
            pltpu.VMEM((n_u, tt, GDN_HEAD_DIM), F32),
            pltpu.VMEM((n_u, tt, GDN_HEAD_DIM), F32),
            pltpu.VMEM((n_u, tt, tt), F32),
            pltpu.VMEM((nb, tt, W), F32),
            pltpu.VMEM((n_u, GDN_HEAD_DIM, GDN_HEAD_DIM), F32),
        ],
        compiler_params=pltpu.CompilerParams(
            dimension_semantics=("arbitrary", "arbitrary"), vmem_limit_bytes=VMEM_LIMIT),
        name="gdn",
    )(qkvz.reshape(bsz, seq, 4 * W), ab.reshape(bsz, seq, LANES), a_log_row, dt_row, norm_w)
    return out.reshape(bsz * seq, W)


def _rwkv_kernel(rw_ref, w0_ref, w2_ref, a0_ref, a2_ref, g2_ref, kk_ref, ka_ref, rk_ref,
                 lnw_ref, lnb_ref, o_ref,
                 r_s, k_s, b_s, kk_s, kend_s, bend_s, v_s, y_s, u_s, w_s, rv_s, bonus_s, gate_s,
                 lct_s, arb_s, state_ref, *, tt, nb):
    W = RWKV_WIDTH
    C = CHUNK
    NH = RWKV_HEADS
    P2 = 2 * RWKV_HEAD_DIM
    n_c = tt // C
    n_p = NH // 2

    @pl.when(pl.program_id(1) == 0)
    def _():
        state_ref[...] = jnp.zeros_like(state_ref)

    hrow, hcol = _tri_masks(MXU_DIM)
    head_ones = ((hrow >> RWKV_HEAD_LOG2) == (hcol >> RWKV_HEAD_LOG2)).astype(BF16)

    def head_sums(x):
        return jnp.concatenate(
            [_mm_const_rhs(x[:, o:o + MXU_DIM], head_ones) for o in range(0, W, MXU_DIM)], axis=1)

    row, col = _tri_masks(tt)
    tril_bf = jnp.logical_and((row >> CHUNK_LOG2) == (col >> CHUNK_LOG2), row >= col).astype(BF16)

    def prepass(bi):
        cf = rw_ref[bi]

        r = cf[:, 0:W]
        k = cf[:, W:2 * W]
        v = cf[:, 2 * W:3 * W]
        wd_ad = cf[:, 3 * W:3 * W + LANES]
        gd = cf[:, 3 * W + LANES:3 * W + 2 * LANES]

        w_in = w0_ref[...] + _mm_bf(jnp.tanh(wd_ad), w2_ref[...])
        a = _sigmoid(a0_ref[...] + _mm_bf(wd_ad, a2_ref[...]))
        yield
        gate_s[bi] = _mm_bf(_sigmoid(gd), g2_ref[...])
        kkv = k * kk_ref[...]
        yield
        kk = kkv * lax.rsqrt(head_sums(kkv * kkv) + NORM_EPS)
        kmod = k * (1.0 + (a - 1.0) * ka_ref[...])
        yield
        bonus_s[bi] = head_sums(r * kmod * rk_ref[...]) * v
        lw = (-math.exp(-0.5) * LOG2_E) * _sigmoid(w_in)
        yield
        lc = _mm_const_lhs(tril_bf, lw)
        lct_s[bi] = lc.T
        l_last = jnp.concatenate(
            [jnp.broadcast_to(lc[c * C + C - 1:c * C + C, :], (C, W)) for c in range(n_c)], axis=0)
        yield
        e_inv = jnp.exp2(-lc)
        e_rem = jnp.exp2(l_last - lc)
        b = kk * a
        r_s[bi] = (r * jnp.exp2(lc)).astype(BF16)
        yield
        k_s[bi] = (kmod * e_inv).astype(BF16)
        b_s[bi] = (b * e_inv).astype(BF16)
        yield
        kk_s[bi] = kk * jnp.exp2(lc - lw)
        kend_s[bi] = (kmod * e_rem).astype(BF16)
        yield
        bend_s[bi] = (b * e_rem).astype(BF16)
        v_s[bi] = v.astype(BF16)

    lane = lax.broadcasted_iota(jnp.int32, (1, P2), 1)
    first = lane < RWKV_HEAD_DIM
    rows_of = lambda c: slice(c * C, (c + 1) * C)
    lanes_of = lambda p: slice(p * P2, (p + 1) * P2)
    cut = lambda ref, u: ref[u[0], rows_of(u[2]), lanes_of(u[1])]
    slot = lambda u: (u[0] * n_p + u[1]) * n_c + u[2]

    def per_head_rows(x):
        zero = jnp.zeros_like(x)
        return jnp.concatenate([jnp.where(first, x, zero), jnp.where(first, zero, x)], axis=0)

    def chunk_form(seqs):
        prow = lax.broadcasted_iota(jnp.int32, (C, 2 * P2), 0)
        pcol = lax.broadcasted_iota(jnp.int32, (C, 2 * P2), 1) & (C - 1)
        strict_p = prow > pcol
        causal_p = prow >= pcol
        units = [(bi, p, c) for bi in seqs for p in range(n_p) for c in range(n_c)]

        prod = [_mm(jnp.concatenate([cut(kk_s, u).astype(BF16), cut(r_s, u)], axis=0),
                    jnp.concatenate([per_head_rows(cut(b_s, u)), per_head_rows(cut(k_s, u))], axis=0), NT)
                for u in units]
        kbkk = [jnp.where(strict_p, x[0:C], 0.0) for x in prod]
        rbrk = [jnp.where(causal_p, x[C:2 * C], 0.0) for x in prod]
        for u, x in zip(units, rbrk):
            arb_s[slot(u)] = x[:, 0:P2]
        yield
        avrv = [_mm(jnp.concatenate([x[:, P2:2 * P2], y[:, P2:2 * P2]], axis=0).astype(BF16),
                    per_head_rows(cut(v_s, u))) for x, y, u in zip(kbkk, rbrk, units)]
        for u, x in zip(units, avrv):
            rv_s[u[0], rows_of(u[2]), lanes_of(u[1])] = x[C:2 * C]
        yield
        prow2, pcol2 = _tri_masks(2 * P2)
        t_p = yield from _unit_lower_inverses(
            [jnp.concatenate([kbkk[i][:, 0:P2], kbkk[i + 1][:, 0:P2]], axis=1) for i in range(0, len(units), 2)],
            ((prow2 >> CHUNK_LOG2) == (pcol2 >> CHUNK_LOG2)).astype(BF16))
        first2 = jnp.concatenate([first, first], axis=1)

        def per_head_rows2(x):
            zero = jnp.zeros_like(x)
            return jnp.concatenate([jnp.where(first2, x, zero), jnp.where(first2, zero, x)], axis=0)

        for i, u in enumerate(units):
            t_hi, t_lo = _split2(t_p[i // 2][:, (i % 2) * P2:(i % 2 + 1) * P2])
            x_hi, x_lo = _split2(jnp.concatenate([avrv[i][0:C], cut(kk_s, u)], axis=1))
            x_hi = per_head_rows2(x_hi)
            uw = _mm(t_hi, x_hi) + (_mm(t_hi, per_head_rows2(x_lo)) + _mm(t_lo, x_hi))
            u_s[u[0], rows_of(u[2]), lanes_of(u[1])] = uw[:, 0:P2]
            w_s[u[0], rows_of(u[2]), lanes_of(u[1])] = uw[:, P2:2 * P2]

    def recurrence(seqs):
        brow, bcol = _tri_masks(P2)
        blockdiag = (brow >> RWKV_HEAD_LOG2) == (bcol >> RWKV_HEAD_LOG2)
        pairs = [(bi, p) for bi in seqs for p in range(n_p)]
        for c in range(n_c):
            us = [(bi, p, c) for bi, p in pairs]
            ss = [state_ref[bi * n_p + p] for bi, p in pairs]
            rw = [_mm(jnp.concatenate([cut(r_s, u), cut(w_s, u).astype(BF16)], axis=0), s.astype(BF16))
                  for u, s in zip(us, ss)]
            pm = [-(x[C:2 * C] + cut(u_s, u)) for u, x in zip(us, rw)]
            upd = [_mm(jnp.concatenate([cut(bend_s, u), cut(kend_s, u)], axis=0),
                       jnp.concatenate([x.astype(BF16), cut(v_s, u)], axis=0), TN) for u, x in zip(us, pm)]
            for i, (bi, p) in enumerate(pairs):
                g_col = jnp.exp2(lct_s[bi, p * P2:(p + 1) * P2, c * C + C - 1:c * C + C])
                state_ref[bi * n_p + p] = ss[i] * g_col + jnp.where(blockdiag, upd[i], 0.0)
                y_s[bi, rows_of(c), lanes_of(p)] = rw[i][0:C] + cut(rv_s, us[i]) + _mm_bf(
                    arb_s[slot(us[i])], per_head_rows(pm[i]))

    _run(prepass(0))
    for bi in range(nb):
        _interleave(chunk_form([bi]), *([prepass(bi + 1)] if bi + 1 < nb else []))
    recurrence(list(range(nb)))

    inv_n = 1.0 / RWKV_HEAD_DIM
    for bi in range(nb):
        y = y_s[bi]
        mean = head_sums(y) * inv_n
        d = y - mean
        var = head_sums(d * d) * inv_n
        yn = d * lax.rsqrt(var + RWKV_GN_EPS) * lnw_ref[...] + lnb_ref[...]
        o_ref[bi] = ((yn + bonus_s[bi]) * gate_s[bi]).astype(o_ref.dtype)


def _rwkv(rw, w0, w2p, a0, a2p, g2, k_k, k_a, r_k, ln_w, ln_b, bsz, seq, tt=256, nb=2):
    cols = rw.shape[1]
    W = RWKV_WIDTH
    tile = lambda b, j: (b, j, 0)
    const = lambda b, j: (0, 0)
    vec = pl.BlockSpec((1, W), const)
    out = pl.pallas_call(
        functools.partial(_rwkv_kernel, tt=tt, nb=nb),
        out_shape=jax.ShapeDtypeStruct((bsz, seq, W), BF16),
        grid=(bsz // nb, seq // tt),
        in_specs=[
            pl.BlockSpec((nb, tt, cols), tile),
            vec,
            pl.BlockSpec(w2p.shape, const),
            vec,
            pl.BlockSpec(a2p.shape, const),
            pl.BlockSpec(g2.shape, const),
            vec, vec, vec, vec, vec,
        ],
        out_specs=pl.BlockSpec((nb, tt, W), tile),
        scratch_shapes=[pltpu.VMEM((nb, tt, W), dt) for dt in (BF16, BF16, BF16, F32, BF16, BF16, BF16) + (F32,) * 6]
        + [pltpu.VMEM((nb, W, tt), F32),
           pltpu.VMEM((nb * (RWKV_HEADS // 2) * (tt // CHUNK), CHUNK, 2 * RWKV_HEAD_DIM), F32),
           pltpu.VMEM((nb * RWKV_HEADS // 2, 2 * RWKV_HEAD_DIM, 2 * RWKV_HEAD_DIM), F32)],
        compiler_params=pltpu.CompilerParams(
            dimension_semantics=("arbitrary", "arbitrary"), vmem_limit_bytes=VMEM_LIMIT),
        name="rwkv7",
    )(rw.reshape(bsz, seq, cols), w0, w2p, a0, a2p, g2, k_k, k_a, r_k, ln_w, ln_b)
    return out.reshape(bsz * seq, W)


def _rglru_kernel(gb_ref, xb_ref, wa_ref, ba_ref, wx_ref, bx_ref, lam_ref, o_ref, h_ref, *, tt):
    @pl.when(pl.program_id(1) == 0)
    def _():
        h_ref[...] = jnp.zeros_like(h_ref)

    xc = xb_ref[...]

    blk = xc.shape[1] // LRU_BLOCKS
    xc_bf = xc.astype(BF16)
    ra = jnp.concatenate(
        [_mm(xc_bf[:, n * blk:(n + 1) * blk], wa_ref[n]) for n in range(LRU_BLOCKS)], axis=1)
    ix = jnp.concatenate(
        [_mm(xc_bf[:, n * blk:(n + 1) * blk], wx_ref[n]) for n in range(LRU_BLOCKS)], axis=1)
    r = _sigmoid(ra + ba_ref[...])
    i = _sigmoid(ix + bx_ref[...])
    a = jnp.exp2(r * ((-LRU_C * LOG2_E) * _softplus(-lam_ref[...])))
    u = xc * i * jnp.sqrt(1.0 - a * a)

    width = xc.shape[1]
    a = a.reshape(tt // SUB, SUB, width)
    u = u.reshape(tt // SUB, SUB, width)
    sub = lax.broadcasted_iota(jnp.int32, (1, SUB, 1), 1)
    d = 1
    while d < SUB:
        keep = sub >= d
        a_sh = jnp.where(keep, pltpu.roll(a, d, axis=1), 1.0)
        u_sh = jnp.where(keep, pltpu.roll(u, d, axis=1), 0.0)
        u = a * u_sh + u
        a = a * a_sh
        d *= 2
    gate = _gelu_tanh(gb_ref[...])
    carry = h_ref[...]
    pack = 2
    for g0 in range(0, tt // SUB, pack):
        outs = []
        for g in range(g0, g0 + pack):
            h = u[g] + a[g] * carry
            outs.append(h * gate[g * SUB:(g + 1) * SUB])
            carry = jnp.broadcast_to(h[SUB - 1:SUB, :], h.shape)
        o_ref[g0 * SUB:(g0 + pack) * SUB, :] = jnp.concatenate(outs, axis=0).astype(o_ref.dtype)
    h_ref[...] = carry


def _rglru(gb, xb, wa_bf, ba, wx_bf, bx, lam, bsz, seq, tt=256):
    n_tok, width = xb.shape
    per_b = seq // tt
    row = lambda b, j: (b * per_b + j, 0)
    const = lambda b, j: (0, 0)
    const3 = lambda b, j: (0, 0, 0)
    vec = pl.BlockSpec((1, width), const)
    return pl.pallas_call(
        functools.partial(_rglru_kernel, tt=tt),
        out_shape=jax.ShapeDtypeStruct((n_tok, width), BF16),
        grid=(bsz, per_b),
        in_specs=[
            pl.BlockSpec((tt, width), row),
            pl.BlockSpec((tt, width), row),
            pl.BlockSpec(wa_bf.shape, const3),
            vec,
            pl.BlockSpec(wx_bf.shape, const3),
            vec,
            vec,
        ],
        out_specs=pl.BlockSpec((tt, width), row),
        scratch_shapes=[pltpu.VMEM((HALO, width), F32)],
        compiler_params=pltpu.CompilerParams(
            dimension_semantics=("arbitrary", "arbitrary"), vmem_limit_bytes=VMEM_LIMIT),
        name="rglru",
    )(gb, xb, wa_bf, ba, wx_bf, bx, lam)


def _pad_lanes(v):
    return jnp.pad(v, (0, LANES - v.shape[0])).reshape(1, LANES)


def kernel(x, c, norm_pre, norm_post, ada_w, ada_b, ffn_w_gate, ffn_w_up, ffn_w_down, mix_w_in, mix_w_out, gdn_conv_w, gdn_a_log, gdn_dt_bias, gdn_norm_w, rwkv_mu, rwkv_w0, rwkv_w2, rwkv_a0, rwkv_a2, rwkv_g2, rwkv_k_k, rwkv_k_a, rwkv_r_k, rwkv_ln_w, rwkv_ln_b, lru_w_in, lru_conv_w, lru_conv_b, lru_wa, lru_ba, lru_wx, lru_bx, lru_lambda, lru_w_out):
    bsz, seq, d = x.shape
    depth = norm_pre.shape[0]
    x2 = x.reshape(bsz * seq, d)
    mods = _ada_params(c, ada_w, ada_b)

    def mod(layer, sub):
        m = mods[layer * 2 + sub]
        return (m[:, None, 0:d], m[:, None, d:2 * d], m[:, None, 2 * d:3 * d])

    GW = GDN_WIDTH
    for layer in range(depth):
        j = layer // 2
        shift, scale, gate = mod(layer, 0)
        shift2, scale2, gate2 = mod(layer, 1)
        mod_rows = jnp.concatenate([gate, shift2, scale2, gate2], axis=1)
        nw_rows = jnp.stack([norm_post[layer, 0], norm_pre[layer, 1], norm_post[layer, 1]])
        nw_pre = norm_pre[layer, 0].reshape(1, d)
        ffn_w = (ffn_w_gate[layer].astype(BF16), ffn_w_up[layer].astype(BF16), ffn_w_down[layer].astype(BF16))
        if layer % 2 == 0:
            w_in = mix_w_in[j].astype(BF16)
            n_gdn = 4 * GW + 2 * GDN_HEADS
            w_cat = jnp.concatenate(
                [w_in[:, 0:4 * GW], w_in[:, n_gdn:],
                 jnp.pad(w_in[:, 4 * GW:n_gdn], ((0, 0), (0, LANES - 2 * GDN_HEADS)))], axis=1)
            qkvz, rw, ab = _norm_proj(
                x2, nw_pre, shift, scale, w_cat, (4 * GW, RWKV_COLS, LANES),
                (("conv_silu", 3 * GW), ("shift_mix",), ("plain",)),
                (gdn_conv_w[j], rwkv_mu[j].reshape(1, RWKV_COLS)), seq)
            out_a = _gdn(qkvz, ab, _pad_lanes(gdn_a_log[j]), _pad_lanes(gdn_dt_bias[j]),
                         gdn_norm_w[j].reshape(1, GDN_HEAD_DIM), bsz, seq)
            w2p = jnp.pad(rwkv_w2[j], ((0, LANES - DECAY_LORA), (0, 0))).astype(BF16)
            a2p = jnp.pad(rwkv_a2[j], ((DECAY_LORA, LANES - DECAY_LORA - AAA_LORA), (0, 0))).astype(BF16)
            vec = lambda t: t.reshape(1, RWKV_WIDTH)
            out_b = _rwkv(rw, vec(rwkv_w0[j]), w2p, vec(rwkv_a0[j]), a2p,
                          rwkv_g2[j].astype(BF16), vec(rwkv_k_k[j]), vec(rwkv_k_a[j]), vec(rwkv_r_k[j]),
                          vec(rwkv_ln_w[j]), vec(rwkv_ln_b[j]), bsz, seq)
            w_out = mix_w_out[j].astype(BF16)
            x2 = _out_ffn(x2, [out_a, out_b], [w_out[0:GW], w_out[GW:]], mod_rows, nw_rows, *ffn_w, seq)
        else:
            width = lru_w_in.shape[2] // 2
            vec = lambda t: t.reshape(1, width)
            gb, xb = _norm_proj(x2, nw_pre, shift, scale, lru_w_in[j].astype(BF16), (width, width),
                                (("plain",), ("conv_bias",)), (lru_conv_w[j], vec(lru_conv_b[j])), seq)
            y = _rglru(gb, xb, lru_wa[j].astype(BF16), vec(lru_ba[j]),
                       lru_wx[j].astype(BF16), vec(lru_bx[j]), vec(lru_lambda[j]), bsz, seq)
            x2 = _out_ffn(x2, [y], [lru_w_out[j].astype(BF16)], mod_rows, nw_rows, *ffn_w, seq)
    return x2.reshape(bsz, seq, d)
```

```python
import functools
import math

import jax
import jax.numpy as jnp
from jax import lax
from jax.experimental import pallas as pl
from jax.experimental.pallas import tpu as pltpu

F32 = jnp.float32
BF16 = jnp.bfloat16

NORM_EPS = 1e-6
LOG2_E = 1.4426950408889634
GDN_HEADS = 4
GDN_HEAD_DIM = 128
GDN_WIDTH = GDN_HEADS * GDN_HEAD_DIM
CHUNK_LOG2 = 6
CHUNK = 1 << CHUNK_LOG2
BASE_LOG2 = 3
CONV_WIDTH = 4
RWKV_HEADS = 8
RWKV_HEAD_LOG2 = 6
RWKV_HEAD_DIM = 1 << RWKV_HEAD_LOG2
RWKV_WIDTH = RWKV_HEADS * RWKV_HEAD_DIM
DECAY_LORA = 64
AAA_LORA = 64
GATE_LORA = 128
RWKV_COLS = 3 * RWKV_WIDTH + DECAY_LORA + AAA_LORA + GATE_LORA
RWKV_GN_EPS = 64e-5
LRU_BLOCKS = 4
LRU_C = 8.0
LANES = 128
MXU_DIM = 256
FFN_PART_ROWS = 256
SUB = 8
HALO = SUB
VMEM_LIMIT = 56 * 1024 * 1024

NN = (((1,), (0,)), ((), ()))
NT = (((1,), (1,)), ((), ()))
TN = (((0,), (0,)), ((), ()))


def _mm(a, b, dims=NN):
    return lax.dot_general(a, b, dims, preferred_element_type=F32)


def _mm_bf(a, b, dims=NN):
    return _mm(a.astype(BF16), b.astype(BF16), dims)


def _split2(x):
    hi = x.astype(BF16)
    lo = (x - hi.astype(F32)).astype(BF16)
    return hi, lo


def _mm_x3(a, b, dims=NN):
    ah, al = _split2(a)
    bh, bl = _split2(b)
    return _mm(ah, bh, dims) + (_mm(ah, bl, dims) + _mm(al, bh, dims))


def _mm_const_lhs(c_bf, x):
    hi = x.astype(BF16)
    r1 = x - hi.astype(F32)
    mid = r1.astype(BF16)
    lo = (r1 - mid.astype(F32)).astype(BF16)
    return _mm(c_bf, hi) + (_mm(c_bf, mid) + _mm(c_bf, lo))


def _mm_const_rhs(x, c_bf):
    hi, lo = _split2(x)
    return _mm(hi, c_bf) + _mm(lo, c_bf)


def _sigmoid(x):
    return 0.5 * jnp.tanh(0.5 * x) + 0.5


def _silu(x):
    h = 0.5 * x
    return h * jnp.tanh(h) + h


def _softplus(x):
    return jnp.maximum(x, 0.0) + jnp.log1p(jnp.exp(-jnp.abs(x)))


def _gelu_tanh(x):
    c = 0.7978845608028654
    h = 0.5 * x
    return h * jnp.tanh(x * (c + (c * 0.044715) * (x * x))) + h


def _rms(x):
    return x * lax.rsqrt(jnp.mean(x * x, axis=-1, keepdims=True) + NORM_EPS)


def _tri_masks(n):
    row = lax.broadcasted_iota(jnp.int32, (n, n), 0)
    col = lax.broadcasted_iota(jnp.int32, (n, n), 1)
    return row, col


def _pack_blocks(x):
    out = x[0:CHUNK]
    for c in range(1, x.shape[0] // CHUNK):
        out = out + x[c * CHUNK:(c + 1) * CHUNK]
    return out


def _expand_blocks(xp_bf, mask_bf):
    return jnp.concatenate([xp_bf] * (xp_bf.shape[1] // CHUNK), axis=0) * mask_bf


def _unit_lower_inverses(ms_p, mask_bf):
    n = ms_p[0].shape[1]
    prow = lax.broadcasted_iota(jnp.int32, (CHUNK, n), 0)
    pcol = lax.broadcasted_iota(jnp.int32, (CHUNK, n), 1) & (CHUNK - 1)
    eye_p = (prow == pcol).astype(F32)
    expand = lambda xp: _expand_blocks(xp.astype(BF16), mask_bf)
    mm = lambda xp, y_bf: _mm(xp.astype(BF16), y_bf)
    same = (prow >> BASE_LOG2) == (pcol >> BASE_LOG2)
    m8_p = [jnp.where(same, mp, 0.0) for mp in ms_p]
    m2_p = [mm(ap, expand(ap)) for ap in m8_p]
    yield
    m4_p = [mm(ap, expand(ap)) for ap in m2_p]
    yield
    inv_p = [mm(eye_p - ap, expand(eye_p + bp)) for ap, bp in zip(m8_p, m2_p)]
    yield
    inv_p = [mm(ap, expand(eye_p + bp)) for ap, bp in zip(inv_p, m4_p)]
    yield
    shift = BASE_LOG2
    while shift < CHUNK_LOG2:
        pair = (prow >> (shift + 1)) == (pcol >> (shift + 1))
        off_mask = jnp.logical_and(pair, jnp.logical_not(same))
        tmp_p = [mm(ap, expand(jnp.where(off_mask, mp, 0.0))) for ap, mp in zip(inv_p, ms_p)]
        yield
        inv_p = [ap - mm(tp, expand(ap)) for ap, tp in zip(inv_p, tmp_p)]
        yield
        same = pair
        shift += 1
    a_split = [_split2(ap) for ap in inv_p]
    m_split = [_split2(mp) for mp in ms_p]
    e_hi = [_expand_blocks(a_hi, mask_bf) for a_hi, _ in a_split]
    e_lo = [_expand_blocks(a_lo, mask_bf) for _, a_lo in a_split]
    prod = [_mm(m_hi, eh) + (_mm(m_hi, el) + _mm(m_lo, eh)) for (m_hi, m_lo), eh, el in zip(m_split, e_hi, e_lo)]
    yield
    res = [expand(eye_p - (ap + pr)) for ap, pr in zip(inv_p, prod)]
    return [ap + _mm(a_hi, r) for ap, (a_hi, _), r in zip(inv_p, a_split, res)]


def _run(gen):
    try:
        while True:
            next(gen)
    except StopIteration as stop:
        return stop.value


def _interleave(*gens):
    live = list(gens)
    while live:
        for g in list(live):
            try:
                next(g)
            except StopIteration:
                live.remove(g)


def _ada_kernel(c_ref, w_ref, b_ref, o_ref):
    s = _silu(c_ref[...])
    o_ref[0] = _mm_x3(s, w_ref[0]) + b_ref[0]


def _ada_params(c, ada_w, ada_b):
    n_l, n_s, d, d3 = ada_w.shape
    n = n_l * n_s
    bsz = c.shape[0]
    tn = 1024
    return pl.pallas_call(
        _ada_kernel,
        out_shape=jax.ShapeDtypeStruct((n, bsz, d3), F32),
        grid=(n, d3 // tn),
        in_specs=[
            pl.BlockSpec((bsz, d), lambda i, j: (0, 0)),
            pl.BlockSpec((1, d, tn), lambda i, j: (i, 0, j)),
            pl.BlockSpec((1, 1, tn), lambda i, j: (i, 0, j)),
        ],
        out_specs=pl.BlockSpec((1, bsz, tn), lambda i, j: (i, 0, j)),
        compiler_params=pltpu.CompilerParams(vmem_limit_bytes=VMEM_LIMIT),
        name="ada_params",
    )(c, ada_w.reshape(n, d, d3), ada_b.reshape(n, 1, d3))


def _causal_conv(x, hist, cw):
    n = x.shape[0]
    xp = jnp.concatenate([hist, x], axis=0)
    y = x * cw[CONV_WIDTH - 1:CONV_WIDTH]
    for j in range(CONV_WIDTH - 1):
        o = HALO - (CONV_WIDTH - 1) + j
        y = y + xp[o:o + n] * cw[j:j + 1]
    return y


def _norm_proj_kernel(x_ref, nw_ref, shift_ref, scale_ref, w_ref, *refs, post, per_b):
    n_par = sum({"plain": 0, "conv_silu": 1, "conv_bias": 2, "shift_mix": 1}[p[0]] for p in post)
    par_refs = list(refs[:n_par])
    out_refs = refs[n_par:n_par + len(post)]
    hist_refs = list(refs[n_par + len(post):])

    @pl.when(pl.program_id(0) % per_b == 0)
    def _():
        for h_ref in hist_refs:
            h_ref[...] = jnp.zeros_like(h_ref)

    tm = x_ref.shape[0]
    halves = [slice(0, tm // 2), slice(tm // 2, tm)]
    hs = [((_rms(x_ref[rows, :]) * nw_ref[...]) * (1.0 + scale_ref[0]) + shift_ref[0]).astype(BF16)
          for rows in halves]
    off = 0
    for o_ref, p in zip(out_refs, post):
        n = o_ref.shape[1]
        ys = [_mm(h, w_ref[:, off:off + n]) for h in hs]
        off += n
        if p[0] == "plain":
            outs = ys
        else:
            h_ref = hist_refs.pop(0)
            hist = h_ref[...]
            h_ref[...] = ys[-1][tm // 2 - HALO:tm // 2]
            hists = [hist, ys[0][tm // 2 - HALO:tm // 2]]
            if p[0] == "conv_silu":
                cw = par_refs.pop(0)[...]
                outs = [jnp.concatenate([_silu(_causal_conv(y[:, 0:p[1]], hi[:, 0:p[1]], cw)), y[:, p[1]:n]], axis=1)
                        for y, hi in zip(ys, hists)]
            elif p[0] == "conv_bias":
                cw = par_refs.pop(0)[...]
                cb = par_refs.pop(0)[...]
                outs = [_causal_conv(y, hi, cw) + cb for y, hi in zip(ys, hists)]
            else:
                mu = par_refs.pop(0)[...]
                trow = lax.broadcasted_iota(jnp.int32, (tm // 2, 1), 0)
                outs = [y + mu * (jnp.where(trow == 0, hi[HALO - 1:HALO, :], pltpu.roll(y, 1, axis=0)) - y)
                        for y, hi in zip(ys, hists)]
        for rows, out in zip(halves, outs):
            o_ref[rows, :] = out


def _norm_proj(x2, nw, shift, scale, w_bf, splits, post, params, seq, tm=512):
    n_tok, d = x2.shape
    per_b = seq // tm
    const = lambda i: (0, 0)
    return pl.pallas_call(
        functools.partial(_norm_proj_kernel, post=post, per_b=per_b),
        out_shape=[jax.ShapeDtypeStruct((n_tok, n), F32) for n in splits],
        grid=(n_tok // tm,),
        in_specs=[
            pl.BlockSpec((tm, d), lambda i: (i, 0)),
            pl.BlockSpec((1, d), const),
            pl.BlockSpec((1, 1, d), lambda i: (i // per_b, 0, 0)),
            pl.BlockSpec((1, 1, d), lambda i: (i // per_b, 0, 0)),
            pl.BlockSpec(w_bf.shape, const, pipeline_mode=pl.Buffered(1)),
        ] + [pl.BlockSpec(p.shape, const) for p in params],
        out_specs=[pl.BlockSpec((tm, n), lambda i: (i, 0)) for n in splits],
        scratch_shapes=[pltpu.VMEM((HALO, n), F32) for n, p in zip(splits, post) if p[0] != "plain"],
        compiler_params=pltpu.CompilerParams(
            dimension_semantics=("arbitrary",), vmem_limit_bytes=VMEM_LIMIT),
        name="norm_proj",
    )(x2, nw, shift, scale, w_bf, *params)


def _out_ffn_kernel(*refs, n_in):
    x_ref = refs[0]
    a_refs = refs[1:1 + n_in]
    w_refs = refs[1 + n_in:1 + 2 * n_in]
    mod_ref, nw_ref, wg_ref, wu_ref, wd_ref, o_ref = refs[1 + 2 * n_in:]
    mod = mod_ref[0]
    nw = nw_ref[...]
    tm = x_ref.shape[0]
    halves = [slice(r, r + FFN_PART_ROWS) for r in range(0, tm, FFN_PART_ROWS)]
    ys = []
    for rows in halves:
        y = _mm(a_refs[0][rows, :].astype(BF16), w_refs[0][...])
        for a_ref, w_ref in zip(a_refs[1:], w_refs[1:]):
            y = y + _mm(a_ref[rows, :].astype(BF16), w_ref[...])
        ys.append(y)
    xs = [x_ref[rows, :] + mod[0:1] * (_rms(y) * nw[0:1]) for rows, y in zip(halves, ys)]
    hs = [((_rms(x) * nw[1:2]) * (1.0 + mod[2:3]) + mod[1:2]).astype(BF16) for x in xs]
    gs = [_mm(h, wg_ref[...]) for h in hs]
    us = [_mm(h, wu_ref[...]) for h in hs]
    acts = [(_silu(g) * u).astype(BF16) for g, u in zip(gs, us)]
    ys = [_mm(act, wd_ref[...]) for act in acts]
    for rows, x, y in zip(halves, xs, ys):
        o_ref[rows, :] = x + mod[3:4] * (_rms(y) * nw[2:3])


def _out_ffn(x2, acts, ws_bf, mod, nw, wg_bf, wu_bf, wd_bf, seq, tm=1024):
    n_tok, d = x2.shape
    per_b = seq // tm
    row = lambda i: (i, 0)
    const = lambda i: (0, 0)
    resident = lambda w: pl.BlockSpec(w.shape, const, pipeline_mode=pl.Buffered(1))
    in_specs = [pl.BlockSpec((tm, d), row)]
    in_specs += [pl.BlockSpec((tm, a.shape[1]), row) for a in acts]
    in_specs += [resident(w) for w in ws_bf]
    in_specs += [pl.BlockSpec((1,) + mod.shape[1:], lambda i: (i // per_b, 0, 0)),
                 pl.BlockSpec(nw.shape, const), resident(wg_bf), resident(wu_bf), resident(wd_bf)]
    return pl.pallas_call(
        functools.partial(_out_ffn_kernel, n_in=len(acts)),
        out_shape=jax.ShapeDtypeStruct((n_tok, d), F32),
        grid=(n_tok // tm,),
        in_specs=in_specs,
        out_specs=pl.BlockSpec((tm, d), row),
        compiler_params=pltpu.CompilerParams(vmem_limit_bytes=VMEM_LIMIT),
        name="out_ffn",
    )(x2, *acts, *ws_bf, mod, nw, wg_bf, wu_bf, wd_bf)


def _gdn_kernel(qkvz_ref, ab_ref, alog_ref, dtb_ref, nw_ref, o_ref,
                qkv_s, gc_s, u_s, w_s, qd_s, kd_s, attn_s, o_s, state_ref, *, tt, nb):
    W = GDN_WIDTH
    D = GDN_HEAD_DIM
    C = CHUNK
    H = GDN_HEADS
    n_c = tt // C

    @pl.when(pl.program_id(1) == 0)
    def _():
        state_ref[...] = jnp.zeros_like(state_ref)

    row, col = _tri_masks(tt)
    same_chunk = (row >> CHUNK_LOG2) == (col >> CHUNK_LOG2)
    causal = jnp.logical_and(same_chunk, row >= col)
    strict = jnp.logical_and(same_chunk, row > col)
    tril_bf = causal.astype(BF16)
    mask_bf = same_chunk.astype(BF16)
    ones_bf = jnp.ones((D, D), BF16)

    units = [(bi, h) for bi in range(nb) for h in range(H)]
    qs, ks, vs, betas, k_betas, decays, e_gs, e_rems = [], [], [], [], [], [], [], []
    for bi in range(nb):
        y = qkvz_ref[bi, :, 0:3 * W]
        for h in range(H):
            q = y[:, h * D:(h + 1) * D]
            k = y[:, W + h * D:W + (h + 1) * D]
            q = q * lax.rsqrt(_mm_const_rhs(q * q, ones_bf) + NORM_EPS) * (D ** -0.5)
            k = k * lax.rsqrt(_mm_const_rhs(k * k, ones_bf) + NORM_EPS)
            qkv_s[bi, :, h * D:(h + 1) * D] = q
            qkv_s[bi, :, W + h * D:W + (h + 1) * D] = k
        qkv_s[bi, :, 2 * W:3 * W] = y[:, 2 * W:3 * W]

        ab = ab_ref[bi]
        g = (-LOG2_E * jnp.exp(alog_ref[...])) * _softplus(ab + dtb_ref[...])
        beta_all = _sigmoid(ab)
        gc = _mm_const_lhs(tril_bf, g)
        gc_s[bi] = gc
        gc_t = gc.T
        g_last = jnp.concatenate(
            [jnp.broadcast_to(gc[c * C + C - 1:c * C + C, :], (C, LANES)) for c in range(n_c)], axis=0)
        e_g = jnp.exp2(gc)
        e_rem = jnp.exp2(g_last - gc)
        for h in range(H):
            qs.append(qkv_s[bi, :, h * D:(h + 1) * D])
            ks.append(qkv_s[bi, :, W + h * D:W + (h + 1) * D])
            vs.append(qkv_s[bi, :, 2 * W + h * D:2 * W + (h + 1) * D])
            betas.append(beta_all[:, H + h:H + h + 1])
            k_betas.append(ks[-1] * betas[-1])
            decays.append(
                jnp.where(causal, jnp.exp2(jnp.minimum(gc[:, h:h + 1] - gc_t[h:h + 1, :], 0.0)), 0.0))
            e_gs.append(e_g[:, h:h + 1])
            e_rems.append(e_rem[:, h:h + 1])

    ms = [jnp.where(strict, _mm_bf(kb, k, NT) * dc, 0.0) for kb, k, dc in zip(k_betas, ks, decays)]
    t_ps = _run(_unit_lower_inverses([_pack_blocks(m) for m in ms], mask_bf))
    for i in range(len(units)):
        rhs = jnp.concatenate([vs[i] * betas[i], k_betas[i] * e_gs[i]], axis=1)
        t_hi, t_lo = _split2(t_ps[i])
        rhs_hi, rhs_lo = _split2(rhs)
        t_hi = _expand_blocks(t_hi, mask_bf)
        uw = _mm(t_hi, rhs_hi) + (_mm(t_hi, rhs_lo) + _mm(_expand_blocks(t_lo, mask_bf), rhs_hi))
        u_s[i] = uw[:, 0:D]
        w_s[i] = uw[:, D:2 * D]
        attn_s[i] = _mm_bf(qs[i], ks[i], NT) * decays[i]
        qd_s[i] = qs[i] * e_gs[i]
        kd_s[i] = ks[i] * e_rems[i]

    for c in range(n_c):
        rows = slice(c * C, (c + 1) * C)
        win = slice((c // 2) * 2 * C, (c // 2 + 1) * 2 * C)
        ss = [state_ref[i] for i in range(len(units))]
        wq = [_mm_bf(jnp.concatenate([w_s[i, rows, :], qd_s[i, rows, :]], axis=0), s) for i, s in enumerate(ss)]
        v_new = [u_s[i, rows, :] - x[0:C] for i, x in enumerate(wq)]
        kv = [_mm_bf(kd_s[i, rows, :], x, TN) for i, x in enumerate(v_new)]
        for i, (bi, h) in enumerate(units):
            gl = jnp.exp2(gc_s[bi, c * C + C - 1:c * C + C, h:h + 1])
            state_ref[i] = ss[i] * gl + kv[i]
            vv = jnp.concatenate([v_new[i], v_new[i]], axis=0)
            o_s[bi, rows, h * D:(h + 1) * D] = wq[i][C:2 * C] + _mm_bf(attn_s[i, rows, win], vv)

    nw = nw_ref[...]
    for bi, h in units:
        z = qkvz_ref[bi, :, 3 * W + h * D:3 * W + (h + 1) * D]
        o_ref[bi, :, h * D:(h + 1) * D] = (_rms(o_s[bi, :, h * D:(h + 1) * D]) * nw * _silu(z)).astype(o_ref.dtype)


def _gdn(qkvz, ab, a_log_row, dt_row, norm_w, bsz, seq, tt=256, nb=4):
    W = GDN_WIDTH
    tile = lambda b, j: (b, j, 0)
    const = lambda b, j: (0, 0)
    n_u = nb * GDN_HEADS
    out = pl.pallas_call(
        functools.partial(_gdn_kernel, tt=tt, nb=nb),
        out_shape=jax.ShapeDtypeStruct((bsz, seq, W), BF16),
        grid=(bsz // nb, seq // tt),
        in_specs=[
            pl.BlockSpec((nb, tt, 4 * W), tile),
            pl.BlockSpec((nb, tt, LANES), tile),
            pl.BlockSpec((1, LANES), const),
            pl.BlockSpec((1, LANES), const),
            pl.BlockSpec((1, GDN_HEAD_DIM), const),
        ],
        out_specs=pl.BlockSpec((nb, tt, W), tile),
        scratch_shapes=[
            pltpu.VMEM((nb, tt, 3 * W), F32),
            pltpu.VMEM((nb, tt, LANES), F32),
            pltpu.VMEM((n_u, tt, GDN_HEAD_DIM), F32),
            pltpu.VMEM((n_u, tt, GDN_HEAD_DIM), F32),
            pltpu.VMEM((n_u, tt, GDN_HEAD_DIM), F32),
            pltpu.VMEM((n_u, tt, GDN_HEAD_DIM), F32),
            pltpu.VMEM((n_u, tt, tt), F32),
            pltpu.VMEM((nb, tt, W), F32),
            pltpu.VMEM((n_u, GDN_HEAD_DIM, GDN_HEAD_DIM), F32),
        ],
        compiler_params=pltpu.CompilerParams(
            dimension_semantics=("arbitrary", "arbitrary"), vmem_limit_bytes=VMEM_LIMIT),
        name="gdn",
    )(qkvz.reshape(bsz, seq, 4 * W), ab.reshape(bsz, seq, LANES), a_log_row, dt_row, norm_w)
    return out.reshape(bsz * seq, W)


def _rwkv_kernel(rw_ref, w0_ref, w2_ref, a0_ref, a2_ref, g2_ref, kk_ref, ka_ref, rk_ref,
                 lnw_ref, lnb_ref, o_ref,
                 r_s, k_s, b_s, kk_s, kend_s, bend_s, v_s, y_s, u_s, w_s, rv_s, bonus_s, gate_s,
                 lct_s, arb_s, state_ref, *, tt, nb):
    W = RWKV_WIDTH
    C = CHUNK
    NH = RWKV_HEADS
    P2 = 2 * RWKV_HEAD_DIM
    n_c = tt // C
    n_p = NH // 2

    @pl.when(pl.program_id(1) == 0)
    def _():
        state_ref[...] = jnp.zeros_like(state_ref)

    hrow, hcol = _tri_masks(MXU_DIM)
    head_ones = ((hrow >> RWKV_HEAD_LOG2) == (hcol >> RWKV_HEAD_LOG2)).astype(BF16)

    def head_sums(x):
        return jnp.concatenate(
            [_mm_const_rhs(x[:, o:o + MXU_DIM], head_ones) for o in range(0, W, MXU_DIM)], axis=1)

    row, col = _tri_masks(tt)
    tril_bf = jnp.logical_and((row >> CHUNK_LOG2) == (col >> CHUNK_LOG2), row >= col).astype(BF16)

    def prepass(bi):
        cf = rw_ref[bi]

        r = cf[:, 0:W]
        k = cf[:, W:2 * W]
        v = cf[:, 2 * W:3 * W]
        wd_ad = cf[:, 3 * W:3 * W + LANES]
        gd = cf[:, 3 * W + LANES:3 * W + 2 * LANES]

        w_in = w0_ref[...] + _mm_bf(jnp.tanh(wd_ad), w2_ref[...])
        a = _sigmoid(a0_ref[...] + _mm_bf(wd_ad, a2_ref[...]))
        yield
        gate_s[bi] = _mm_bf(_sigmoid(gd), g2_ref[...])
        kkv = k * kk_ref[...]
        yield
        kk = kkv * lax.rsqrt(head_sums(kkv * kkv) + NORM_EPS)
        kmod = k * (1.0 + (a - 1.0) * ka_ref[...])
        yield
        bonus_s[bi] = head_sums(r * kmod * rk_ref[...]) * v
        lw = (-math.exp(-0.5) * LOG2_E) * _sigmoid(w_in)
        yield
        lc = _mm_const_lhs(tril_bf, lw)
        lct_s[bi] = lc.T
        l_last = jnp.concatenate(
            [jnp.broadcast_to(lc[c * C + C - 1:c * C + C, :], (C, W)) for c in range(n_c)], axis=0)
        yield
        e_inv = jnp.exp2(-lc)
        e_rem = jnp.exp2(l_last - lc)
        b = kk * a
        r_s[bi] = (r * jnp.exp2(lc)).astype(BF16)
        yield
        k_s[bi] = (kmod * e_inv).astype(BF16)
        b_s[bi] = (b * e_inv).astype(BF16)
        yield
        kk_s[bi] = kk * jnp.exp2(lc - lw)
        kend_s[bi] = (kmod * e_rem).astype(BF16)
        yield
        bend_s[bi] = (b * e_rem).astype(BF16)
        v_s[bi] = v.astype(BF16)

    lane = lax.broadcasted_iota(jnp.int32, (1, P2), 1)
    first = lane < RWKV_HEAD_DIM
    rows_of = lambda c: slice(c * C, (c + 1) * C)
    lanes_of = lambda p: slice(p * P2, (p + 1) * P2)
    cut = lambda ref, u: ref[u[0], rows_of(u[2]), lanes_of(u[1])]
    slot = lambda u: (u[0] * n_p + u[1]) * n_c + u[2]

    def per_head_rows(x):
        zero = jnp.zeros_like(x)
        return jnp.concatenate([jnp.where(first, x, zero), jnp.where(first, zero, x)], axis=0)

    def chunk_form(seqs):
        prow = lax.broadcasted_iota(jnp.int32, (C, 2 * P2), 0)
        pcol = lax.broadcasted_iota(jnp.int32, (C, 2 * P2), 1) & (C - 1)
        strict_p = prow > pcol
        causal_p = prow >= pcol
        units = [(bi, p, c) for bi in seqs for p in range(n_p) for c in range(n_c)]

        prod = [_mm(jnp.concatenate([cut(kk_s, u).astype(BF16), cut(r_s, u)], axis=0),
                    jnp.concatenate([per_head_rows(cut(b_s, u)), per_head_rows(cut(k_s, u))], axis=0), NT)
                for u in units]
        kbkk = [jnp.where(strict_p, x[0:C], 0.0) for x in prod]
        rbrk = [jnp.where(causal_p, x[C:2 * C], 0.0) for x in prod]
        for u, x in zip(units, rbrk):
            arb_s[slot(u)] = x[:, 0:P2]
        yield
        avrv = [_mm(jnp.concatenate([x[:, P2:2 * P2], y[:, P2:2 * P2]], axis=0).astype(BF16),
                    per_head_rows(cut(v_s, u))) for x, y, u in zip(kbkk, rbrk, units)]
        for u, x in zip(units, avrv):
            rv_s[u[0], rows_of(u[2]), lanes_of(u[1])] = x[C:2 * C]
        yield
        prow2, pcol2 = _tri_masks(2 * P2)
        t_p = yield from _unit_lower_inverses(
            [jnp.concatenate([kbkk[i][:, 0:P2], kbkk[i + 1][:, 0:P2]], axis=1) for i in range(0, len(units), 2)],
            ((prow2 >> CHUNK_LOG2) == (pcol2 >> CHUNK_LOG2)).astype(BF16))
        first2 = jnp.concatenate([first, first], axis=1)

        def per_head_rows2(x):
            zero = jnp.zeros_like(x)
            return jnp.concatenate([jnp.where(first2, x, zero), jnp.where(first2, zero, x)], axis=0)

        for i, u in enumerate(units):
            t_hi, t_lo = _split2(t_p[i // 2][:, (i % 2) * P2:(i % 2 + 1) * P2])
            x_hi, x_lo = _split2(jnp.concatenate([avrv[i][0:C], cut(kk_s, u)], axis=1))
            x_hi = per_head_rows2(x_hi)
            uw = _mm(t_hi, x_hi) + (_mm(t_hi, per_head_rows2(x_lo)) + _mm(t_lo, x_hi))
            u_s[u[0], rows_of(u[2]), lanes_of(u[1])] = uw[:, 0:P2]
            w_s[u[0], rows_of(u[2]), lanes_of(u[1])] = uw[:, P2:2 * P2]

    def recurrence(seqs):
        brow, bcol = _tri_masks(P2)
        blockdiag = (brow >> RWKV_HEAD_LOG2) == (bcol >> RWKV_HEAD_LOG2)
        pairs = [(bi, p) for bi in seqs for p in range(n_p)]
        for c in range(n_c):
            us = [(bi, p, c) for bi, p in pairs]
            ss = [state_ref[bi * n_p + p] for bi, p in pairs]
            rw = [_mm(jnp.concatenate([cut(r_s, u), cut(w_s, u).astype(BF16)], axis=0), s.astype(BF16))
                  for u, s in zip(us, ss)]
            pm = [-(x[C:2 * C] + cut(u_s, u)) for u, x in zip(us, rw)]
            upd = [_mm(jnp.concatenate([cut(bend_s, u), cut(kend_s, u)], axis=0),
                       jnp.concatenate([x.astype(BF16), cut(v_s, u)], axis=0), TN) for u, x in zip(us, pm)]
            for i, (bi, p) in enumerate(pairs):
                g_col = jnp.exp2(lct_s[bi, p * P2:(p + 1) * P2, c * C + C - 1:c * C + C])
                state_ref[bi * n_p + p] = ss[i] * g_col + jnp.where(blockdiag, upd[i], 0.0)
                y_s[bi, rows_of(c), lanes_of(p)] = rw[i][0:C] + cut(rv_s, us[i]) + _mm_bf(
                    arb_s[slot(us[i])], per_head_rows(pm[i]))

    _run(prepass(0))
    for bi in range(nb):
        _interleave(chunk_form([bi]), *([prepass(bi + 1)] if bi + 1 < nb else []))
    recurrence(list(range(nb)))

    inv_n = 1.0 / RWKV_HEAD_DIM
    for bi in range(nb):
        y = y_s[bi]
        mean = head_sums(y) * inv_n
        d = y - mean
        var = head_sums(d * d) * inv_n
        yn = d * lax.rsqrt(var + RWKV_GN_EPS) * lnw_ref[...] + lnb_ref[...]
        o_ref[bi] = ((yn + bonus_s[bi]) * gate_s[bi]).astype(o_ref.dtype)


def _rwkv(rw, w0, w2p, a0, a2p, g2, k_k, k_a, r_k, ln_w, ln_b, bsz, seq, tt=256, nb=4):
    cols = rw.shape[1]
    W = RWKV_WIDTH
    tile = lambda b, j: (b, j, 0)
    const = lambda b, j: (0, 0)
    vec = pl.BlockSpec((1, W), const)
    out = pl.pallas_call(
        functools.partial(_rwkv_kernel, tt=tt, nb=nb),
        out_shape=jax.ShapeDtypeStruct((bsz, seq, W), BF16),
        grid=(bsz // nb, seq // tt),
        in_specs=[
            pl.BlockSpec((nb, tt, cols), tile),
            vec,
            pl.BlockSpec(w2p.shape, const),
            vec,
            pl.BlockSpec(a2p.shape, const),
            pl.BlockSpec(g2.shape, const),
            vec, vec, vec, vec, vec,
        ],
        out_specs=pl.BlockSpec((nb, tt, W), tile),
        scratch_shapes=[pltpu.VMEM((nb, tt, W), dt) for dt in (BF16, BF16, BF16, F32, BF16, BF16, BF16) + (F32,) * 6]
        + [pltpu.VMEM((nb, W, tt), F32),
           pltpu.VMEM((nb * (RWKV_HEADS // 2) * (tt // CHUNK), CHUNK, 2 * RWKV_HEAD_DIM), F32),
           pltpu.VMEM((nb * RWKV_HEADS // 2, 2 * RWKV_HEAD_DIM, 2 * RWKV_HEAD_DIM), F32)],
        compiler_params=pltpu.CompilerParams(
            dimension_semantics=("arbitrary", "arbitrary"), vmem_limit_bytes=VMEM_LIMIT),
        name="rwkv7",
    )(rw.reshape(bsz, seq, cols), w0, w2p, a0, a2p, g2, k_k, k_a, r_k, ln_w, ln_b)
    return out.reshape(bsz * seq, W)


def _rglru_kernel(gb_ref, xb_ref, wa_ref, ba_ref, wx_ref, bx_ref, lam_ref, o_ref, h_ref, *, tt):
    @pl.when(pl.program_id(1) == 0)
    def _():
        h_ref[...] = jnp.zeros_like(h_ref)

    xc = xb_ref[...]

    blk = xc.shape[1] // LRU_BLOCKS
    xc_bf = xc.astype(BF16)
    ra = jnp.concatenate(
        [_mm(xc_bf[:, n * blk:(n + 1) * blk], wa_ref[n]) for n in range(LRU_BLOCKS)], axis=1)
    ix = jnp.concatenate(
        [_mm(xc_bf[:, n * blk:(n + 1) * blk], wx_ref[n]) for n in range(LRU_BLOCKS)], axis=1)
    r = _sigmoid(ra + ba_ref[...])
    i = _sigmoid(ix + bx_ref[...])
    a = jnp.exp2(r * ((-LRU_C * LOG2_E) * _softplus(-lam_ref[...])))
    u = xc * i * jnp.sqrt(1.0 - a * a)

    width = xc.shape[1]
    a = a.reshape(tt // SUB, SUB, width)
    u = u.reshape(tt // SUB, SUB, width)
    sub = lax.broadcasted_iota(jnp.int32, (1, SUB, 1), 1)
    d = 1
    while d < SUB:
        keep = sub >= d
        a_sh = jnp.where(keep, pltpu.roll(a, d, axis=1), 1.0)
        u_sh = jnp.where(keep, pltpu.roll(u, d, axis=1), 0.0)
        u = a * u_sh + u
        a = a * a_sh
        d *= 2
    gate = _gelu_tanh(gb_ref[...])
    carry = h_ref[...]
    pack = 2
    for g0 in range(0, tt // SUB, pack):
        outs = []
        for g in range(g0, g0 + pack):
            h = u[g] + a[g] * carry
            outs.append(h * gate[g * SUB:(g + 1) * SUB])
            carry = jnp.broadcast_to(h[SUB - 1:SUB, :], h.shape)
        o_ref[g0 * SUB:(g0 + pack) * SUB, :] = jnp.concatenate(outs, axis=0).astype(o_ref.dtype)
    h_ref[...] = carry


def _rglru(gb, xb, wa_bf, ba, wx_bf, bx, lam, bsz, seq, tt=256):
    n_tok, width = xb.shape
    per_b = seq // tt
    row = lambda b, j: (b * per_b + j, 0)
    const = lambda b, j: (0, 0)
    const3 = lambda b, j: (0, 0, 0)
    vec = pl.BlockSpec((1, width), const)
    return pl.pallas_call(
        functools.partial(_rglru_kernel, tt=tt),
        out_shape=jax.ShapeDtypeStruct((n_tok, width), BF16),
        grid=(bsz, per_b),
        in_specs=[
            pl.BlockSpec((tt, width), row),
            pl.BlockSpec((tt, width), row),
            pl.BlockSpec(wa_bf.shape, const3),
            vec,
            pl.BlockSpec(wx_bf.shape, const3),
            vec,
            vec,
        ],
        out_specs=pl.BlockSpec((tt, width), row),
        scratch_shapes=[pltpu.VMEM((HALO, width), F32)],
        compiler_params=pltpu.CompilerParams(
            dimension_semantics=("arbitrary", "arbitrary"), vmem_limit_bytes=VMEM_LIMIT),
        name="rglru",
    )(gb, xb, wa_bf, ba, wx_bf, bx, lam)


def _pad_lanes(v):
    return jnp.pad(v, (0, LANES - v.shape[0])).reshape(1, LANES)


def kernel(x, c, norm_pre, norm_post, ada_w, ada_b, ffn_w_gate, ffn_w_up, ffn_w_down, mix_w_in, mix_w_out, gdn_conv_w, gdn_a_log, gdn_dt_bias, gdn_norm_w, rwkv_mu, rwkv_w0, rwkv_w2, rwkv_a0, rwkv_a2, rwkv_g2, rwkv_k_k, rwkv_k_a, rwkv_r_k, rwkv_ln_w, rwkv_ln_b, lru_w_in, lru_conv_w, lru_conv_b, lru_wa, lru_ba, lru_wx, lru_bx, lru_lambda, lru_w_out):
    bsz, seq, d = x.shape
    depth = norm_pre.shape[0]
    x2 = x.reshape(bsz * seq, d)
    mods = _ada_params(c, ada_w, ada_b)

    def mod(layer, sub):
        m = mods[layer * 2 + sub]
        return (m[:, None, 0:d], m[:, None, d:2 * d], m[:, None, 2 * d:3 * d])

    GW = GDN_WIDTH
    for layer in range(depth):
        j = layer // 2
        shift, scale, gate = mod(layer, 0)
        shift2, scale2, gate2 = mod(layer, 1)
        mod_rows = jnp.concatenate([gate, shift2, scale2, gate2], axis=1)
        nw_rows = jnp.stack([norm_post[layer, 0], norm_pre[layer, 1], norm_post[layer, 1]])
        nw_pre = norm_pre[layer, 0].reshape(1, d)
        ffn_w = (ffn_w_gate[layer].astype(BF16), ffn_w_up[layer].astype(BF16), ffn_w_down[layer].astype(BF16))
        if layer % 2 == 0:
            w_in = mix_w_in[j].astype(BF16)
            n_gdn = 4 * GW + 2 * GDN_HEADS
            w_cat = jnp.concatenate(
                [w_in[:, 0:4 * GW], w_in[:, n_gdn:],
                 jnp.pad(w_in[:, 4 * GW:n_gdn], ((0, 0), (0, LANES - 2 * GDN_HEADS)))], axis=1)
            qkvz, rw, ab = _norm_proj(
                x2, nw_pre, shift, scale, w_cat, (4 * GW, RWKV_COLS, LANES),
                (("conv_silu", 3 * GW), ("shift_mix",), ("plain",)),
                (gdn_conv_w[j], rwkv_mu[j].reshape(1, RWKV_COLS)), seq)
            out_a = _gdn(qkvz, ab, _pad_lanes(gdn_a_log[j]), _pad_lanes(gdn_dt_bias[j]),
                         gdn_norm_w[j].reshape(1, GDN_HEAD_DIM), bsz, seq)
            w2p = jnp.pad(rwkv_w2[j], ((0, LANES - DECAY_LORA), (0, 0))).astype(BF16)
            a2p = jnp.pad(rwkv_a2[j], ((DECAY_LORA, LANES - DECAY_LORA - AAA_LORA), (0, 0))).astype(BF16)
            vec = lambda t: t.reshape(1, RWKV_WIDTH)
            out_b = _rwkv(rw, vec(rwkv_w0[j]), w2p, vec(rwkv_a0[j]), a2p,
                          rwkv_g2[j].astype(BF16), vec(rwkv_k_k[j]), vec(rwkv_k_a[j]), vec(rwkv_r_k[j]),
                          vec(rwkv_ln_w[j]), vec(rwkv_ln_b[j]), bsz, seq)
            w_out = mix_w_out[j].astype(BF16)
            x2 = _out_ffn(x2, [out_a, out_b], [w_out[0:GW], w_out[GW:]], mod_rows, nw_rows, *ffn_w, seq)
        else:
            width = lru_w_in.shape[2] // 2
            vec = lambda t: t.reshape(1, width)
            gb, xb = _norm_proj(x2, nw_pre, shift, scale, lru_w_in[j].astype(BF16), (width, width),
                                (("plain",), ("conv_bias",)), (lru_conv_w[j], vec(lru_conv_b[j])), seq)
            y = _rglru(gb, xb, lru_wa[j].astype(BF16), vec(lru_ba[j]),
                       lru_wx[j].astype(BF16), vec(lru_bx[j]), vec(lru_lambda[j]), bsz, seq)
            x2 = _out_ffn(x2, [y], [lru_w_out[j].astype(BF16)], mod_rows, nw_rows, *ffn_w, seq)
    return x2.reshape(bsz, seq, d)
```

```python
import functools
import math

import jax
import jax.numpy as jnp
from jax import lax
from jax.experimental import pallas as pl
from jax.experimental.pallas import tpu as pltpu

F32 = jnp.float32
BF16 = jnp.bfloat16

NORM_EPS = 1e-6
LOG2_E = 1.4426950408889634
GDN_HEADS = 4
GDN_HEAD_DIM = 128
GDN_WIDTH = GDN_HEADS * GDN_HEAD_DIM
CHUNK_LOG2 = 6
CHUNK = 1 << CHUNK_LOG2
BASE_LOG2 = 3
CONV_WIDTH = 4
RWKV_HEADS = 8
RWKV_HEAD_LOG2 = 6
RWKV_HEAD_DIM = 1 << RWKV_HEAD_LOG2
RWKV_WIDTH = RWKV_HEADS * RWKV_HEAD_DIM
DECAY_LORA = 64
AAA_LORA = 64
GATE_LORA = 128
RWKV_COLS = 3 * RWKV_WIDTH + DECAY_LORA + AAA_LORA + GATE_LORA
RWKV_GN_EPS = 64e-5
LRU_BLOCKS = 4
LRU_C = 8.0
LANES = 128
MXU_DIM = 256
FFN_PART_ROWS = 256
SUB = 8
HALO = SUB
VMEM_LIMIT = 56 * 1024 * 1024

NN = (((1,), (0,)), ((), ()))
NT = (((1,), (1,)), ((), ()))
TN = (((0,), (0,)), ((), ()))


def _mm(a, b, dims=NN):
    return lax.dot_general(a, b, dims, preferred_element_type=F32)


def _mm_bf(a, b, dims=NN):
    return _mm(a.astype(BF16), b.astype(BF16), dims)


def _split2(x):
    hi = x.astype(BF16)
    lo = (x - hi.astype(F32)).astype(BF16)
    return hi, lo


def _mm_x3(a, b, dims=NN):
    ah, al = _split2(a)
    bh, bl = _split2(b)
    return _mm(ah, bh, dims) + (_mm(ah, bl, dims) + _mm(al, bh, dims))


def _mm_const_lhs(c_bf, x):
    hi = x.astype(BF16)
    r1 = x - hi.astype(F32)
    mid = r1.astype(BF16)
    lo = (r1 - mid.astype(F32)).astype(BF16)
    return _mm(c_bf, hi) + (_mm(c_bf, mid) + _mm(c_bf, lo))


def _mm_const_rhs(x, c_bf):
    hi, lo = _split2(x)
    return _mm(hi, c_bf) + _mm(lo, c_bf)


def _sigmoid(x):
    return 0.5 * jnp.tanh(0.5 * x) + 0.5


def _silu(x):
    h = 0.5 * x
    return h * jnp.tanh(h) + h


def _softplus(x):
    return jnp.maximum(x, 0.0) + jnp.log1p(jnp.exp(-jnp.abs(x)))


def _gelu_tanh(x):
    c = 0.7978845608028654
    h = 0.5 * x
    return h * jnp.tanh(x * (c + (c * 0.044715) * (x * x))) + h


def _rms(x):
    return x * lax.rsqrt(jnp.mean(x * x, axis=-1, keepdims=True) + NORM_EPS)


def _tri_masks(n):
    row = lax.broadcasted_iota(jnp.int32, (n, n), 0)
    col = lax.broadcasted_iota(jnp.int32, (n, n), 1)
    return row, col


def _pack_blocks(x):
    out = x[0:CHUNK]
    for c in range(1, x.shape[0] // CHUNK):
        out = out + x[c * CHUNK:(c + 1) * CHUNK]
    return out


def _expand_blocks(xp_bf, mask_bf):
    return jnp.concatenate([xp_bf] * (xp_bf.shape[1] // CHUNK), axis=0) * mask_bf


def _unit_lower_inverses(ms_p, mask_bf):
    n = ms_p[0].shape[1]
    prow = lax.broadcasted_iota(jnp.int32, (CHUNK, n), 0)
    pcol = lax.broadcasted_iota(jnp.int32, (CHUNK, n), 1) & (CHUNK - 1)
    eye_p = (prow == pcol).astype(F32)
    expand = lambda xp: _expand_blocks(xp.astype(BF16), mask_bf)
    mm = lambda xp, y_bf: _mm(xp.astype(BF16), y_bf)
    same = (prow >> BASE_LOG2) == (pcol >> BASE_LOG2)
    m8_p = [jnp.where(same, mp, 0.0) for mp in ms_p]
    m2_p = [mm(ap, expand(ap)) for ap in m8_p]
    yield
    m4_p = [mm(ap, expand(ap)) for ap in m2_p]
    yield
    inv_p = [mm(eye_p - ap, expand(eye_p + bp)) for ap, bp in zip(m8_p, m2_p)]
    yield
    inv_p = [mm(ap, expand(eye_p + bp)) for ap, bp in zip(inv_p, m4_p)]
    yield
    shift = BASE_LOG2
    while shift < CHUNK_LOG2:
        pair = (prow >> (shift + 1)) == (pcol >> (shift + 1))
        off_mask = jnp.logical_and(pair, jnp.logical_not(same))
        tmp_p = [mm(ap, expand(jnp.where(off_mask, mp, 0.0))) for ap, mp in zip(inv_p, ms_p)]
        yield
        inv_p = [ap - mm(tp, expand(ap)) for ap, tp in zip(inv_p, tmp_p)]
        yield
        same = pair
        shift += 1
    a_split = [_split2(ap) for ap in inv_p]
    m_split = [_split2(mp) for mp in ms_p]
    e_hi = [_expand_blocks(a_hi, mask_bf) for a_hi, _ in a_split]
    e_lo = [_expand_blocks(a_lo, mask_bf) for _, a_lo in a_split]
    prod = [_mm(m_hi, eh) + (_mm(m_hi, el) + _mm(m_lo, eh)) for (m_hi, m_lo), eh, el in zip(m_split, e_hi, e_lo)]
    yield
    res = [expand(eye_p - (ap + pr)) for ap, pr in zip(inv_p, prod)]
    return [ap + _mm(a_hi, r) for ap, (a_hi, _), r in zip(inv_p, a_split, res)]


def _run(gen):
    try:
        while True:
            next(gen)
    except StopIteration as stop:
        return stop.value


def _interleave(*gens):
    live = list(gens)
    while live:
        for g in list(live):
            try:
                next(g)
            except StopIteration:
                live.remove(g)


def _ada_kernel(c_ref, w_ref, b_ref, o_ref):
    s = _silu(c_ref[...])
    o_ref[0] = _mm_x3(s, w_ref[0]) + b_ref[0]


def _ada_params(c, ada_w, ada_b):
    n_l, n_s, d, d3 = ada_w.shape
    n = n_l * n_s
    bsz = c.shape[0]
    tn = 1024
    return pl.pallas_call(
        _ada_kernel,
        out_shape=jax.ShapeDtypeStruct((n, bsz, d3), F32),
        grid=(n, d3 // tn),
        in_specs=[
            pl.BlockSpec((bsz, d), lambda i, j: (0, 0)),
            pl.BlockSpec((1, d, tn), lambda i, j: (i, 0, j)),
            pl.BlockSpec((1, 1, tn), lambda i, j: (i, 0, j)),
        ],
        out_specs=pl.BlockSpec((1, bsz, tn), lambda i, j: (i, 0, j)),
        compiler_params=pltpu.CompilerParams(vmem_limit_bytes=VMEM_LIMIT),
        name="ada_params",
    )(c, ada_w.reshape(n, d, d3), ada_b.reshape(n, 1, d3))


def _causal_conv(x, hist, cw):
    n = x.shape[0]
    xp = jnp.concatenate([hist, x], axis=0)
    y = x * cw[CONV_WIDTH - 1:CONV_WIDTH]
    for j in range(CONV_WIDTH - 1):
        o = HALO - (CONV_WIDTH - 1) + j
        y = y + xp[o:o + n] * cw[j:j + 1]
    return y


def _norm_proj_kernel(x_ref, nw_ref, shift_ref, scale_ref, w_ref, *refs, post, per_b):
    n_par = sum({"plain": 0, "conv_silu": 1, "conv_bias": 2, "shift_mix": 1}[p[0]] for p in post)
    par_refs = list(refs[:n_par])
    out_refs = refs[n_par:n_par + len(post)]
    hist_refs = list(refs[n_par + len(post):])

    @pl.when(pl.program_id(0) % per_b == 0)
    def _():
        for h_ref in hist_refs:
            h_ref[...] = jnp.zeros_like(h_ref)

    tm = x_ref.shape[0]
    halves = [slice(0, tm // 2), slice(tm // 2, tm)]
    hs = [((_rms(x_ref[rows, :]) * nw_ref[...]) * (1.0 + scale_ref[0]) + shift_ref[0]).astype(BF16)
          for rows in halves]
    off = 0
    for o_ref, p in zip(out_refs, post):
        n = o_ref.shape[1]
        ys = [_mm(h, w_ref[:, off:off + n]) for h in hs]
        off += n
        if p[0] == "plain":
            outs = ys
        else:
            h_ref = hist_refs.pop(0)
            hist = h_ref[...]
            h_ref[...] = ys[-1][tm // 2 - HALO:tm // 2]
            hists = [hist, ys[0][tm // 2 - HALO:tm // 2]]
            if p[0] == "conv_silu":
                cw = par_refs.pop(0)[...]
                outs = [jnp.concatenate([_silu(_causal_conv(y[:, 0:p[1]], hi[:, 0:p[1]], cw)), y[:, p[1]:n]], axis=1)
                        for y, hi in zip(ys, hists)]
            elif p[0] == "conv_bias":
                cw = par_refs.pop(0)[...]
                cb = par_refs.pop(0)[...]
                outs = [_causal_conv(y, hi, cw) + cb for y, hi in zip(ys, hists)]
            else:
                mu = par_refs.pop(0)[...]
                trow = lax.broadcasted_iota(jnp.int32, (tm // 2, 1), 0)
                outs = [y + mu * (jnp.where(trow == 0, hi[HALO - 1:HALO, :], pltpu.roll(y, 1, axis=0)) - y)
                        for y, hi in zip(ys, hists)]
        for rows, out in zip(halves, outs):
            o_ref[rows, :] = out


def _norm_proj(x2, nw, shift, scale, w_bf, splits, post, params, seq, tm=512):
    n_tok, d = x2.shape
    per_b = seq // tm
    const = lambda i: (0, 0)
    return pl.pallas_call(
        functools.partial(_norm_proj_kernel, post=post, per_b=per_b),
        out_shape=[jax.ShapeDtypeStruct((n_tok, n), F32) for n in splits],
        grid=(n_tok // tm,),
        in_specs=[
            pl.BlockSpec((tm, d), lambda i: (i, 0)),
            pl.BlockSpec((1, d), const),
            pl.BlockSpec((1, 1, d), lambda i: (i // per_b, 0, 0)),
            pl.BlockSpec((1, 1, d), lambda i: (i // per_b, 0, 0)),
            pl.BlockSpec(w_bf.shape, const, pipeline_mode=pl.Buffered(1)),
        ] + [pl.BlockSpec(p.shape, const) for p in params],
        out_specs=[pl.BlockSpec((tm, n), lambda i: (i, 0)) for n in splits],
        scratch_shapes=[pltpu.VMEM((HALO, n), F32) for n, p in zip(splits, post) if p[0] != "plain"],
        compiler_params=pltpu.CompilerParams(
            dimension_semantics=("arbitrary",), vmem_limit_bytes=VMEM_LIMIT),
        name="norm_proj",
    )(x2, nw, shift, scale, w_bf, *params)


def _out_ffn_kernel(*refs, n_in):
    x_ref = refs[0]
    a_refs = refs[1:1 + n_in]
    w_refs = refs[1 + n_in:1 + 2 * n_in]
    mod_ref, nw_ref, wg_ref, wu_ref, wd_ref, o_ref = refs[1 + 2 * n_in:]
    mod = mod_ref[0]
    nw = nw_ref[...]
    tm = x_ref.shape[0]
    halves = [slice(r, r + FFN_PART_ROWS) for r in range(0, tm, FFN_PART_ROWS)]
    ys = []
    for rows in halves:
        y = _mm(a_refs[0][rows, :].astype(BF16), w_refs[0][...])
        for a_ref, w_ref in zip(a_refs[1:], w_refs[1:]):
            y = y + _mm(a_ref[rows, :].astype(BF16), w_ref[...])
        ys.append(y)
    xs = [x_ref[rows, :] + mod[0:1] * (_rms(y) * nw[0:1]) for rows, y in zip(halves, ys)]
    hs = [((_rms(x) * nw[1:2]) * (1.0 + mod[2:3]) + mod[1:2]).astype(BF16) for x in xs]
    gs = [_mm(h, wg_ref[...]) for h in hs]
    us = [_mm(h, wu_ref[...]) for h in hs]
    acts = [(_silu(g) * u).astype(BF16) for g, u in zip(gs, us)]
    ys = [_mm(act, wd_ref[...]) for act in acts]
    for rows, x, y in zip(halves, xs, ys):
        o_ref[rows, :] = x + mod[3:4] * (_rms(y) * nw[2:3])


def _out_ffn(x2, acts, ws_bf, mod, nw, wg_bf, wu_bf, wd_bf, seq, tm=1024):
    n_tok, d = x2.shape
    per_b = seq // tm
    row = lambda i: (i, 0)
    const = lambda i: (0, 0)
    resident = lambda w: pl.BlockSpec(w.shape, const, pipeline_mode=pl.Buffered(1))
    in_specs = [pl.BlockSpec((tm, d), row)]
    in_specs += [pl.BlockSpec((tm, a.shape[1]), row) for a in acts]
    in_specs += [resident(w) for w in ws_bf]
    in_specs += [pl.BlockSpec((1,) + mod.shape[1:], lambda i: (i // per_b, 0, 0)),
                 pl.BlockSpec(nw.shape, const), resident(wg_bf), resident(wu_bf), resident(wd_bf)]
    return pl.pallas_call(
        functools.partial(_out_ffn_kernel, n_in=len(acts)),
        out_shape=jax.ShapeDtypeStruct((n_tok, d), F32),
        grid=(n_tok // tm,),
        in_specs=in_specs,
        out_specs=pl.BlockSpec((tm, d), row),
        compiler_params=pltpu.CompilerParams(vmem_limit_bytes=VMEM_LIMIT),
        name="out_ffn",
    )(x2, *acts, *ws_bf, mod, nw, wg_bf, wu_bf, wd_bf)


def _gdn_kernel(qkvz_ref, ab_ref, alog_ref, dtb_ref, nw_ref, o_ref,
                qkv_s, gc_s, u_s, w_s, qd_s, kd_s, attn_s, o_s, state_ref, *, tt, nb):
    W = GDN_WIDTH
    D = GDN_HEAD_DIM
    C = CHUNK
    H = GDN_HEADS
    n_c = tt // C

    @pl.when(pl.program_id(1) == 0)
    def _():
        state_ref[...] = jnp.zeros_like(state_ref)

    row, col = _tri_masks(tt)
    same_chunk = (row >> CHUNK_LOG2) == (col >> CHUNK_LOG2)
    causal = jnp.logical_and(same_chunk, row >= col)
    strict = jnp.logical_and(same_chunk, row > col)
    tril_bf = causal.astype(BF16)
    mask_bf = same_chunk.astype(BF16)
    ones_bf = jnp.ones((D, D), BF16)

    units = [(bi, h) for bi in range(nb) for h in range(H)]
    qs, ks, vs, betas, k_betas, decays, e_gs, e_rems = [], [], [], [], [], [], [], []
    for bi in range(nb):
        y = qkvz_ref[bi, :, 0:3 * W]
        for h in range(H):
            q = y[:, h * D:(h + 1) * D]
            k = y[:, W + h * D:W + (h + 1) * D]
            q = q * lax.rsqrt(_mm_const_rhs(q * q, ones_bf) + NORM_EPS) * (D ** -0.5)
            k = k * lax.rsqrt(_mm_const_rhs(k * k, ones_bf) + NORM_EPS)
            qkv_s[bi, :, h * D:(h + 1) * D] = q
            qkv_s[bi, :, W + h * D:W + (h + 1) * D] = k
        qkv_s[bi, :, 2 * W:3 * W] = y[:, 2 * W:3 * W]

        ab = ab_ref[bi]
        g = (-LOG2_E * jnp.exp(alog_ref[...])) * _softplus(ab + dtb_ref[...])
        beta_all = _sigmoid(ab)
        gc = _mm_const_lhs(tril_bf, g)
        gc_s[bi] = gc
        gc_t = gc.T
        g_last = jnp.concatenate(
            [jnp.broadcast_to(gc[c * C + C - 1:c * C + C, :], (C, LANES)) for c in range(n_c)], axis=0)
        e_g = jnp.exp2(gc)
        e_rem = jnp.exp2(g_last - gc)
        for h in range(H):
            qs.append(qkv_s[bi, :, h * D:(h + 1) * D])
            ks.append(qkv_s[bi, :, W + h * D:W + (h + 1) * D])
            vs.append(qkv_s[bi, :, 2 * W + h * D:2 * W + (h + 1) * D])
            betas.append(beta_all[:, H + h:H + h + 1])
            k_betas.append(ks[-1] * betas[-1])
            decays.append(
                jnp.where(causal, jnp.exp2(jnp.minimum(gc[:, h:h + 1] - gc_t[h:h + 1, :], 0.0)), 0.0))
            e_gs.append(e_g[:, h:h + 1])
            e_rems.append(e_rem[:, h:h + 1])

    ms = [jnp.where(strict, _mm_bf(kb, k, NT) * dc, 0.0) for kb, k, dc in zip(k_betas, ks, decays)]
    t_ps = _run(_unit_lower_inverses([_pack_blocks(m) for m in ms], mask_bf))
    for i in range(len(units)):
        rhs = jnp.concatenate([vs[i] * betas[i], k_betas[i] * e_gs[i]], axis=1)
        t_hi, t_lo = _split2(t_ps[i])
        rhs_hi, rhs_lo = _split2(rhs)
        t_hi = _expand_blocks(t_hi, mask_bf)
        uw = _mm(t_hi, rhs_hi) + (_mm(t_hi, rhs_lo) + _mm(_expand_blocks(t_lo, mask_bf), rhs_hi))
        u_s[i] = uw[:, 0:D]
        w_s[i] = uw[:, D:2 * D]
        attn_s[i] = _mm_bf(qs[i], ks[i], NT) * decays[i]
        qd_s[i] = qs[i] * e_gs[i]
        kd_s[i] = ks[i] * e_rems[i]

    for c in range(n_c):
        rows = slice(c * C, (c + 1) * C)
        win = slice((c // 2) * 2 * C, (c // 2 + 1) * 2 * C)
        ss = [state_ref[i] for i in range(len(units))]
        wq = [_mm_bf(jnp.concatenate([w_s[i, rows, :], qd_s[i, rows, :]], axis=0), s) for i, s in enumerate(ss)]
        v_new = [u_s[i, rows, :] - x[0:C] for i, x in enumerate(wq)]
        kv = [_mm_bf(kd_s[i, rows, :], x, TN) for i, x in enumerate(v_new)]
        for i, (bi, h) in enumerate(units):
            gl = jnp.exp2(gc_s[bi, c * C + C - 1:c * C + C, h:h + 1])
            state_ref[i] = ss[i] * gl + kv[i]
            vv = jnp.concatenate([v_new[i], v_new[i]], axis=0)
            o_s[bi, rows, h * D:(h + 1) * D] = wq[i][C:2 * C] + _mm_bf(attn_s[i, rows, win], vv)

    nw = nw_ref[...]
    for bi, h in units:
        z = qkvz_ref[bi, :, 3 * W + h * D:3 * W + (h + 1) * D]
        o_ref[bi, :, h * D:(h + 1) * D] = (_rms(o_s[bi, :, h * D:(h + 1) * D]) * nw * _silu(z)).astype(o_ref.dtype)


def _gdn(qkvz, ab, a_log_row, dt_row, norm_w, bsz, seq, tt=256, nb=4):
    W = GDN_WIDTH
    tile = lambda b, j: (b, j, 0)
    const = lambda b, j: (0, 0)
    n_u = nb * GDN_HEADS
    out = pl.pallas_call(
        functools.partial(_gdn_kernel, tt=tt, nb=nb),
        out_shape=jax.ShapeDtypeStruct((bsz, seq, W), BF16),
        grid=(bsz // nb, seq // tt),
        in_specs=[
            pl.BlockSpec((nb, tt, 4 * W), tile),
            pl.BlockSpec((nb, tt, LANES), tile),
            pl.BlockSpec((1, LANES), const),
            pl.BlockSpec((1, LANES), const),
            pl.BlockSpec((1, GDN_HEAD_DIM), const),
        ],
        out_specs=pl.BlockSpec((nb, tt, W), tile),
        scratch_shapes=[
            pltpu.VMEM((nb, tt, 3 * W), F32),
            pltpu.VMEM((nb, tt, LANES), F32),
            pltpu.VMEM((n_u, tt, GDN_HEAD_DIM), F32),
            pltpu.VMEM((n_u, tt, GDN_HEAD_DIM), F32),
            pltpu.VMEM((n_u, tt, GDN_HEAD_DIM), F32),
            pltpu.VMEM((n_u, tt, GDN_HEAD_DIM), F32),
            pltpu.VMEM((n_u, tt, tt), F32),
            pltpu.VMEM((nb, tt, W), F32),
            pltpu.VMEM((n_u, GDN_HEAD_DIM, GDN_HEAD_DIM), F32),
        ],
        compiler_params=pltpu.CompilerParams(
            dimension_semantics=("arbitrary", "arbitrary"), vmem_limit_bytes=VMEM_LIMIT),
        name="gdn",
    )(qkvz.reshape(bsz, seq, 4 * W), ab.reshape(bsz, seq, LANES), a_log_row, dt_row, norm_w)
    return out.reshape(bsz * seq, W)


def _rwkv_kernel(rw_ref, w0_ref, w2_ref, a0_ref, a2_ref, g2_ref, kk_ref, ka_ref, rk_ref,
                 lnw_ref, lnb_ref, o_ref,
                 r_s, k_s, b_s, kk_s, kend_s, bend_s, v_s, y_s, u_s, w_s, rv_s, bonus_s, gate_s,
                 lct_s, arb_s, state_ref, *, tt, nb):
    W = RWKV_WIDTH
    C = CHUNK
    NH = RWKV_HEADS
    P2 = 2 * RWKV_HEAD_DIM
    n_c = tt // C
    n_p = NH // 2

    @pl.when(pl.program_id(1) == 0)
    def _():
        state_ref[...] = jnp.zeros_like(state_ref)

    hrow, hcol = _tri_masks(MXU_DIM)
    head_ones = ((hrow >> RWKV_HEAD_LOG2) == (hcol >> RWKV_HEAD_LOG2)).astype(BF16)

    def head_sums(x):
        return jnp.concatenate(
            [_mm_const_rhs(x[:, o:o + MXU_DIM], head_ones) for o in range(0, W, MXU_DIM)], axis=1)

    row, col = _tri_masks(tt)
    tril_bf = jnp.logical_and((row >> CHUNK_LOG2) == (col >> CHUNK_LOG2), row >= col).astype(BF16)

    def prepass(bi):
        cf = rw_ref[bi]

        r = cf[:, 0:W]
        k = cf[:, W:2 * W]
        v = cf[:, 2 * W:3 * W]
        wd_ad = cf[:, 3 * W:3 * W + LANES]
        gd = cf[:, 3 * W + LANES:3 * W + 2 * LANES]

        w_in = w0_ref[...] + _mm_bf(jnp.tanh(wd_ad), w2_ref[...])
        a = _sigmoid(a0_ref[...] + _mm_bf(wd_ad, a2_ref[...]))
        yield
        gate_s[bi] = _mm_bf(_sigmoid(gd), g2_ref[...])
        kkv = k * kk_ref[...]
        yield
        kk = kkv * lax.rsqrt(head_sums(kkv * kkv) + NORM_EPS)
        kmod = k * (1.0 + (a - 1.0) * ka_ref[...])
        yield
        bonus_s[bi] = head_sums(r * kmod * rk_ref[...]) * v
        lw = (-math.exp(-0.5) * LOG2_E) * _sigmoid(w_in)
        yield
        lc = _mm_const_lhs(tril_bf, lw)
        lct_s[bi] = lc.T
        l_last = jnp.concatenate(
            [jnp.broadcast_to(lc[c * C + C - 1:c * C + C, :], (C, W)) for c in range(n_c)], axis=0)
        yield
        e_inv = jnp.exp2(-lc)
        e_rem = jnp.exp2(l_last - lc)
        b = kk * a
        r_s[bi] = (r * jnp.exp2(lc)).astype(BF16)
        yield
        k_s[bi] = (kmod * e_inv).astype(BF16)
        b_s[bi] = (b * e_inv).astype(BF16)
        yield
        kk_s[bi] = kk * jnp.exp2(lc - lw)
        kend_s[bi] = (kmod * e_rem).astype(BF16)
        yield
        bend_s[bi] = (b * e_rem).astype(BF16)
        v_s[bi] = v.astype(BF16)

    lane = lax.broadcasted_iota(jnp.int32, (1, P2), 1)
    first = lane < RWKV_HEAD_DIM
    rows_of = lambda c: slice(c * C, (c + 1) * C)
    lanes_of = lambda p: slice(p * P2, (p + 1) * P2)
    cut = lambda ref, u: ref[u[0], rows_of(u[2]), lanes_of(u[1])]
    slot = lambda u: (u[0] * n_p + u[1]) * n_c + u[2]

    def per_head_rows(x):
        zero = jnp.zeros_like(x)
        return jnp.concatenate([jnp.where(first, x, zero), jnp.where(first, zero, x)], axis=0)

    def chunk_form(seqs):
        prow = lax.broadcasted_iota(jnp.int32, (C, 2 * P2), 0)
        pcol = lax.broadcasted_iota(jnp.int32, (C, 2 * P2), 1) & (C - 1)
        strict_p = prow > pcol
        causal_p = prow >= pcol
        units = [(bi, p, c) for bi in seqs for p in range(n_p) for c in range(n_c)]

        prod = [_mm(jnp.concatenate([cut(kk_s, u).astype(BF16), cut(r_s, u)], axis=0),
                    jnp.concatenate([per_head_rows(cut(b_s, u)), per_head_rows(cut(k_s, u))], axis=0), NT)
                for u in units]
        kbkk = [jnp.where(strict_p, x[0:C], 0.0) for x in prod]
        rbrk = [jnp.where(causal_p, x[C:2 * C], 0.0) for x in prod]
        for u, x in zip(units, rbrk):
            arb_s[slot(u)] = x[:, 0:P2]
        yield
        avrv = [_mm(jnp.concatenate([x[:, P2:2 * P2], y[:, P2:2 * P2]], axis=0).astype(BF16),
                    per_head_rows(cut(v_s, u))) for x, y, u in zip(kbkk, rbrk, units)]
        for u, x in zip(units, avrv):
            rv_s[u[0], rows_of(u[2]), lanes_of(u[1])] = x[C:2 * C]
        yield
        prow2, pcol2 = _tri_masks(2 * P2)
        t_p = yield from _unit_lower_inverses(
            [jnp.concatenate([kbkk[i][:, 0:P2], kbkk[i + 1][:, 0:P2]], axis=1) for i in range(0, len(units), 2)],
            ((prow2 >> CHUNK_LOG2) == (pcol2 >> CHUNK_LOG2)).astype(BF16))
        first2 = jnp.concatenate([first, first], axis=1)

        def per_head_rows2(x):
            zero = jnp.zeros_like(x)
            return jnp.concatenate([jnp.where(first2, x, zero), jnp.where(first2, zero, x)], axis=0)

        for i, u in enumerate(units):
            t_hi, t_lo = _split2(t_p[i // 2][:, (i % 2) * P2:(i % 2 + 1) * P2])
            x_hi, x_lo = _split2(jnp.concatenate([avrv[i][0:C], cut(kk_s, u)], axis=1))
            x_hi = per_head_rows2(x_hi)
            uw = _mm(t_hi, x_hi) + (_mm(t_hi, per_head_rows2(x_lo)) + _mm(t_lo, x_hi))
            u_s[u[0], rows_of(u[2]), lanes_of(u[1])] = uw[:, 0:P2]
            w_s[u[0], rows_of(u[2]), lanes_of(u[1])] = uw[:, P2:2 * P2]

    def recurrence(seqs):
        brow, bcol = _tri_masks(P2)
        blockdiag = (brow >> RWKV_HEAD_LOG2) == (bcol >> RWKV_HEAD_LOG2)
        pairs = [(bi, p) for bi in seqs for p in range(n_p)]
        for c in range(n_c):
            us = [(bi, p, c) for bi, p in pairs]
            ss = [state_ref[bi * n_p + p] for bi, p in pairs]
            rw = [_mm(jnp.concatenate([cut(r_s, u), cut(w_s, u).astype(BF16)], axis=0), s.astype(BF16))
                  for u, s in zip(us, ss)]
            pm = [-(x[C:2 * C] + cut(u_s, u)) for u, x in zip(us, rw)]
            upd = [_mm(jnp.concatenate([cut(bend_s, u), cut(kend_s, u)], axis=0),
                       jnp.concatenate([x.astype(BF16), cut(v_s, u)], axis=0), TN) for u, x in zip(us, pm)]
            for i, (bi, p) in enumerate(pairs):
                g_col = jnp.exp2(lct_s[bi, p * P2:(p + 1) * P2, c * C + C - 1:c * C + C])
                state_ref[bi * n_p + p] = ss[i] * g_col + jnp.where(blockdiag, upd[i], 0.0)
                y_s[bi, rows_of(c), lanes_of(p)] = rw[i][0:C] + cut(rv_s, us[i]) + _mm_bf(
                    arb_s[slot(us[i])], per_head_rows(pm[i]))

    _run(prepass(0))
    for bi in range(nb):
        _interleave(chunk_form([bi]), *([prepass(bi + 1)] if bi + 1 < nb else []))
    recurrence(list(range(nb)))

    inv_n = 1.0 / RWKV_HEAD_DIM
    for bi in range(nb):
        y = y_s[bi]
        mean = head_sums(y) * inv_n
        d = y - mean
        var = head_sums(d * d) * inv_n
        yn = d * lax.rsqrt(var + RWKV_GN_EPS) * lnw_ref[...] + lnb_ref[...]
        o_ref[bi] = ((yn + bonus_s[bi]) * gate_s[bi]).astype(o_ref.dtype)


def _rwkv(rw, w0, w2p, a0, a2p, g2, k_k, k_a, r_k, ln_w, ln_b, bsz, seq, tt=256, nb=4):
    cols = rw.shape[1]
    W = RWKV_WIDTH
    tile = lambda b, j: (b, j, 0)
    const = lambda b, j: (0, 0)
    vec = pl.BlockSpec((1, W), const)
    out = pl.pallas_call(
        functools.partial(_rwkv_kernel, tt=tt, nb=nb),
        out_shape=jax.ShapeDtypeStruct((bsz, seq, W), BF16),
        grid=(bsz // nb, seq // tt),
        in_specs=[
            pl.BlockSpec((nb, tt, cols), tile),
            vec,
            pl.BlockSpec(w2p.shape, const),
            vec,
            pl.BlockSpec(a2p.shape, const),
            pl.BlockSpec(g2.shape, const),
            vec, vec, vec, vec, vec,
        ],
        out_specs=pl.BlockSpec((nb, tt, W), tile),
        scratch_shapes=[pltpu.VMEM((nb, tt, W), dt) for dt in (BF16, BF16, BF16, F32, BF16, BF16, BF16) + (F32,) * 6]
        + [pltpu.VMEM((nb, W, tt), F32),
           pltpu.VMEM((nb * (RWKV_HEADS // 2) * (tt // CHUNK), CHUNK, 2 * RWKV_HEAD_DIM), F32),
           pltpu.VMEM((nb * RWKV_HEADS // 2, 2 * RWKV_HEAD_DIM, 2 * RWKV_HEAD_DIM), F32)],
        compiler_params=pltpu.CompilerParams(
            dimension_semantics=("arbitrary", "arbitrary"), vmem_limit_bytes=VMEM_LIMIT),
        name="rwkv7",
    )(rw.reshape(bsz, seq, cols), w0, w2p, a0, a2p, g2, k_k, k_a, r_k, ln_w, ln_b)
    return out.reshape(bsz * seq, W)


def _rglru_kernel(gb_ref, xb_ref, wa_ref, ba_ref, wx_ref, bx_ref, lam_ref, o_ref, h_ref, *, tt):
    @pl.when(pl.program_id(1) == 0)
    def _():
        h_ref[...] = jnp.zeros_like(h_ref)

    xc = xb_ref[...]

    blk = xc.shape[1] // LRU_BLOCKS
    xc_bf = xc.astype(BF16)
    ra = jnp.concatenate(
        [_mm(xc_bf[:, n * blk:(n + 1) * blk], wa_ref[n]) for n in range(LRU_BLOCKS)], axis=1)
    ix = jnp.concatenate(
        [_mm(xc_bf[:, n * blk:(n + 1) * blk], wx_ref[n]) for n in range(LRU_BLOCKS)], axis=1)
    r = _sigmoid(ra + ba_ref[...])
    i = _sigmoid(ix + bx_ref[...])
    a = jnp.exp2(r * ((-LRU_C * LOG2_E) * _softplus(-lam_ref[...])))
    u = xc * i * jnp.sqrt(1.0 - a * a)

    width = xc.shape[1]
    a = a.reshape(tt // SUB, SUB, width)
    u = u.reshape(tt // SUB, SUB, width)
    sub = lax.broadcasted_iota(jnp.int32, (1, SUB, 1), 1)
    d = 1
    while d < SUB:
        keep = sub >= d
        a_sh = jnp.where(keep, pltpu.roll(a, d, axis=1), 1.0)
        u_sh = jnp.where(keep, pltpu.roll(u, d, axis=1), 0.0)
        u = a * u_sh + u
        a = a * a_sh
        d *= 2
    gate = _gelu_tanh(gb_ref[...])
    carry = h_ref[...]
    pack = 2
    for g0 in range(0, tt // SUB, pack):
        outs = []
        for g in range(g0, g0 + pack):
            h = u[g] + a[g] * carry
            outs.append(h * gate[g * SUB:(g + 1) * SUB])
            carry = jnp.broadcast_to(h[SUB - 1:SUB, :], h.shape)
        o_ref[g0 * SUB:(g0 + pack) * SUB, :] = jnp.concatenate(outs, axis=0).astype(o_ref.dtype)
    h_ref[...] = carry


def _rglru(gb, xb, wa_bf, ba, wx_bf, bx, lam, bsz, seq, tt=512):
    n_tok, width = xb.shape
    per_b = seq // tt
    row = lambda b, j: (b * per_b + j, 0)
    const = lambda b, j: (0, 0)
    const3 = lambda b, j: (0, 0, 0)
    vec = pl.BlockSpec((1, width), const)
    return pl.pallas_call(
        functools.partial(_rglru_kernel, tt=tt),
        out_shape=jax.ShapeDtypeStruct((n_tok, width), BF16),
        grid=(bsz, per_b),
        in_specs=[
            pl.BlockSpec((tt, width), row),
            pl.BlockSpec((tt, width), row),
            pl.BlockSpec(wa_bf.shape, const3),
            vec,
            pl.BlockSpec(wx_bf.shape, const3),
            vec,
            vec,
        ],
        out_specs=pl.BlockSpec((tt, width), row),
        scratch_shapes=[pltpu.VMEM((HALO, width), F32)],
        compiler_params=pltpu.CompilerParams(
            dimension_semantics=("arbitrary", "arbitrary"), vmem_limit_bytes=VMEM_LIMIT),
        name="rglru",
    )(gb, xb, wa_bf, ba, wx_bf, bx, lam)


def _pad_lanes(v):
    return jnp.pad(v, (0, LANES - v.shape[0])).reshape(1, LANES)


def kernel(x, c, norm_pre, norm_post, ada_w, ada_b, ffn_w_gate, ffn_w_up, ffn_w_down, mix_w_in, mix_w_out, gdn_conv_w, gdn_a_log, gdn_dt_bias, gdn_norm_w, rwkv_mu, rwkv_w0, rwkv_w2, rwkv_a0, rwkv_a2, rwkv_g2, rwkv_k_k, rwkv_k_a, rwkv_r_k, rwkv_ln_w, rwkv_ln_b, lru_w_in, lru_conv_w, lru_conv_b, lru_wa, lru_ba, lru_wx, lru_bx, lru_lambda, lru_w_out):
    bsz, seq, d = x.shape
    depth = norm_pre.shape[0]
    x2 = x.reshape(bsz * seq, d)
    mods = _ada_params(c, ada_w, ada_b)

    def mod(layer, sub):
        m = mods[layer * 2 + sub]
        return (m[:, None, 0:d], m[:, None, d:2 * d], m[:, None, 2 * d:3 * d])

    GW = GDN_WIDTH
    for layer in range(depth):
        j = layer // 2
        shift, scale, gate = mod(layer, 0)
        shift2, scale2, gate2 = mod(layer, 1)
        mod_rows = jnp.concatenate([gate, shift2, scale2, gate2], axis=1)
        nw_rows = jnp.stack([norm_post[layer, 0], norm_pre[layer, 1], norm_post[layer, 1]])
        nw_pre = norm_pre[layer, 0].reshape(1, d)
        ffn_w = (ffn_w_gate[layer].astype(BF16), ffn_w_up[layer].astype(BF16), ffn_w_down[layer].astype(BF16))
        if layer % 2 == 0:
            w_in = mix_w_in[j].astype(BF16)
            n_gdn = 4 * GW + 2 * GDN_HEADS
            w_cat = jnp.concatenate(
                [w_in[:, 0:4 * GW], w_in[:, n_gdn:],
                 jnp.pad(w_in[:, 4 * GW:n_gdn], ((0, 0), (0, LANES - 2 * GDN_HEADS)))], axis=1)
            qkvz, rw, ab = _norm_proj(
                x2, nw_pre, shift, scale, w_cat, (4 * GW, RWKV_COLS, LANES),
                (("conv_silu", 3 * GW), ("shift_mix",), ("plain",)),
                (gdn_conv_w[j], rwkv_mu[j].reshape(1, RWKV_COLS)), seq)
            out_a = _gdn(qkvz, ab, _pad_lanes(gdn_a_log[j]), _pad_lanes(gdn_dt_bias[j]),
                         gdn_norm_w[j].reshape(1, GDN_HEAD_DIM), bsz, seq)
            w2p = jnp.pad(rwkv_w2[j], ((0, LANES - DECAY_LORA), (0, 0))).astype(BF16)
            a2p = jnp.pad(rwkv_a2[j], ((DECAY_LORA, LANES - DECAY_LORA - AAA_LORA), (0, 0))).astype(BF16)
            vec = lambda t: t.reshape(1, RWKV_WIDTH)
            out_b = _rwkv(rw, vec(rwkv_w0[j]), w2p, vec(rwkv_a0[j]), a2p,
                          rwkv_g2[j].astype(BF16), vec(rwkv_k_k[j]), vec(rwkv_k_a[j]), vec(rwkv_r_k[j]),
                          vec(rwkv_ln_w[j]), vec(rwkv_ln_b[j]), bsz, seq)
            w_out = mix_w_out[j].astype(BF16)
            x2 = _out_ffn(x2, [out_a, out_b], [w_out[0:GW], w_out[GW:]], mod_rows, nw_rows, *ffn_w, seq)
        else:
            width = lru_w_in.shape[2] // 2
            vec = lambda t: t.reshape(1, width)
            gb, xb = _norm_proj(x2, nw_pre, shift, scale, lru_w_in[j].astype(BF16), (width, width),
                                (("plain",), ("conv_bias",)), (lru_conv_w[j], vec(lru_conv_b[j])), seq)
            y = _rglru(gb, xb, lru_wa[j].astype(BF16), vec(lru_ba[j]),
                       lru_wx[j].astype(BF16), vec(lru_bx[j]), vec(lru_lambda[j]), bsz, seq)
            x2 = _out_ffn(x2, [y], [lru_w_out[j].astype(BF16)], mod_rows, nw_rows, *ffn_w, seq)
    return x2.reshape(bsz, seq, d)
```
